```python
import jax, jax.numpy as jnp
from jax import lax
import numpy as np

D_MODEL = 2048
BATCH = 1
SEQ = 8192
DEPTH = 1

N_HEADS = 8
HEAD_DIM = 128
N_KV = 2
HEADS_PER_KV = N_HEADS // N_KV
CMP_BLOCK = 32
CMP_STRIDE = 16
SEL_BLOCK = 64
N_SELECT = 16
WINDOW = 512
Q_BLOCK = 128
SGU_WIDTH = 1024
SGU_GROUPS = 8
SGU_GROUP_DIM = SGU_WIDTH // SGU_GROUPS
SGU_CHUNK = 128
D_FF = 5504
CONV_WIDTH = 3
NORM_EPS = 1e-6
LN_EPS = 1e-5
NEG_INF = -1e30
FORCE_SCORE = 1e6

Q_W = N_HEADS * HEAD_DIM
KV_W = N_KV * HEAD_DIM
NSA_GATE_W = 3 * N_HEADS
IN_SIZES = (Q_W, KV_W, KV_W, KV_W, KV_W, KV_W, KV_W, NSA_GATE_W, SGU_WIDTH, SGU_WIDTH, D_MODEL, D_MODEL)
IN_W = sum(IN_SIZES)

kernel_name = 'hybrid_nsa_gmlp_convffn_block'


def rms_norm(x, g):
    x32 = x.astype(jnp.float32)
    y = x32 * lax.rsqrt(jnp.mean(x32 * x32, -1, keepdims=True) + NORM_EPS)
    return (y * g.astype(jnp.float32)).astype(x.dtype)


def layer_norm(x, g, b):
    x32 = x.astype(jnp.float32)
    xc = x32 - jnp.mean(x32, -1, keepdims=True)
    y = xc * lax.rsqrt(jnp.mean(xc * xc, -1, keepdims=True) + LN_EPS)
    return (y * g.astype(jnp.float32) + b.astype(jnp.float32)).astype(x.dtype)


def masked_softmax(s, mask):
    p = jax.nn.softmax(jnp.where(mask, s, NEG_INF), axis=-1)
    return p * jnp.any(mask, -1, keepdims=True)


def alibi_slopes():
    return jnp.exp2(-8.0 * jnp.arange(1, N_HEADS + 1, dtype=jnp.float32) / N_HEADS)


def compress(kv, pe, w1, w2):
    B, T = kv.shape[:2]
    n_cmp = (T - CMP_BLOCK) // CMP_STRIDE + 1
    idx = jnp.arange(n_cmp)[:, None] * CMP_STRIDE + jnp.arange(CMP_BLOCK)[None, :]
    blocks = kv[:, idx] + pe[:, None, :]
    blocks = blocks.transpose(0, 3, 1, 2, 4).reshape(B, N_KV, n_cmp, CMP_BLOCK * HEAD_DIM)
    return jax.nn.gelu(blocks @ w1) @ w2


def nsa_attention(q, k_cmp, v_cmp, k_slc, v_slc, k_win, v_win, gates,
                  cmp_k_pe, cmp_k_w1, cmp_k_w2, cmp_v_pe, cmp_v_w1, cmp_v_w2):
    B, T = q.shape[:2]
    kc = compress(k_cmp, cmp_k_pe, cmp_k_w1, cmp_k_w2)
    vc = compress(v_cmp, cmp_v_pe, cmp_v_w1, cmp_v_w2)
    n_cmp = kc.shape[2]
    n_sel = T // SEL_BLOCK
    top_n = min(N_SELECT, n_sel)
    cmp_start = jnp.arange(n_cmp) * CMP_STRIDE
    cmp_end = cmp_start + CMP_BLOCK - 1
    sel_start = jnp.arange(n_sel) * SEL_BLOCK
    sel_end = sel_start + SEL_BLOCK - 1
    overlap = jnp.clip(jnp.minimum(cmp_end[:, None], sel_end[None]) - jnp.maximum(cmp_start[:, None], sel_start[None]) + 1, 0).astype(jnp.float32)
    ks = k_slc.reshape(B, n_sel, SEL_BLOCK, N_KV, HEAD_DIM).transpose(0, 3, 1, 2, 4)
    vs = v_slc.reshape(B, n_sel, SEL_BLOCK, N_KV, HEAD_DIM).transpose(0, 3, 1, 2, 4)
    pad = ((0, 0), (0, 0), (WINDOW, 0), (0, 0))
    kw = jnp.pad(k_win.transpose(0, 2, 1, 3), pad)
    vw = jnp.pad(v_win.transpose(0, 2, 1, 3), pad)
    n_qb = T // Q_BLOCK
    qb = q.reshape(B, n_qb, Q_BLOCK, N_KV, HEADS_PER_KV, HEAD_DIM).transpose(1, 0, 3, 4, 2, 5)
    slopes = alibi_slopes().reshape(N_KV, HEADS_PER_KV)[:, :, None, None]
    scale = HEAD_DIM ** -0.5
    b_ix = jnp.arange(B)[:, None, None, None]
    g_ix = jnp.arange(N_KV)[None, :, None, None]
    j_sel = jnp.arange(n_sel)

    def block_fn(args):
        qblk, i = args
        t = i * Q_BLOCK + jnp.arange(Q_BLOCK)
        qf = qblk * scale
        dist = (t[:, None] - cmp_end[None]).astype(jnp.float32)
        s = jnp.einsum('bghqd,bgnd->bghqn', qf, kc).astype(jnp.float32) - slopes * dist
        p_cmp = masked_softmax(s, dist >= 0)
        o_cmp = jnp.einsum('bghqn,bgnd->bghqd', p_cmp.astype(vc.dtype), vc)
        imp = jnp.einsum('bghqn,nj->bgqj', p_cmp, overlap)
        cur = t[:, None] // SEL_BLOCK
        valid = sel_start[None] <= t[:, None]
        forced = (j_sel[None] == 0) | (j_sel[None] == cur) | (j_sel[None] == cur - 1)
        imp = jnp.where(forced, FORCE_SCORE, jnp.where(valid, imp, NEG_INF))
        _, sel = lax.top_k(imp, top_n)
        kg = ks[b_ix, g_ix, sel].reshape(B, N_KV, Q_BLOCK, top_n * SEL_BLOCK, HEAD_DIM)
        vg = vs[b_ix, g_ix, sel].reshape(B, N_KV, Q_BLOCK, top_n * SEL_BLOCK, HEAD_DIM)
        pos = (sel[..., None] * SEL_BLOCK + jnp.arange(SEL_BLOCK)).reshape(B, N_KV, Q_BLOCK, top_n * SEL_BLOCK)
        dist = (t[:, None] - pos).astype(jnp.float32)[:, :, None]
        s = jnp.einsum('bghqd,bgqkd->bghqk', qf, kg).astype(jnp.float32) - slopes * dist
        p = masked_softmax(s, dist >= 0)
        o_slc = jnp.einsum('bghqk,bgqkd->bghqd', p.astype(vg.dtype), vg)
        kwb = lax.dynamic_slice_in_dim(kw, i * Q_BLOCK, WINDOW + Q_BLOCK, axis=2)
        vwb = lax.dynamic_slice_in_dim(vw, i * Q_BLOCK, WINDOW + Q_BLOCK, axis=2)
        spos = i * Q_BLOCK - WINDOW + jnp.arange(WINDOW + Q_BLOCK)
        d_int = t[:, None] - spos[None]
        mask = (d_int >= 0) & (d_int < WINDOW) & (spos[None] >= 0)
        s = jnp.einsum('bghqd,bgkd->bghqk', qf, kwb).astype(jnp.float32) - slopes * d_int.astype(jnp.float32)
        p = masked_softmax(s, mask)
        o_win = jnp.einsum('bghqk,bgkd->bghqd', p.astype(vwb.dtype), vwb)
        return o_cmp, o_slc, o_win

    o_cmp, o_slc, o_win = lax.map(block_fn, (qb, jnp.arange(n_qb)))
    to_bthd = lambda o: o.transpose(1, 0, 4, 2, 3, 5).reshape(B, T, N_HEADS, HEAD_DIM)
    g = jax.nn.sigmoid(gates)
    o = g[..., 0:1] * to_bthd(o_cmp) + g[..., 1:2] * to_bthd(o_slc) + g[..., 2:3] * to_bthd(o_win)
    return o.reshape(B, T, Q_W)


def spatial_gating(u, v, ln_g, ln_b, w_s, b_s):
    B, T, _ = u.shape
    v = layer_norm(v, ln_g, ln_b)
    n_chunk = T // SGU_CHUNK
    v = v.reshape(B, n_chunk, SGU_CHUNK, SGU_GROUPS, SGU_GROUP_DIM)
    causal = jnp.tril(jnp.ones((SGU_CHUNK, SGU_CHUNK), dtype=bool))
    w = jnp.where(causal, w_s, 0)
    s = jnp.einsum('gts,bnsgd->bntgd', w, v) + b_s.T[None, None, :, :, None]
    return u * s.reshape(B, T, SGU_WIDTH)


def causal_dwconv(a, w, b):
    y = lax.conv_general_dilated(a, w[:, None, :], window_strides=(1,), padding=[(CONV_WIDTH - 1, 0)],
                                 dimension_numbers=('NWC', 'WIO', 'NWC'), feature_group_count=a.shape[-1])
    return y + b


def hybrid_layer(x, norm1_g, w_in, cmp_k_pe, cmp_k_w1, cmp_k_w2, cmp_v_pe, cmp_v_w1, cmp_v_w2,
                 sgu_ln_g, sgu_ln_b, sgu_w, sgu_b, w_branch_a, w_branch_b, w_out,
                 norm2_g, ffn_w_in, ffn_conv_w, ffn_conv_b, ffn_w_down):
    B, T, _ = x.shape
    offsets = np.cumsum(IN_SIZES)[:-1].tolist()
    h = rms_norm(x, norm1_g)
    (q, k_cmp, v_cmp, k_slc, v_slc, k_win, v_win, nsa_g, sgu_u, sgu_v, gate_a, gate_b) = jnp.split(h @ w_in, offsets, axis=-1)
    kvh = lambda t: t.reshape(B, T, N_KV, HEAD_DIM)
    y_a = nsa_attention(q.reshape(B, T, N_HEADS, HEAD_DIM), kvh(k_cmp), kvh(v_cmp), kvh(k_slc), kvh(v_slc),
                        kvh(k_win), kvh(v_win), nsa_g.reshape(B, T, N_HEADS, 3),
                        cmp_k_pe, cmp_k_w1, cmp_k_w2, cmp_v_pe, cmp_v_w1, cmp_v_w2)
    y_b = spatial_gating(jax.nn.gelu(sgu_u), jax.nn.gelu(sgu_v), sgu_ln_g, sgu_ln_b, sgu_w, sgu_b)
    mixed = jax.nn.sigmoid(gate_a) * (y_a @ w_branch_a) + jax.nn.sigmoid(gate_b) * (y_b @ w_branch_b)
    x = x + mixed @ w_out
    h2 = rms_norm(x, norm2_g)
    a, b = jnp.split(h2 @ ffn_w_in, 2, axis=-1)
    return x + (jax.nn.gelu(causal_dwconv(a, ffn_conv_w, ffn_conv_b)) * b) @ ffn_w_down


def setup_inputs(seed: int = 0) -> dict:
    key = jax.random.key(seed)
    ks = jax.random.split(key, 24)
    f32 = jnp.float32
    L = DEPTH
    nrm = lambda k, shape, s: jax.random.normal(k, shape, f32) * s
    return {
        'x': nrm(ks[0], (BATCH, SEQ, D_MODEL), 1.0),
        'norm1_g': 1.0 + nrm(ks[1], (L, D_MODEL), 0.02),
        'w_in': nrm(ks[2], (L, D_MODEL, IN_W), D_MODEL ** -0.5),
        'cmp_k_pe': nrm(ks[3], (L, CMP_BLOCK, HEAD_DIM), 0.1),
        'cmp_k_w1': nrm(ks[4], (L, CMP_BLOCK * HEAD_DIM, HEAD_DIM), (CMP_BLOCK * HEAD_DIM) ** -0.5),
        'cmp_k_w2': nrm(ks[5], (L, HEAD_DIM, HEAD_DIM), HEAD_DIM ** -0.5),
        'cmp_v_pe': nrm(ks[6], (L, CMP_BLOCK, HEAD_DIM), 0.1),
        'cmp_v_w1': nrm(ks[7], (L, CMP_BLOCK * HEAD_DIM, HEAD_DIM), (CMP_BLOCK * HEAD_DIM) ** -0.5),
        'cmp_v_w2': nrm(ks[8], (L, HEAD_DIM, HEAD_DIM), HEAD_DIM ** -0.5),
        'sgu_ln_g': 1.0 + nrm(ks[9], (L, SGU_WIDTH), 0.02),
        'sgu_ln_b': nrm(ks[10], (L, SGU_WIDTH), 0.02),
        'sgu_w': nrm(ks[11], (L, SGU_GROUPS, SGU_CHUNK, SGU_CHUNK), SGU_CHUNK ** -0.5),
        'sgu_b': 1.0 + nrm(ks[12], (L, SGU_GROUPS, SGU_CHUNK), 0.02),
        'w_branch_a': nrm(ks[13], (L, Q_W, D_MODEL), Q_W ** -0.5),
        'w_branch_b': nrm(ks[14], (L, SGU_WIDTH, D_MODEL), SGU_WIDTH ** -0.5),
        'w_out': nrm(ks[15], (L, D_MODEL, D_MODEL), D_MODEL ** -0.5),
        'norm2_g': 1.0 + nrm(ks[16], (L, D_MODEL), 0.02),
        'ffn_w_in': nrm(ks[17], (L, D_MODEL, 2 * D_FF), D_MODEL ** -0.5),
        'ffn_conv_w': nrm(ks[18], (L, CONV_WIDTH, D_FF), CONV_WIDTH ** -0.5),
        'ffn_conv_b': nrm(ks[19], (L, D_FF), 0.02),
        'ffn_w_down': nrm(ks[20], (L, D_FF, D_MODEL), D_FF ** -0.5),
        'final_g': 1.0 + nrm(ks[21], (D_MODEL,), 0.02),
    }


def reference(x, norm1_g, w_in, cmp_k_pe, cmp_k_w1, cmp_k_w2, cmp_v_pe, cmp_v_w1, cmp_v_w2,
              sgu_ln_g, sgu_ln_b, sgu_w, sgu_b, w_branch_a, w_branch_b, w_out,
              norm2_g, ffn_w_in, ffn_conv_w, ffn_conv_b, ffn_w_down, final_g):
    for l in range(DEPTH):
        x = hybrid_layer(x, norm1_g[l], w_in[l], cmp_k_pe[l], cmp_k_w1[l], cmp_k_w2[l],
                         cmp_v_pe[l], cmp_v_w1[l], cmp_v_w2[l], sgu_ln_g[l], sgu_ln_b[l],
                         sgu_w[l], sgu_b[l], w_branch_a[l], w_branch_b[l], w_out[l],
                         norm2_g[l], ffn_w_in[l], ffn_conv_w[l], ffn_conv_b[l], ffn_w_down[l])
    return rms_norm(x, final_g)
```

```python
import functools

import numpy as np
import jax
import jax.numpy as jnp
from jax import lax
from jax.experimental import pallas as pl
from jax.experimental.pallas import tpu as pltpu

F32 = jnp.float32
BF16 = jnp.bfloat16

D_MODEL = 2048
N_HEADS = 8
HEAD_DIM = 128
N_KV = 2
HEADS_PER_KV = N_HEADS // N_KV
CMP_BLOCK = 32
CMP_STRIDE = 16
SEL_BLOCK = 64
N_SELECT = 16
WINDOW = 512
Q_BLOCK = 128
SGU_WIDTH = 1024
SGU_GROUPS = 8
SGU_GROUP_DIM = SGU_WIDTH // SGU_GROUPS
SGU_CHUNK = 128
D_FF = 5504
D_FF_PAD = 5632
CONV_WIDTH = 3
NORM_EPS = 1e-6
LN_EPS = 1e-5
NEG_INF = -1e30
FORCE_SCORE = 1e6

Q_W = N_HEADS * HEAD_DIM
KV_W = N_KV * HEAD_DIM
NSA_GATE_W = 3 * N_HEADS
IN_SIZES = (Q_W, KV_W, KV_W, KV_W, KV_W, KV_W, KV_W, NSA_GATE_W, SGU_WIDTH, SGU_WIDTH, D_MODEL, D_MODEL)

LANES = 128
QROWS = HEADS_PER_KV * Q_BLOCK
SLC_CHUNK = 512
SLC_SHIFT = SLC_CHUNK.bit_length() - 1
SEL_SHIFT = SEL_BLOCK.bit_length() - 1
WIN_KEYS = WINDOW + Q_BLOCK
PROJ_TN = 512
N16_TILES = 4
FFN_TN = 512
VMEM_LIMIT = 56 * 1024 * 1024


def _cparams(sem):
    return pltpu.CompilerParams(dimension_semantics=sem, vmem_limit_bytes=VMEM_LIMIT)


def _dot(a, b):
    return jnp.dot(a, b, preferred_element_type=F32)


def _dot_nt(a, b):
    return lax.dot_general(a, b, (((1,), (1,)), ((), ())), preferred_element_type=F32)


def _rms(x, g):
    return x * lax.rsqrt(jnp.mean(x * x, -1, keepdims=True) + NORM_EPS) * g


def _in_proj_kernel(x_ref, g_ref, w_ref, wg_ref, cs_ref, o16_ref, o32_ref, ong_ref, h_scr):
    j = pl.program_id(1)

    @pl.when(j == 0)
    def _():
        h = _rms(x_ref[...], g_ref[...]).astype(BF16)
        h_scr[...] = h
        ong_ref[...] = _dot(h, wg_ref[...])

    acc = _dot(h_scr[...], w_ref[...])

    @pl.when(j < N16_TILES)
    def _():
        o16_ref[...] = (acc * cs_ref[...]).astype(BF16)

    @pl.when(j >= N16_TILES)
    def _():
        o32_ref[...] = acc


def _in_proj(x, g, w, wg, cs):
    t = x.shape[0]
    n = w.shape[1]
    tm = min(1024, t)
    nj = n // PROJ_TN
    n16 = N16_TILES * PROJ_TN
    return pl.pallas_call(
        _in_proj_kernel,
        grid=(t // tm, nj),
        in_specs=[
            pl.BlockSpec((tm, D_MODEL), lambda i, j: (i, 0)),
            pl.BlockSpec((1, D_MODEL), lambda i, j: (0, 0)),
            pl.BlockSpec((D_MODEL, PROJ_TN), lambda i, j: (0, j)),
            pl.BlockSpec((D_MODEL, 2 * LANES), lambda i, j: (0, 0)),
            pl.BlockSpec((1, PROJ_TN), lambda i, j: (0, j)),
        ],
        out_specs=[
            pl.BlockSpec((tm, PROJ_TN), lambda i, j: (i, jnp.minimum(j, N16_TILES - 1))),
            pl.BlockSpec((tm, PROJ_TN), lambda i, j: (i, jnp.maximum(j - N16_TILES, 0))),
            pl.BlockSpec((tm, 2 * LANES), lambda i, j: (i, 0)),
        ],
        out_shape=[
            jax.ShapeDtypeStruct((t, n16), BF16),
            jax.ShapeDtypeStruct((t, n - n16), F32),
            jax.ShapeDtypeStruct((t, 2 * LANES), F32),
        ],
        scratch_shapes=[pltpu.VMEM((tm, D_MODEL), BF16)],
        compiler_params=_cparams(("arbitrary", "arbitrary")),
        name="in_proj",
    )(x, g, w, wg, cs)


def _compress_kernel(u_ref, pe_ref, w1_ref, w2_ref, o_ref):
    u = u_ref[...]
    nu = u.shape[0]
    a = _dot((u + pe_ref[0]).astype(BF16), w1_ref[0])
    b = _dot((u + pe_ref[1]).astype(BF16), w1_ref[1])
    hid = a + pltpu.roll(b, nu - 1, 0)
    o_ref[...] = _dot(jax.nn.gelu(hid).astype(BF16), w2_ref[...]).astype(BF16)


def _compress(u, pe, w1, w2):
    nu = u.shape[2]
    half = CMP_STRIDE * HEAD_DIM
    return pl.pallas_call(
        _compress_kernel,
        grid=(2, N_KV),
        in_specs=[
            pl.BlockSpec((None, None, nu, half), lambda a, g: (a, g, 0, 0)),
            pl.BlockSpec((None, 2, 1, half), lambda a, g: (a, 0, 0, 0)),
            pl.BlockSpec((None, 2, half, HEAD_DIM), lambda a, g: (a, 0, 0, 0)),
            pl.BlockSpec((None, HEAD_DIM, HEAD_DIM), lambda a, g: (a, 0, 0)),
        ],
        out_specs=pl.BlockSpec((None, None, nu, HEAD_DIM), lambda a, g: (a, g, 0, 0)),
        out_shape=jax.ShapeDtypeStruct((2, N_KV, nu, HEAD_DIM), BF16),
        compiler_params=_cparams(("arbitrary", "arbitrary")),
        name="compress",
    )(u, pe, w1, w2)


def _masked_softmax(s, valid):
    m = jnp.max(s, axis=1, keepdims=True)
    e = jnp.where(valid, jnp.exp(s - m), 0.0)
    l = jnp.sum(e, axis=1, keepdims=True)
    return e * jnp.where(l > 0.0, 1.0 / l, 0.0)


def _lane_tile(x, n):
    return jnp.concatenate([x] * n, axis=1)


def _nsa_kernel(q_ref, ks_ref, vs_ref, kw_ref, vw_ref, kc_ref, vc_ref, ng_ref,
                sl_ref, d0c_ref, d0s_ref, d0w_ref, et_ref, ov_ref, o_ref):
    i = pl.program_id(1)
    t0 = i * Q_BLOCK
    q = q_ref[...]
    q4 = jnp.concatenate([q[:, h * HEAD_DIM:(h + 1) * HEAD_DIM] for h in range(HEADS_PER_KV)], axis=0)
    sl = sl_ref[...]
    nc = kc_ref.shape[0]

    d = d0c_ref[...] + t0.astype(F32)
    valid = d >= 0.0
    s = jnp.where(valid, _dot_nt(q4, kc_ref[...]) - _lane_tile(sl, nc // LANES) * d, NEG_INF)
    p = _masked_softmax(s, valid)
    o_cmp = _dot(p.astype(BF16), vc_ref[...])
    pcat = jnp.concatenate([p[h * Q_BLOCK:(h + 1) * Q_BLOCK] for h in range(HEADS_PER_KV)], axis=1)
    imp = _dot(pcat.astype(BF16), ov_ref[...])

    jl = lax.broadcasted_iota(jnp.int32, (Q_BLOCK, LANES), 1)
    tq = t0 + lax.broadcasted_iota(jnp.int32, (Q_BLOCK, LANES), 0)
    cur = lax.shift_right_logical(tq, SEL_SHIFT)
    forced = (jl == 0) | (jl == cur) | (jl == cur - 1)
    score = jnp.where(forced, FORCE_SCORE, jnp.where(jl * SEL_BLOCK <= tq, imp, NEG_INF))
    jf = jl.astype(F32)
    sel = jnp.zeros((Q_BLOCK, LANES), F32)
    for _ in range(N_SELECT):
        mx = jnp.max(score, axis=1, keepdims=True)
        idx = jnp.min(jnp.where(score == mx, jf, float(LANES)), axis=1, keepdims=True)
        hit = jf == idx
        sel = jnp.where(hit, 1.0, sel)
        score = jnp.where(hit, -jnp.inf, score)
    sel16 = sel.astype(BF16)

    d0s = d0s_ref[...]
    sls = _lane_tile(sl, SLC_CHUNK // LANES)

    def slc_step(c, carry):
        m, l, acc = carry
        k0 = pl.multiple_of(c * SLC_CHUNK, SLC_CHUNK)
        s = _dot_nt(q4, ks_ref[pl.ds(k0, SLC_CHUNK), :])
        selk = _dot_nt(sel16, et_ref[pl.ds(k0, SLC_CHUNK), :])
        selk = jnp.concatenate([selk] * HEADS_PER_KV, axis=0)
        d = d0s + (t0 - k0).astype(F32)
        valid = (d >= 0.0) & (selk > 0.5)
        s = jnp.where(valid, s - sls * d, NEG_INF)
        m_new = jnp.maximum(m, jnp.max(s, axis=1, keepdims=True))
        alpha = jnp.exp(m - m_new)
        e = jnp.where(valid, jnp.exp(s - m_new), 0.0)
        l = alpha * l + jnp.sum(e, axis=1, keepdims=True)
        acc = alpha * acc + _dot(e.astype(BF16), vs_ref[pl.ds(k0, SLC_CHUNK), :])
        return m_new, l, acc

    n_chunks = lax.shift_right_logical(t0 + (Q_BLOCK + SLC_CHUNK - 1), SLC_SHIFT)
    m, l, acc = lax.fori_loop(
        0, n_chunks, slc_step,
        (jnp.full((QROWS, 1), NEG_INF, F32), jnp.zeros((QROWS, 1), F32), jnp.zeros((QROWS, HEAD_DIM), F32)))
    o_slc = acc * jnp.where(l > 0.0, 1.0 / l, 0.0)

    w0 = pl.multiple_of(jnp.maximum(t0 - WINDOW, 0), Q_BLOCK)
    d = d0w_ref[...] + (t0 - w0).astype(F32)
    valid = (d >= 0.0) & (d < float(WINDOW))
    s = jnp.where(valid, _dot_nt(q4, kw_ref[pl.ds(w0, WIN_KEYS), :]) - _lane_tile(sl, WIN_KEYS // LANES) * d, NEG_INF)
    o_win = _dot(_masked_softmax(s, valid).astype(BF16), vw_ref[pl.ds(w0, WIN_KEYS), :])

    sg = jax.nn.sigmoid(ng_ref[...])

    def gate(col):
        return jnp.sum(jnp.where(jl == col, sg, 0.0), axis=1, keepdims=True)

    outs = []
    for h in range(HEADS_PER_KV):
        rows = slice(h * Q_BLOCK, (h + 1) * Q_BLOCK)
        outs.append(gate(3 * h) * o_cmp[rows] + gate(3 * h + 1) * o_slc[rows] + gate(3 * h + 2) * o_win[rows])
    o_ref[...] = jnp.concatenate(outs, axis=1).astype(BF16)


def _nsa_tables(t):
    nc = t // CMP_STRIDE
    r = (np.arange(QROWS) % Q_BLOCK)[:, None].astype(np.float32)
    d0c = r - (np.arange(nc)[None, :] * CMP_STRIDE + CMP_BLOCK - 1)
    d0s = r - np.arange(SLC_CHUNK)[None, :]
    d0w = r - np.arange(WIN_KEYS)[None, :]
    slopes = np.exp2(-8.0 * np.arange(1, N_HEADS + 1, dtype=np.float32) / N_HEADS).astype(np.float32)
    sl = np.repeat(slopes.reshape(N_KV, HEADS_PER_KV), Q_BLOCK, axis=1)[:, :, None] * np.ones((1, 1, LANES), np.float32)
    et = (np.arange(t)[:, None] // SEL_BLOCK == np.arange(LANES)[None, :]).astype(np.float32)
    cmp_start = np.arange(nc) * CMP_STRIDE
    sel_start = np.arange(LANES) * SEL_BLOCK
    ov = np.clip(np.minimum(cmp_start[:, None] + CMP_BLOCK - 1, sel_start[None] + SEL_BLOCK - 1)
                 - np.maximum(cmp_start[:, None], sel_start[None]) + 1, 0, None).astype(np.float32)
    ov[nc - 1] = 0.0
    ov = np.tile(ov, (HEADS_PER_KV, 1))
    return (jnp.asarray(sl), jnp.asarray(d0c.astype(np.float32)), jnp.asarray(d0s.astype(np.float32)),
            jnp.asarray(d0w.astype(np.float32)), jnp.asarray(et, BF16), jnp.asarray(ov, BF16))


def _nsa(o16, kvc, ng, t):
    sl, d0c, d0s, d0w, et, ov = _nsa_tables(t)
    nc = t // CMP_STRIDE
    qcols = Q_W // LANES

    def kvspec(base):
        return pl.BlockSpec((t, HEAD_DIM), lambda g, i, base=base: (0, base + g))

    const2 = lambda g, i: (0, 0)
    return pl.pallas_call(
        _nsa_kernel,
        grid=(N_KV, t // Q_BLOCK),
        in_specs=[
            pl.BlockSpec((Q_BLOCK, HEADS_PER_KV * HEAD_DIM), lambda g, i: (i, g)),
            kvspec(qcols), kvspec(qcols + 2), kvspec(qcols + 4), kvspec(qcols + 6),
            pl.BlockSpec((None, None, nc, HEAD_DIM), lambda g, i: (0, g, 0, 0)),
            pl.BlockSpec((None, None, nc, HEAD_DIM), lambda g, i: (1, g, 0, 0)),
            pl.BlockSpec((Q_BLOCK, LANES), lambda g, i: (i, g)),
            pl.BlockSpec((None, QROWS, LANES), lambda g, i: (g, 0, 0)),
            pl.BlockSpec((QROWS, nc), const2),
            pl.BlockSpec((QROWS, SLC_CHUNK), const2),
            pl.BlockSpec((QROWS, WIN_KEYS), const2),
            pl.BlockSpec((t, LANES), const2),
            pl.BlockSpec((HEADS_PER_KV * nc, LANES), const2),
        ],
        out_specs=pl.BlockSpec((Q_BLOCK, HEADS_PER_KV * HEAD_DIM), lambda g, i: (i, g)),
        out_shape=jax.ShapeDtypeStruct((t, Q_W), BF16),
        compiler_params=_cparams(("arbitrary", "arbitrary")),
        name="nsa",
    )(o16, o16, o16, o16, o16, kvc, kvc, ng, sl, d0c, d0s, d0w, et, ov)


def _sgu_kernel(u_ref, v_ref, lng_ref, lnb_ref, ws_ref, bs_ref, o_ref):
    gu = jax.nn.gelu(u_ref[...])
    gv = jax.nn.gelu(v_ref[...])
    xc = gv - jnp.mean(gv, -1, keepdims=True)
    vn = (xc * lax.rsqrt(jnp.mean(xc * xc, -1, keepdims=True) + LN_EPS) * lng_ref[...] + lnb_ref[...]).astype(BF16)
    causal = (lax.broadcasted_iota(jnp.int32, (SGU_CHUNK, SGU_CHUNK), 0)
              >= lax.broadcasted_iota(jnp.int32, (SGU_CHUNK, SGU_CHUNK), 1))
    outs = []
    for g in range(SGU_GROUPS):
        w = jnp.where(causal, ws_ref[g], 0.0).astype(BF16)
        outs.append(_dot(w, vn[:, g * SGU_GROUP_DIM:(g + 1) * SGU_GROUP_DIM]) + bs_ref[g])
    o_ref[...] = (gu * jnp.concatenate(outs, axis=1)).astype(BF16)


def _sgu(o32, lng, lnb, ws, bs, t):
    return pl.pallas_call(
        _sgu_kernel,
        grid=(t // SGU_CHUNK,),
        in_specs=[
            pl.BlockSpec((SGU_CHUNK, SGU_WIDTH), lambda i: (i, 0)),
            pl.BlockSpec((SGU_CHUNK, SGU_WIDTH), lambda i: (i, 1)),
            pl.BlockSpec((1, SGU_WIDTH), lambda i: (0, 0)),
            pl.BlockSpec((1, SGU_WIDTH), lambda i: (0, 0)),
            pl.BlockSpec((SGU_GROUPS, SGU_CHUNK, SGU_CHUNK), lambda i: (0, 0, 0)),
            pl.BlockSpec((SGU_GROUPS, SGU_CHUNK, LANES), lambda i: (0, 0, 0)),
        ],
        out_specs=pl.BlockSpec((SGU_CHUNK, SGU_WIDTH), lambda i: (i, 0)),
        out_shape=jax.ShapeDtypeStruct((t, SGU_WIDTH), BF16),
        compiler_params=_cparams(("arbitrary",)),
        name="sgu",
    )(o32, o32, lng, lnb, ws, bs)


def _mix_kernel(ya_ref, yb_ref, ga_ref, gb_ref, x_ref, pa_ref, pb_ref, wo_ref, g2_ref, x1_ref, h2_ref):
    mixed = (jax.nn.sigmoid(ga_ref[...]) * _dot(ya_ref[...], pa_ref[...])
             + jax.nn.sigmoid(gb_ref[...]) * _dot(yb_ref[...], pb_ref[...]))
    x1 = x_ref[...] + _dot(mixed.astype(BF16), wo_ref[...])
    x1_ref[...] = x1
    h2_ref[...] = _rms(x1, g2_ref[...]).astype(BF16)


def _mix(ya, yb, o32, x, pa, pb, wo, g2, t):
    tm = min(256, t)
    once = pl.Buffered(1)
    return pl.pallas_call(
        _mix_kernel,
        grid=(t // tm,),
        in_specs=[
            pl.BlockSpec((tm, Q_W), lambda i: (i, 0)),
            pl.BlockSpec((tm, SGU_WIDTH), lambda i: (i, 0)),
            pl.BlockSpec((tm, D_MODEL), lambda i: (i, 1)),
            pl.BlockSpec((tm, D_MODEL), lambda i: (i, 2)),
            pl.BlockSpec((tm, D_MODEL), lambda i: (i, 0)),
            pl.BlockSpec((Q_W, D_MODEL), lambda i: (0, 0), pipeline_mode=once),
            pl.BlockSpec((SGU_WIDTH, D_MODEL), lambda i: (0, 0), pipeline_mode=once),
            pl.BlockSpec((D_MODEL, D_MODEL), lambda i: (0, 0), pipeline_mode=once),
            pl.BlockSpec((1, D_MODEL), lambda i: (0, 0)),
        ],
        out_specs=[
            pl.BlockSpec((tm, D_MODEL), lambda i: (i, 0)),
            pl.BlockSpec((tm, D_MODEL), lambda i: (i, 0)),
        ],
        out_shape=[jax.ShapeDtypeStruct((t, D_MODEL), F32), jax.ShapeDtypeStruct((t, D_MODEL), BF16)],
        compiler_params=_cparams(("arbitrary",)),
        name="mix_out",
    )(ya, yb, o32, o32, x, pa, pb, wo, g2)


def _ffn_in_kernel(h_ref, wa_ref, wb_ref, cw_ref, cb_ref, o_ref, a_scr, carry_scr):
    i = pl.program_id(0)
    j = pl.program_id(1)
    tm = h_ref.shape[0]
    h = h_ref[...]
    a = _dot(h, wa_ref[...])
    b = _dot(h, wb_ref[...])
    a_scr[0:8, :] = jnp.where(i > 0, carry_scr[j], 0.0)
    a_scr[8:8 + tm, :] = a
    carry_scr[j] = a[tm - 8:tm, :]
    cw = cw_ref[...]
    y = (cw[0:1] * a_scr[pl.ds(6, tm), :] + cw[1:2] * a_scr[pl.ds(7, tm), :] + cw[2:3] * a + cb_ref[...])
    o_ref[...] = (jax.nn.gelu(y) * b).astype(BF16)


def _ffn_in(h2, wa, wb, cw, cb, t):
    tm = min(1024, t)
    nj = D_FF_PAD // FFN_TN
    return pl.pallas_call(
        _ffn_in_kernel,
        grid=(t // tm, nj),
        in_specs=[
            pl.BlockSpec((tm, D_MODEL), lambda i, j: (i, 0)),
            pl.BlockSpec((D_MODEL, FFN_TN), lambda i, j: (0, j)),
            pl.BlockSpec((D_MODEL, FFN_TN), lambda i, j: (0, j)),
            pl.BlockSpec((8, FFN_TN), lambda i, j: (0, j)),
            pl.BlockSpec((1, FFN_TN), lambda i, j: (0, j)),
        ],
        out_specs=pl.BlockSpec((tm, FFN_TN), lambda i, j: (i, j)),
        out_shape=jax.ShapeDtypeStruct((t, D_FF_PAD), BF16),
        scratch_shapes=[pltpu.VMEM((tm + 8, FFN_TN), F32), pltpu.VMEM((nj, 8, FFN_TN), F32)],
        compiler_params=_cparams(("arbitrary", "arbitrary")),
        name="ffn_in",
    )(h2, wa, wb, cw, cb)


def _ffn_out_kernel(a_ref, wd_ref, x_ref, g_ref, o_ref, *, final_norm):
    x2 = x_ref[...] + _dot(a_ref[...], wd_ref[...])
    o_ref[...] = _rms(x2, g_ref[...]) if final_norm else x2


def _ffn_out(act, wd, x1, g, t, final_norm):
    tm = min(256, t)
    return pl.pallas_call(
        functools.partial(_ffn_out_kernel, final_norm=final_norm),
        grid=(t // tm,),
        in_specs=[
            pl.BlockSpec((tm, D_FF_PAD), lambda i: (i, 0)),
            pl.BlockSpec((D_FF_PAD, D_MODEL), lambda i: (0, 0), pipeline_mode=pl.Buffered(1)),
            pl.BlockSpec((tm, D_MODEL), lambda i: (i, 0)),
            pl.BlockSpec((1, D_MODEL), lambda i: (0, 0)),
        ],
        out_specs=pl.BlockSpec((tm, D_MODEL), lambda i: (i, 0)),
        out_shape=jax.ShapeDtypeStruct((t, D_MODEL), F32),
        compiler_params=_cparams(("arbitrary",)),
        name="ffn_out",
    )(act, wd, x1, g)


def _layer(x, p, final_g, t):
    offs = np.cumsum((0,) + IN_SIZES)
    w_in = p["w_in"]
    col = lambda k: w_in[:, offs[k]:offs[k + 1]]
    w_main = jnp.concatenate([col(0), col(3), col(4), col(5), col(6), col(8), col(9), col(10), col(11), col(1), col(2)],
                             axis=1).astype(BF16)
    wg = col(7).reshape(D_MODEL, N_KV, HEADS_PER_KV * 3)
    wg = jnp.pad(wg, ((0, 0), (0, 0), (0, LANES - HEADS_PER_KV * 3))).reshape(D_MODEL, N_KV * LANES).astype(BF16)
    cs = jnp.concatenate([jnp.full((1, Q_W), HEAD_DIM ** -0.5, F32), jnp.ones((1, w_main.shape[1] - Q_W), F32)], axis=1)
    o16, o32, ng = _in_proj(x, p["norm1_g"][None, :], w_main, wg, cs)

    kvc_cols = o32[:, 3 * D_MODEL:]
    u = kvc_cols.reshape(t // CMP_STRIDE, CMP_STRIDE, 2, N_KV, HEAD_DIM).transpose(2, 3, 0, 1, 4)
    u = u.reshape(2, N_KV, t // CMP_STRIDE, CMP_STRIDE * HEAD_DIM)
    pe = jnp.stack([p["cmp_k_pe"], p["cmp_v_pe"]]).reshape(2, 2, 1, CMP_STRIDE * HEAD_DIM)
    w1 = jnp.stack([p["cmp_k_w1"], p["cmp_v_w1"]]).reshape(2, 2, CMP_STRIDE * HEAD_DIM, HEAD_DIM).astype(BF16)
    w2 = jnp.stack([p["cmp_k_w2"], p["cmp_v_w2"]]).astype(BF16)
    kvc = _compress(u, pe, w1, w2)

    y_a = _nsa(o16, kvc, ng, t)
    bs = jnp.broadcast_to(p["sgu_b"][:, :, None], (SGU_GROUPS, SGU_CHUNK, LANES))
    y_b = _sgu(o32, p["sgu_ln_g"][None, :], p["sgu_ln_b"][None, :], p["sgu_w"], bs, t)
    x1, h2 = _mix(y_a, y_b, o32, x, p["w_branch_a"].astype(BF16), p["w_branch_b"].astype(BF16),
                  p["w_out"].astype(BF16), p["norm2_g"][None, :], t)

    padc = ((0, 0), (0, D_FF_PAD - D_FF))
    wa = jnp.pad(p["ffn_w_in"][:, :D_FF], padc).astype(BF16)
    wb = jnp.pad(p["ffn_w_in"][:, D_FF:], padc).astype(BF16)
    cw = jnp.pad(p["ffn_conv_w"], ((0, 8 - CONV_WIDTH), (0, D_FF_PAD - D_FF)))
    cb = jnp.pad(p["ffn_conv_b"][None, :], padc)
    act = _ffn_in(h2, wa, wb, cw, cb, t)
    wd = jnp.pad(p["ffn_w_down"], ((0, D_FF_PAD - D_FF), (0, 0))).astype(BF16)
    g = final_g[None, :] if final_g is not None else jnp.ones((1, D_MODEL), F32)
    return _ffn_out(act, wd, x1, g, t, final_g is not None)


def kernel(x, norm1_g, w_in, cmp_k_pe, cmp_k_w1, cmp_k_w2, cmp_v_pe, cmp_v_w1, cmp_v_w2, sgu_ln_g, sgu_ln_b, sgu_w, sgu_b, w_branch_a, w_branch_b, w_out, norm2_g, ffn_w_in, ffn_conv_w, ffn_conv_b, ffn_w_down, final_g):
    b, t, _ = x.shape
    assert b == 1 and t % 1024 == 0, "one sequence whose length is a multiple of 1024"
    params = dict(norm1_g=norm1_g, w_in=w_in, cmp_k_pe=cmp_k_pe, cmp_k_w1=cmp_k_w1, cmp_k_w2=cmp_k_w2,
                  cmp_v_pe=cmp_v_pe, cmp_v_w1=cmp_v_w1, cmp_v_w2=cmp_v_w2, sgu_ln_g=sgu_ln_g, sgu_ln_b=sgu_ln_b,
                  sgu_w=sgu_w, sgu_b=sgu_b, w_branch_a=w_branch_a, w_branch_b=w_branch_b, w_out=w_out,
                  norm2_g=norm2_g, ffn_w_in=ffn_w_in, ffn_conv_w=ffn_conv_w, ffn_conv_b=ffn_conv_b,
                  ffn_w_down=ffn_w_down)
    depth = norm1_g.shape[0]
    h = x[0]
    for l in range(depth):
        layer = {k: v[l] for k, v in params.items()}
        h = _layer(h, layer, final_g if l == depth - 1 else None, t)
    return h[None]
```

```python
import functools

import numpy as np
import jax
import jax.numpy as jnp
from jax import lax
from jax.experimental import pallas as pl
from jax.experimental.pallas import tpu as pltpu

F32 = jnp.float32
BF16 = jnp.bfloat16

D_MODEL = 2048
N_HEADS = 8
HEAD_DIM = 128
N_KV = 2
HEADS_PER_KV = N_HEADS // N_KV
CMP_BLOCK = 32
CMP_STRIDE = 16
SEL_BLOCK = 64
N_SELECT = 16
N_FORCED = 3
WINDOW = 512
Q_BLOCK = 128
SGU_WIDTH = 1024
SGU_GROUPS = 8
SGU_GROUP_DIM = SGU_WIDTH // SGU_GROUPS
SGU_CHUNK = 128
D_FF = 5504
CONV_WIDTH = 3
NORM_EPS = 1e-6
LN_EPS = 1e-5
NEG_INF = -1e30

Q_W = N_HEADS * HEAD_DIM
KV_W = N_KV * HEAD_DIM
NSA_GATE_W = 3 * N_HEADS
IN_SIZES = (Q_W, KV_W, KV_W, KV_W, KV_W, KV_W, KV_W, NSA_GATE_W, SGU_WIDTH, SGU_WIDTH, D_MODEL, D_MODEL)

LANES = 128
SUBLANES = 8
QROWS = HEADS_PER_KV * Q_BLOCK
SEL_SHIFT = SEL_BLOCK.bit_length() - 1
BLOCKS_PER_Q = Q_BLOCK // SEL_BLOCK
SLC_BLOCKS = 8
SLC_CHUNK = SLC_BLOCKS * SEL_BLOCK
SLC_UNROLL = 2
WIN_KEYS = WINDOW + Q_BLOCK
PROJ_TN = 512
N16_TILES = 4
FFN_TN = 512
VMEM_LIMIT = 56 * 1024 * 1024


def _cparams(sem):
    return pltpu.CompilerParams(dimension_semantics=sem, vmem_limit_bytes=VMEM_LIMIT)


def _dot(a, b):
    return jnp.dot(a, b, preferred_element_type=F32)


def _dot_nt(a, b):
    return lax.dot_general(a, b, (((1,), (1,)), ((), ())), preferred_element_type=F32)


def _rms(x, g):
    return x * lax.rsqrt(jnp.mean(x * x, -1, keepdims=True) + NORM_EPS) * g


def _in_proj_kernel(x_ref, g_ref, w_ref, wg_ref, cs_ref, o16_ref, o32_ref, ong_ref, h_scr):
    j = pl.program_id(1)

    @pl.when(j == 0)
    def _():
        h = _rms(x_ref[...], g_ref[...]).astype(BF16)
        h_scr[...] = h
        ong_ref[...] = _dot(h, wg_ref[...])

    acc = _dot(h_scr[...], w_ref[...])

    @pl.when(j < N16_TILES)
    def _():
        o16_ref[...] = (acc * cs_ref[...]).astype(BF16)

    @pl.when(j >= N16_TILES)
    def _():
        o32_ref[...] = acc


def _in_proj(x, g, w, wg, cs):
    t = x.shape[0]
    n = w.shape[1]
    tm = min(1024, t)
    nj = n // PROJ_TN
    n16 = N16_TILES * PROJ_TN
    return pl.pallas_call(
        _in_proj_kernel,
        grid=(t // tm, nj),
        in_specs=[
            pl.BlockSpec((tm, D_MODEL), lambda i, j: (i, 0)),
            pl.BlockSpec((1, D_MODEL), lambda i, j: (0, 0)),
            pl.BlockSpec((D_MODEL, PROJ_TN), lambda i, j: (0, j)),
            pl.BlockSpec((D_MODEL, 2 * LANES), lambda i, j: (0, 0)),
            pl.BlockSpec((1, PROJ_TN), lambda i, j: (0, j)),
        ],
        out_specs=[
            pl.BlockSpec((tm, PROJ_TN), lambda i, j: (i, jnp.minimum(j, N16_TILES - 1))),
            pl.BlockSpec((tm, PROJ_TN), lambda i, j: (i, jnp.maximum(j - N16_TILES, 0))),
            pl.BlockSpec((tm, 2 * LANES), lambda i, j: (i, 0)),
        ],
        out_shape=[
            jax.ShapeDtypeStruct((t, n16), BF16),
            jax.ShapeDtypeStruct((t, n - n16), F32),
            jax.ShapeDtypeStruct((t, 2 * LANES), F32),
        ],
        scratch_shapes=[pltpu.VMEM((tm, D_MODEL), BF16)],
        compiler_params=_cparams(("arbitrary", "arbitrary")),
        name="in_proj",
    )(x, g, w, wg, cs)


def _compress_kernel(kv_ref, pe_ref, w1_ref, w2_ref, o_ref):
    nu = o_ref.shape[0]
    a = jnp.zeros((nu, HEAD_DIM), F32)
    b = jnp.zeros((nu, HEAD_DIM), F32)
    for r in range(CMP_STRIDE):
        x = kv_ref[pl.ds(r, nu, stride=CMP_STRIDE), :]
        a += _dot((x + pe_ref[pl.ds(r, 1), :]).astype(BF16), w1_ref[r])
        b += _dot((x + pe_ref[pl.ds(CMP_STRIDE + r, 1), :]).astype(BF16), w1_ref[CMP_STRIDE + r])
    hid = a + pltpu.roll(b, nu - 1, 0)
    o_ref[...] = _dot(jax.nn.gelu(hid).astype(BF16), w2_ref[...]).astype(BF16)


def _compress(o32, col0, pe, w1, w2, t):
    nu = t // CMP_STRIDE
    return pl.pallas_call(
        _compress_kernel,
        grid=(2, N_KV),
        in_specs=[
            pl.BlockSpec((t, HEAD_DIM), lambda a, g: (0, col0 + N_KV * a + g)),
            pl.BlockSpec((None, CMP_BLOCK, HEAD_DIM), lambda a, g: (a, 0, 0)),
            pl.BlockSpec((None, CMP_BLOCK, HEAD_DIM, HEAD_DIM), lambda a, g: (a, 0, 0, 0)),
            pl.BlockSpec((None, HEAD_DIM, HEAD_DIM), lambda a, g: (a, 0, 0)),
        ],
        out_specs=pl.BlockSpec((None, None, nu, HEAD_DIM), lambda a, g: (a, g, 0, 0)),
        out_shape=jax.ShapeDtypeStruct((2, N_KV, nu, HEAD_DIM), BF16),
        compiler_params=_cparams(("arbitrary", "arbitrary")),
        name="compress",
    )(o32, pe, w1, w2)


def _lane_tile(x, n):
    return jnp.concatenate([x] * n, axis=1)


def _row_tile(x, n):
    return jnp.concatenate([x] * n, axis=0)


def _softmax_numerators(s):
    return jnp.exp(s - jnp.max(s, axis=1, keepdims=True)).astype(BF16)


def _stack_heads(q, g):
    return jnp.concatenate([q[:, (g * HEADS_PER_KV + h) * HEAD_DIM:(g * HEADS_PER_KV + h + 1) * HEAD_DIM]
                            for h in range(HEADS_PER_KV)], axis=0)


def _nsa_select_kernel(q_ref, kvc_ref, bc_ref, sj_ref, slr_ref, ov_ref, lt_ref, ocmp_ref, aug_ref, lists_ref):
    i = pl.program_id(0)
    t0 = i * Q_BLOCK
    t0f = t0.astype(F32)
    q = q_ref[...]
    nc = kvc_ref.shape[2]
    row = lax.broadcasted_iota(jnp.int32, (QROWS, 1), 0) & (Q_BLOCK - 1)
    jr = lax.broadcasted_iota(jnp.int32, (LANES, Q_BLOCK), 0)
    kl = lax.broadcasted_iota(jnp.int32, (LANES, Q_BLOCK), 1)
    tq = t0 + kl
    cur = lax.shift_right_logical(tq, SEL_SHIFT)
    forced = (jr == 0) | (jr == cur) | (jr == cur - 1)
    visible = jr * SEL_BLOCK <= tq
    jf = jr.astype(F32)

    ocmp, rows = [], []
    for g in range(N_KV):
        bc = bc_ref[g]
        s = jnp.where(bc <= _lane_tile(slr_ref[g] * t0f, nc // LANES),
                      _dot_nt(_stack_heads(q, g), kvc_ref[0, g]) + bc, NEG_INF)
        r = _dot(_softmax_numerators(s), jnp.concatenate([kvc_ref[1, g], ov_ref[...]], axis=1))
        l = jnp.sum(r[:, HEAD_DIM:], axis=1, keepdims=True) * (1.0 / CMP_BLOCK)
        inv = jnp.where(t0 + row >= CMP_BLOCK - 1, 1.0 / l, 0.0)
        o = r[:, :HEAD_DIM] * inv
        ocmp += [o[h * Q_BLOCK:(h + 1) * Q_BLOCK] for h in range(HEADS_PER_KV)]
        imp = r[:, HEAD_DIM:] * inv
        imp = sum(imp[h * Q_BLOCK:(h + 1) * Q_BLOCK] for h in range(HEADS_PER_KV))

        score = jnp.where(forced | jnp.logical_not(visible), NEG_INF, imp.T)
        sel_t = jnp.where(forced, 1.0, 0.0)
        for _ in range(N_SELECT - N_FORCED):
            mx = jnp.max(score, axis=0, keepdims=True)
            idx = jnp.min(jnp.where(score == mx, jf, float(LANES)), axis=0, keepdims=True)
            hit = jf == idx
            sel_t = jnp.where(hit, 1.0, sel_t)
            score = jnp.where(hit, -jnp.inf, score)
        sel_t = jnp.where(visible, sel_t, 0.0)
        aug_ref[g] = jnp.where(_row_tile(sel_t.T, HEADS_PER_KV) > 0.5, sj_ref[g], NEG_INF).astype(BF16)

        used = jnp.max(jnp.where(jr < BLOCKS_PER_Q * i, sel_t, 0.0), axis=1, keepdims=True)
        used = jnp.broadcast_to(used, (LANES, Q_BLOCK))
        slot = _dot(lt_ref[...], used.astype(BF16))
        lst = jnp.sum(jnp.where((slot == kl.astype(F32)) & (used > 0.5), jf, 0.0), axis=0, keepdims=True)
        cnt = jnp.sum(used, axis=0, keepdims=True)
        lst = jnp.where(kl[:1].astype(F32) < cnt, lst, (BLOCKS_PER_Q * i).astype(F32))
        rows += [lst, cnt]
    ocmp_ref[...] = jnp.concatenate(ocmp, axis=1)
    pad = [jnp.zeros((1, LANES), F32)] * (SUBLANES - len(rows))
    lists_ref[...] = jnp.concatenate(rows[0::2] + rows[1::2] + pad, axis=0).astype(jnp.int32)


def _nsa_attend_kernel(lists_ref, counts_ref, q_ref, ks_ref, vs_ref, kw_ref, vw_ref, aug_ref, ocmp_ref, ng_ref,
                       bw_ref, b1_ref, b1d_ref, et_ref, o_ref):
    g = pl.program_id(0)
    i = pl.program_id(1)
    t0 = i * Q_BLOCK
    q4 = _stack_heads(q_ref[...], 0)
    ones_col = jnp.ones((WIN_KEYS, LANES), BF16)

    s = _dot_nt(q4, kw_ref[pl.ds(t0, WIN_KEYS), :]) + bw_ref[...]
    s = jnp.concatenate(
        [s[:, b * Q_BLOCK:(b + 1) * Q_BLOCK] + jnp.where(b < WINDOW // Q_BLOCK - i, NEG_INF, 0.0)
         for b in range(WINDOW // Q_BLOCK)] + [s[:, WINDOW:]], axis=1)
    r = _dot(_softmax_numerators(s), jnp.concatenate([vw_ref[pl.ds(t0, WIN_KEYS), :], ones_col], axis=1))
    o_win = r[:, :HEAD_DIM] * (1.0 / r[:, HEAD_DIM:])

    aug = aug_ref[...]
    jl = lax.broadcasted_iota(jnp.int32, (QROWS, LANES), 1)
    qx = jnp.concatenate([q4, aug], axis=1)
    qx_past = jnp.concatenate([q4, jnp.where(jl < BLOCKS_PER_Q * i, aug, NEG_INF).astype(BF16)], axis=1)
    b1 = _lane_tile(b1_ref[...], SLC_CHUNK // LANES)
    lrow = g * pl.num_programs(1) + i

    def gathered(ref, first):
        parts = []
        for k in range(SLC_BLOCKS):
            k0 = pl.multiple_of(lists_ref[lrow, first + k] * SEL_BLOCK, SEL_BLOCK)
            parts.append(ref[pl.ds(k0, SEL_BLOCK), :])
        return jnp.concatenate(parts, axis=0)

    def slc_chunk(first, m, acc):
        kx = jnp.concatenate([gathered(ks_ref, first), gathered(et_ref, first)], axis=1)
        u = _dot_nt(qx_past, kx) + b1
        m_new = jnp.maximum(m, jnp.max(u, axis=1, keepdims=True))
        e = jnp.exp(u - m_new).astype(BF16)
        vx = jnp.concatenate([gathered(vs_ref, first), ones_col[:SLC_CHUNK]], axis=1)
        return m_new, jnp.exp(m - m_new) * acc + _dot(e, vx)

    def slc_step(c, carry):
        for s_ in range(SLC_UNROLL):
            carry = slc_chunk((c * SLC_UNROLL + s_) * SLC_BLOCKS, *carry)
        return carry

    per_trip = SLC_UNROLL * SLC_BLOCKS
    n_trips = lax.shift_right_logical(counts_ref[lrow] + (per_trip - 1), per_trip.bit_length() - 1)
    m, acc = lax.fori_loop(0, n_trips, slc_step,
                           (jnp.full((QROWS, 1), NEG_INF, F32), jnp.zeros((QROWS, 2 * HEAD_DIM), F32)))
    kx = jnp.concatenate([ks_ref[pl.ds(t0, Q_BLOCK), :], et_ref[pl.ds(t0, Q_BLOCK), :]], axis=1)
    u = _dot_nt(qx, kx) + b1d_ref[...]
    m_new = jnp.maximum(m, jnp.max(u, axis=1, keepdims=True))
    vx = jnp.concatenate([vs_ref[pl.ds(t0, Q_BLOCK), :], ones_col[:Q_BLOCK]], axis=1)
    acc = jnp.exp(m - m_new) * acc + _dot(jnp.exp(u - m_new).astype(BF16), vx)
    o_slc = acc[:, :HEAD_DIM] * (1.0 / acc[:, HEAD_DIM:])

    sg = jax.nn.sigmoid(ng_ref[...])
    gl = lax.broadcasted_iota(jnp.int32, (Q_BLOCK, LANES), 1)
    o_cmp = ocmp_ref[...]

    def gate(col):
        return jnp.sum(jnp.where(gl == col, sg, 0.0), axis=1, keepdims=True)

    outs = []
    for h in range(HEADS_PER_KV):
        rows = slice(h * Q_BLOCK, (h + 1) * Q_BLOCK)
        outs.append(gate(3 * h) * o_cmp[:, h * HEAD_DIM:(h + 1) * HEAD_DIM]
                    + gate(3 * h + 1) * o_slc[rows] + gate(3 * h + 2) * o_win[rows])
    o_ref[...] = jnp.concatenate(outs, axis=1).astype(BF16)


def _nsa_tables(t):
    nc = t // CMP_STRIDE
    f32 = np.float32
    r = (np.arange(QROWS) % Q_BLOCK)[:, None].astype(f32)
    slopes = np.exp2(-8.0 * np.arange(1, N_HEADS + 1, dtype=f32) / N_HEADS).astype(f32)
    slr = np.repeat(slopes.reshape(N_KV, HEADS_PER_KV), Q_BLOCK, axis=1)[:, :, None]
    bc = slr * ((np.arange(nc) * CMP_STRIDE + CMP_BLOCK - 1)[None, None, :] - r[None])
    dw = r - np.arange(WIN_KEYS)[None, :] + WINDOW
    bw = np.where((dw >= 0) & (dw < WINDOW), -slr * dw[None], f32(NEG_INF))
    kk = np.arange(LANES)[None, :]
    b1 = slr * (kk % SEL_BLOCK)[None]
    b1d = np.where(kk <= r, b1, f32(NEG_INF))
    sj = slr * (SEL_BLOCK * kk)[None]
    et = (np.arange(t)[:, None] // SEL_BLOCK == kk)
    cmp_start = np.arange(nc) * CMP_STRIDE
    sel_start = np.arange(LANES) * SEL_BLOCK
    ov = np.clip(np.minimum(cmp_start[:, None] + CMP_BLOCK - 1, sel_start[None] + SEL_BLOCK - 1)
                 - np.maximum(cmp_start[:, None], sel_start[None]) + 1, 0, None)
    ov[nc - 1] = 0
    lt = np.tril(np.ones((LANES, LANES)), -1)
    slr = slr * np.ones((1, 1, LANES), f32)
    as_f32 = lambda a: jnp.asarray(a.astype(f32))
    as_bf16 = lambda a: jnp.asarray(a.astype(f32), BF16)
    return dict(bc=as_f32(bc), bw=as_f32(bw), b1=as_f32(b1), b1d=as_f32(b1d), sj=as_f32(sj), slr=as_f32(slr),
                et=as_bf16(et), ov=as_bf16(ov), lt=as_bf16(lt))


def _nsa(o16, kvw, kvc, ng, t):
    tb = _nsa_tables(t)
    nc = t // CMP_STRIDE
    nqb = t // Q_BLOCK
    qcols = Q_W // LANES
    whole = lambda a: pl.BlockSpec(a.shape, lambda i: (0,) * a.ndim)
    ocmp, aug, lists = pl.pallas_call(
        _nsa_select_kernel,
        grid=(nqb,),
        in_specs=[
            pl.BlockSpec((Q_BLOCK, Q_W), lambda i: (i, 0)),
            whole(kvc), whole(tb["bc"]), whole(tb["sj"]), whole(tb["slr"]), whole(tb["ov"]), whole(tb["lt"]),
        ],
        out_specs=[
            pl.BlockSpec((Q_BLOCK, Q_W), lambda i: (i, 0)),
            pl.BlockSpec((N_KV, QROWS, LANES), lambda i: (0, i, 0)),
            pl.BlockSpec((None, SUBLANES, LANES), lambda i: (i, 0, 0)),
        ],
        out_shape=[
            jax.ShapeDtypeStruct((t, Q_W), F32),
            jax.ShapeDtypeStruct((N_KV, nqb * QROWS, LANES), BF16),
            jax.ShapeDtypeStruct((nqb, SUBLANES, LANES), jnp.int32),
        ],
        compiler_params=_cparams(("arbitrary",)),
        name="nsa_select",
    )(o16, kvc, tb["bc"], tb["sj"], tb["slr"], tb["ov"], tb["lt"])
    block_lists = lists[:, :N_KV, :].transpose(1, 0, 2).reshape(N_KV * nqb, LANES)
    counts = lists[:, N_KV:2 * N_KV, 0].T.reshape(N_KV * nqb)

    per_group = lambda *shape: pl.BlockSpec((None,) + shape, lambda g, i, *_: (g,) + (0,) * len(shape))
    grid_spec = pltpu.PrefetchScalarGridSpec(
        num_scalar_prefetch=2,
        grid=(N_KV, nqb),
        in_specs=[
            pl.BlockSpec((Q_BLOCK, HEADS_PER_KV * HEAD_DIM), lambda g, i, *_: (i, g)),
            pl.BlockSpec((t, HEAD_DIM), lambda g, i, *_: (0, qcols + g)),
            pl.BlockSpec((t, HEAD_DIM), lambda g, i, *_: (0, qcols + 2 + g)),
            pl.BlockSpec((t + WINDOW, HEAD_DIM), lambda g, i, *_: (0, g)),
            pl.BlockSpec((t + WINDOW, HEAD_DIM), lambda g, i, *_: (0, 2 + g)),
            pl.BlockSpec((None, QROWS, LANES), lambda g, i, *_: (g, i, 0)),
            pl.BlockSpec((Q_BLOCK, HEADS_PER_KV * HEAD_DIM), lambda g, i, *_: (i, g)),
            pl.BlockSpec((Q_BLOCK, LANES), lambda g, i, *_: (i, g)),
            per_group(QROWS, WIN_KEYS), per_group(QROWS, LANES), per_group(QROWS, LANES),
            pl.BlockSpec((t, LANES), lambda g, i, *_: (0, 0)),
        ],
        out_specs=pl.BlockSpec((Q_BLOCK, HEADS_PER_KV * HEAD_DIM), lambda g, i, *_: (i, g)),
    )
    return pl.pallas_call(
        _nsa_attend_kernel,
        grid_spec=grid_spec,
        out_shape=jax.ShapeDtypeStruct((t, Q_W), BF16),
        compiler_params=_cparams(("arbitrary", "arbitrary")),
        name="nsa_attend",
    )(block_lists, counts, o16, o16, o16, kvw, kvw, aug, ocmp, ng, tb["bw"], tb["b1"], tb["b1d"], tb["et"])


def _sgu_kernel(u_ref, v_ref, lng_ref, lnb_ref, ws_ref, bs_ref, o_ref):
    gu = jax.nn.gelu(u_ref[...])
    gv = jax.nn.gelu(v_ref[...])
    xc = gv - jnp.mean(gv, -1, keepdims=True)
    vn = (xc * lax.rsqrt(jnp.mean(xc * xc, -1, keepdims=True) + LN_EPS) * lng_ref[...] + lnb_ref[...]).astype(BF16)
    causal = (lax.broadcasted_iota(jnp.int32, (SGU_CHUNK, SGU_CHUNK), 0)
              >= lax.broadcasted_iota(jnp.int32, (SGU_CHUNK, SGU_CHUNK), 1))
    outs = []
    for g in range(SGU_GROUPS):
        w = jnp.where(causal, ws_ref[g], 0.0).astype(BF16)
        outs.append(_dot(w, vn[:, g * SGU_GROUP_DIM:(g + 1) * SGU_GROUP_DIM]) + bs_ref[g])
    o_ref[...] = (gu * jnp.concatenate(outs, axis=1)).astype(BF16)


def _sgu(o32, lng, lnb, ws, bs, t):
    return pl.pallas_call(
        _sgu_kernel,
        grid=(t // SGU_CHUNK,),
        in_specs=[
            pl.BlockSpec((SGU_CHUNK, SGU_WIDTH), lambda i: (i, 0)),
            pl.BlockSpec((SGU_CHUNK, SGU_WIDTH), lambda i: (i, 1)),
            pl.BlockSpec((1, SGU_WIDTH), lambda i: (0, 0)),
            pl.BlockSpec((1, SGU_WIDTH), lambda i: (0, 0)),
            pl.BlockSpec((SGU_GROUPS, SGU_CHUNK, SGU_CHUNK), lambda i: (0, 0, 0)),
            pl.BlockSpec((SGU_GROUPS, SGU_CHUNK, LANES), lambda i: (0, 0, 0)),
        ],
        out_specs=pl.BlockSpec((SGU_CHUNK, SGU_WIDTH), lambda i: (i, 0)),
        out_shape=jax.ShapeDtypeStruct((t, SGU_WIDTH), BF16),
        compiler_params=_cparams(("arbitrary",)),
        name="sgu",
    )(o32, o32, lng, lnb, ws, bs)


def _mix_kernel(ya_ref, yb_ref, ga_ref, gb_ref, x_ref, pa_ref, pb_ref, wo_ref, g2_ref, x1_ref, h2_ref):
    mixed = (jax.nn.sigmoid(ga_ref[...]) * _dot(ya_ref[...], pa_ref[...])
             + jax.nn.sigmoid(gb_ref[...]) * _dot(yb_ref[...], pb_ref[...]))
    x1 = x_ref[...] + _dot(mixed.astype(BF16), wo_ref[...])
    x1_ref[...] = x1
    h2_ref[...] = _rms(x1, g2_ref[...]).astype(BF16)


def _mix(ya, yb, o32, x, pa, pb, wo, g2, t):
    tm = min(256, t)
    once = pl.Buffered(1)
    return pl.pallas_call(
        _mix_kernel,
        grid=(t // tm,),
        in_specs=[
            pl.BlockSpec((tm, Q_W), lambda i: (i, 0)),
            pl.BlockSpec((tm, SGU_WIDTH), lambda i: (i, 0)),
            pl.BlockSpec((tm, D_MODEL), lambda i: (i, 1)),
            pl.BlockSpec((tm, D_MODEL), lambda i: (i, 2)),
            pl.BlockSpec((tm, D_MODEL), lambda i: (i, 0)),
            pl.BlockSpec((Q_W, D_MODEL), lambda i: (0, 0), pipeline_mode=once),
            pl.BlockSpec((SGU_WIDTH, D_MODEL), lambda i: (0, 0), pipeline_mode=once),
            pl.BlockSpec((D_MODEL, D_MODEL), lambda i: (0, 0), pipeline_mode=once),
            pl.BlockSpec((1, D_MODEL), lambda i: (0, 0)),
        ],
        out_specs=[
            pl.BlockSpec((tm, D_MODEL), lambda i: (i, 0)),
            pl.BlockSpec((tm, D_MODEL), lambda i: (i, 0)),
        ],
        out_shape=[jax.ShapeDtypeStruct((t, D_MODEL), F32), jax.ShapeDtypeStruct((t, D_MODEL), BF16)],
        compiler_params=_cparams(("arbitrary",)),
        name="mix_out",
    )(ya, yb, o32, o32, x, pa, pb, wo, g2)


def _ffn_in_kernel(h_ref, wa_ref, wb_ref, cw_ref, cb_ref, o_ref, a_scr, carry_scr):
    i = pl.program_id(0)
    j = pl.program_id(1)
    tm = h_ref.shape[0]
    h = h_ref[...]
    a = _dot(h, wa_ref[...])
    b = _dot(h, wb_ref[...])
    a_scr[0:8, :] = jnp.where(i > 0, carry_scr[j], 0.0)
    a_scr[8:8 + tm, :] = a
    carry_scr[j] = a[tm - 8:tm, :]
    cw = cw_ref[...]
    y = (cw[0:1] * a_scr[pl.ds(6, tm), :] + cw[1:2] * a_scr[pl.ds(7, tm), :] + cw[2:3] * a + cb_ref[...])
    o_ref[...] = (jax.nn.gelu(y) * b).astype(BF16)


def _ffn_in(h2, w, cw, cb, t):
    tm = min(1024, t)
    nj = pl.cdiv(D_FF, FFN_TN)
    back = (nj * FFN_TN - D_FF) // LANES
    col = lambda j, base=0: (base // LANES + j * (FFN_TN // LANES) - (j // (nj - 1)) * back) * LANES
    tile = lambda rows, base: pl.BlockSpec((pl.Element(rows), pl.Element(FFN_TN)), lambda i, j: (0, col(j, base)))
    return pl.pallas_call(
        _ffn_in_kernel,
        grid=(t // tm, nj),
        in_specs=[
            pl.BlockSpec((tm, D_MODEL), lambda i, j: (i, 0)),
            tile(D_MODEL, 0),
            tile(D_MODEL, D_FF),
            tile(SUBLANES, 0),
            tile(1, 0),
        ],
        out_specs=pl.BlockSpec((pl.Element(tm), pl.Element(FFN_TN)), lambda i, j: (i * tm, col(j))),
        out_shape=jax.ShapeDtypeStruct((t, D_FF), BF16),
        scratch_shapes=[pltpu.VMEM((tm + 8, FFN_TN), F32), pltpu.VMEM((nj, 8, FFN_TN), F32)],
        compiler_params=_cparams(("arbitrary", "arbitrary")),
        name="ffn_in",
    )(h2, w, w, cw, cb)


def _ffn_out_kernel(a_ref, wd_ref, x_ref, g_ref, o_ref, *, final_norm):
    x2 = x_ref[...] + _dot(a_ref[...], wd_ref[...])
    o_ref[...] = _rms(x2, g_ref[...]) if final_norm else x2


def _ffn_out(act, wd, x1, g, t, final_norm):
    tm = min(256, t)
    return pl.pallas_call(
        functools.partial(_ffn_out_kernel, final_norm=final_norm),
        grid=(t // tm,),
        in_specs=[
            pl.BlockSpec((tm, D_FF), lambda i: (i, 0)),
            pl.BlockSpec((D_FF, D_MODEL), lambda i: (0, 0), pipeline_mode=pl.Buffered(1)),
            pl.BlockSpec((tm, D_MODEL), lambda i: (i, 0)),
            pl.BlockSpec((1, D_MODEL), lambda i: (0, 0)),
        ],
        out_specs=pl.BlockSpec((tm, D_MODEL), lambda i: (i, 0)),
        out_shape=jax.ShapeDtypeStruct((t, D_MODEL), F32),
        compiler_params=_cparams(("arbitrary",)),
        name="ffn_out",
    )(act, wd, x1, g)


def _layer(x, p, final_g, t):
    offs = np.cumsum((0,) + IN_SIZES)
    w_in = p["w_in"]
    col = lambda k: w_in[:, offs[k]:offs[k + 1]]
    w_main = jnp.concatenate([col(0), col(3), col(4), col(5), col(6), col(8), col(9), col(10), col(11), col(1), col(2)],
                             axis=1).astype(BF16)
    wg = col(7).reshape(D_MODEL, N_KV, HEADS_PER_KV * 3)
    wg = jnp.pad(wg, ((0, 0), (0, 0), (0, LANES - HEADS_PER_KV * 3))).reshape(D_MODEL, N_KV * LANES).astype(BF16)
    cs = jnp.concatenate([jnp.full((1, Q_W), HEAD_DIM ** -0.5, F32), jnp.ones((1, w_main.shape[1] - Q_W), F32)], axis=1)
    o16, o32, ng = _in_proj(x, p["norm1_g"][None, :], w_main, wg, cs)

    pe = jnp.stack([p["cmp_k_pe"], p["cmp_v_pe"]])
    w1 = jnp.stack([p["cmp_k_w1"], p["cmp_v_w1"]]).reshape(2, CMP_BLOCK, HEAD_DIM, HEAD_DIM).astype(BF16)
    w2 = jnp.stack([p["cmp_k_w2"], p["cmp_v_w2"]]).astype(BF16)
    kvc = _compress(o32, 3 * D_MODEL // HEAD_DIM, pe, w1, w2, t)

    kvw = jnp.pad(o16[:, Q_W + 2 * KV_W:], ((WINDOW, 0), (0, 0)))
    y_a = _nsa(o16, kvw, kvc, ng, t)
    bs = jnp.broadcast_to(p["sgu_b"][:, :, None], (SGU_GROUPS, SGU_CHUNK, LANES))
    y_b = _sgu(o32, p["sgu_ln_g"][None, :], p["sgu_ln_b"][None, :], p["sgu_w"], bs, t)
    x1, h2 = _mix(y_a, y_b, o32, x, p["w_branch_a"].astype(BF16), p["w_branch_b"].astype(BF16),
                  p["w_out"].astype(BF16), p["norm2_g"][None, :], t)

    cw = jnp.pad(p["ffn_conv_w"], ((0, SUBLANES - CONV_WIDTH), (0, 0)))
    act = _ffn_in(h2, p["ffn_w_in"].astype(BF16), cw, p["ffn_conv_b"][None, :], t)
    wd = p["ffn_w_down"].astype(BF16)
    g = final_g[None, :] if final_g is not None else jnp.ones((1, D_MODEL), F32)
    return _ffn_out(act, wd, x1, g, t, final_g is not None)


def kernel(x, norm1_g, w_in, cmp_k_pe, cmp_k_w1, cmp_k_w2, cmp_v_pe, cmp_v_w1, cmp_v_w2, sgu_ln_g, sgu_ln_b, sgu_w, sgu_b, w_branch_a, w_branch_b, w_out, norm2_g, ffn_w_in, ffn_conv_w, ffn_conv_b, ffn_w_down, final_g):
    b, t, _ = x.shape
    assert b == 1 and t % 1024 == 0, "one sequence whose length is a multiple of 1024"
    params = dict(norm1_g=norm1_g, w_in=w_in, cmp_k_pe=cmp_k_pe, cmp_k_w1=cmp_k_w1, cmp_k_w2=cmp_k_w2,
                  cmp_v_pe=cmp_v_pe, cmp_v_w1=cmp_v_w1, cmp_v_w2=cmp_v_w2, sgu_ln_g=sgu_ln_g, sgu_ln_b=sgu_ln_b,
                  sgu_w=sgu_w, sgu_b=sgu_b, w_branch_a=w_branch_a, w_branch_b=w_branch_b, w_out=w_out,
                  norm2_g=norm2_g, ffn_w_in=ffn_w_in, ffn_conv_w=ffn_conv_w, ffn_conv_b=ffn_conv_b,
                  ffn_w_down=ffn_w_down)
    depth = norm1_g.shape[0]
    h = x[0]
    for l in range(depth):
        layer = {k: v[l] for k, v in params.items()}
        h = _layer(h, layer, final_g if l == depth - 1 else None, t)
    return h[None]
```

```python
import functools

import numpy as np
import jax
import jax.numpy as jnp
from jax import lax
from jax.experimental import pallas as pl
from jax.experimental.pallas import tpu as pltpu

F32 = jnp.float32
BF16 = jnp.bfloat16

D_MODEL = 2048
N_HEADS = 8
HEAD_DIM = 128
N_KV = 2
HEADS_PER_KV = N_HEADS // N_KV
CMP_BLOCK = 32
CMP_STRIDE = 16
SEL_BLOCK = 64
N_SELECT = 16
N_FORCED = 3
WINDOW = 512
Q_BLOCK = 128
SGU_WIDTH = 1024
SGU_GROUPS = 8
SGU_GROUP_DIM = SGU_WIDTH // SGU_GROUPS
SGU_CHUNK = 128
D_FF = 5504
CONV_WIDTH = 3
NORM_EPS = 1e-6
LN_EPS = 1e-5
NEG_INF = -1e30

Q_W = N_HEADS * HEAD_DIM
KV_W = N_KV * HEAD_DIM
NSA_GATE_W = 3 * N_HEADS
IN_SIZES = (Q_W, KV_W, KV_W, KV_W, KV_W, KV_W, KV_W, NSA_GATE_W, SGU_WIDTH, SGU_WIDTH, D_MODEL, D_MODEL)

LANES = 128
SUBLANES = 8
QROWS = HEADS_PER_KV * Q_BLOCK
SEL_SHIFT = SEL_BLOCK.bit_length() - 1
BLOCKS_PER_Q = Q_BLOCK // SEL_BLOCK
SLC_BLOCKS = 8
SLC_CHUNK = SLC_BLOCKS * SEL_BLOCK
WIN_KEYS = WINDOW + Q_BLOCK
PROJ_TN = 512
N16_TILES = 4
FFN_TN = 512
VMEM_LIMIT = 56 * 1024 * 1024


def _cparams(sem):
    return pltpu.CompilerParams(dimension_semantics=sem, vmem_limit_bytes=VMEM_LIMIT)


def _dot(a, b):
    return jnp.dot(a, b, preferred_element_type=F32)


def _dot_nt(a, b):
    return lax.dot_general(a, b, (((1,), (1,)), ((), ())), preferred_element_type=F32)


def _rms(x, g):
    return x * lax.rsqrt(jnp.mean(x * x, -1, keepdims=True) + NORM_EPS) * g


N_TAIL_TILES = (SGU_WIDTH * 2 + D_MODEL * 2) // PROJ_TN
N_PROJ_TILES = N_TAIL_TILES + 1 + N16_TILES
KVC_TILE = N_TAIL_TILES


def _head_tile(j):
    k = j - N_TAIL_TILES
    return jnp.where(k <= 0, 2, jnp.where(k <= 2, k - 1, k))


def _in_proj_kernel(x_ref, g_ref, wh_ref, wt_ref, wg_ref, o16_ref, o32_ref, ong_ref, h_scr):
    j = pl.program_id(1)

    @pl.when(j == 0)
    def _():
        h = _rms(x_ref[...], g_ref[...]).astype(BF16)
        h_scr[...] = h
        ong_ref[...] = _dot(h, wg_ref[...])

    acc = _dot(h_scr[...], jnp.where(j < N_TAIL_TILES, wt_ref[...], wh_ref[...]))
    o32_ref[...] = acc
    is_q = (j == N_TAIL_TILES + 1) | (j == N_TAIL_TILES + 2)
    o16_ref[...] = (acc * jnp.where(is_q, HEAD_DIM ** -0.5, 1.0)).astype(BF16)


def _in_proj(x, g, w_head, w_tail, wg):
    t = x.shape[0]
    tm = min(1024, t)
    return pl.pallas_call(
        _in_proj_kernel,
        grid=(t // tm, N_PROJ_TILES),
        in_specs=[
            pl.BlockSpec((tm, D_MODEL), lambda i, j: (i, 0)),
            pl.BlockSpec((1, D_MODEL), lambda i, j: (0, 0)),
            pl.BlockSpec((D_MODEL, PROJ_TN), lambda i, j: (0, _head_tile(j))),
            pl.BlockSpec((D_MODEL, PROJ_TN), lambda i, j: (0, jnp.minimum(j, N_TAIL_TILES - 1))),
            pl.BlockSpec((D_MODEL, 2 * LANES), lambda i, j: (0, 0)),
        ],
        out_specs=[
            pl.BlockSpec((tm, PROJ_TN), lambda i, j: (i, jnp.maximum(j - (N_TAIL_TILES + 1), 0))),
            pl.BlockSpec((tm, PROJ_TN), lambda i, j: (i, jnp.minimum(j, N_TAIL_TILES + 1))),
            pl.BlockSpec((tm, 2 * LANES), lambda i, j: (i, 0)),
        ],
        out_shape=[
            jax.ShapeDtypeStruct((t, N16_TILES * PROJ_TN), BF16),
            jax.ShapeDtypeStruct((t, (N_TAIL_TILES + 2) * PROJ_TN), F32),
            jax.ShapeDtypeStruct((t, 2 * LANES), F32),
        ],
        scratch_shapes=[pltpu.VMEM((tm, D_MODEL), BF16)],
        compiler_params=_cparams(("arbitrary", "arbitrary")),
        name="in_proj",
    )(x, g, w_head, w_tail, wg)


def _compress_kernel(kv_ref, pe_ref, w1_ref, w2_ref, o_ref):
    nu = o_ref.shape[0]
    a = jnp.zeros((nu, HEAD_DIM), F32)
    b = jnp.zeros((nu, HEAD_DIM), F32)
    for r in range(CMP_STRIDE):
        x = kv_ref[pl.ds(r, nu, stride=CMP_STRIDE), :]
        a += _dot((x + pe_ref[pl.ds(r, 1), :]).astype(BF16), w1_ref[r])
        b += _dot((x + pe_ref[pl.ds(CMP_STRIDE + r, 1), :]).astype(BF16), w1_ref[CMP_STRIDE + r])
    hid = a + pltpu.roll(b, nu - 1, 0)
    o_ref[...] = _dot(jax.nn.gelu(hid).astype(BF16), w2_ref[...]).astype(BF16)


def _compress(o32, col0, pe, w1, w2, t):
    nu = t // CMP_STRIDE
    return pl.pallas_call(
        _compress_kernel,
        grid=(2, N_KV),
        in_specs=[
            pl.BlockSpec((t, HEAD_DIM), lambda a, g: (0, col0 + N_KV * a + g)),
            pl.BlockSpec((None, CMP_BLOCK, HEAD_DIM), lambda a, g: (a, 0, 0)),
            pl.BlockSpec((None, CMP_BLOCK, HEAD_DIM, HEAD_DIM), lambda a, g: (a, 0, 0, 0)),
            pl.BlockSpec((None, HEAD_DIM, HEAD_DIM), lambda a, g: (a, 0, 0)),
        ],
        out_specs=pl.BlockSpec((None, None, nu, HEAD_DIM), lambda a, g: (a, g, 0, 0)),
        out_shape=jax.ShapeDtypeStruct((2, N_KV, nu, HEAD_DIM), BF16),
        compiler_params=_cparams(("arbitrary", "arbitrary")),
        name="compress",
    )(o32, pe, w1, w2)


def _lane_tile(x, n):
    return jnp.concatenate([x] * n, axis=1)


def _row_tile(x, n):
    return jnp.concatenate([x] * n, axis=0)


def _softmax_numerators(s):
    return jnp.exp(s - jnp.max(s, axis=1, keepdims=True)).astype(BF16)


def _stack_heads(q, g):
    return jnp.concatenate([q[:, (g * HEADS_PER_KV + h) * HEAD_DIM:(g * HEADS_PER_KV + h + 1) * HEAD_DIM]
                            for h in range(HEADS_PER_KV)], axis=0)


def _nsa_select_kernel(q_ref, kvc_ref, bc_ref, sj_ref, slr_ref, ov_ref, lt_ref, ocmp_ref, aug_ref, lists_ref):
    i = pl.program_id(0)
    t0 = i * Q_BLOCK
    t0f = t0.astype(F32)
    q = q_ref[...]
    nc = kvc_ref.shape[2]
    row = lax.broadcasted_iota(jnp.int32, (QROWS, 1), 0) & (Q_BLOCK - 1)
    jr = lax.broadcasted_iota(jnp.int32, (LANES, Q_BLOCK), 0)
    kl = lax.broadcasted_iota(jnp.int32, (LANES, Q_BLOCK), 1)
    tq = t0 + kl
    cur = lax.shift_right_logical(tq, SEL_SHIFT)
    forced = (jr == 0) | (jr == cur) | (jr == cur - 1)
    visible = jr * SEL_BLOCK <= tq
    jf = jr.astype(F32)

    ocmp, imps = [], []
    for g in range(N_KV):
        bc = bc_ref[g]
        s = jnp.where(bc <= _lane_tile(slr_ref[g] * t0f, nc // LANES),
                      _dot_nt(_stack_heads(q, g), kvc_ref[0, g]) + bc, NEG_INF)
        r = _dot(_softmax_numerators(s), jnp.concatenate([kvc_ref[1, g], ov_ref[...]], axis=1))
        l = jnp.sum(r[:, HEAD_DIM:], axis=1, keepdims=True) * (1.0 / CMP_BLOCK)
        inv = jnp.where(t0 + row >= CMP_BLOCK - 1, 1.0 / l, 0.0)
        o = r[:, :HEAD_DIM] * inv
        ocmp += [o[h * Q_BLOCK:(h + 1) * Q_BLOCK] for h in range(HEADS_PER_KV)]
        imp = r[:, HEAD_DIM:] * inv
        imps.append(sum(imp[h * Q_BLOCK:(h + 1) * Q_BLOCK] for h in range(HEADS_PER_KV)).T)
    ocmp_ref[...] = jnp.concatenate(ocmp, axis=1)

    forced2, visible2, jf2 = (_lane_tile(a, N_KV) for a in (forced, visible, jf))
    score = jnp.where(forced2 | jnp.logical_not(visible2), NEG_INF, jnp.concatenate(imps, axis=1))
    sel_all = jnp.where(forced2, 1.0, 0.0)
    for _ in range(N_SELECT - N_FORCED):
        mx = jnp.max(score, axis=0, keepdims=True)
        idx = jnp.min(jnp.where(score == mx, jf2, float(LANES)), axis=0, keepdims=True)
        hit = jf2 == idx
        sel_all = jnp.where(hit, 1.0, sel_all)
        score = jnp.where(hit, -jnp.inf, score)
    sel_all = jnp.where(visible2, sel_all, 0.0)

    rows = []
    for g in range(N_KV):
        sel_t = sel_all[:, g * Q_BLOCK:(g + 1) * Q_BLOCK]
        aug_ref[g] = jnp.where(_row_tile(sel_t.T, HEADS_PER_KV) > 0.5, sj_ref[g], NEG_INF).astype(BF16)

        used = jnp.max(jnp.where(jr < BLOCKS_PER_Q * i, sel_t, 0.0), axis=1, keepdims=True)
        used = jnp.broadcast_to(used, (LANES, Q_BLOCK))
        slot = _dot(lt_ref[...], used.astype(BF16))
        lst = jnp.sum(jnp.where((slot == kl.astype(F32)) & (used > 0.5), jf, 0.0), axis=0, keepdims=True)
        cnt = jnp.sum(used, axis=0, keepdims=True)
        lst = jnp.where(kl[:1].astype(F32) < cnt, lst, (BLOCKS_PER_Q * i).astype(F32))
        rows += [lst, cnt]
    pad =[jnp.zeros((1, LANES), F32)] * (SUBLANES - len(rows))
    lists_ref[...] = jnp.concatenate(rows[0::2] + rows[1::2] + pad, axis=0).astype(jnp.int32)


def _nsa_attend_kernel(lists_ref, counts_ref, q_ref, ks_ref, vs_ref, kw_ref, vw_ref, aug_ref, ocmp_ref, ng_ref,
                       bw_ref, b1_ref, b1d_ref, et_ref, o_ref):
    i = pl.program_id(0)
    nqb = pl.num_programs(0)
    t0 = i * Q_BLOCK
    q = q_ref[...]
    q4 = [_stack_heads(q, g) for g in range(N_KV)]
    ones_col = jnp.ones((WIN_KEYS, LANES), BF16)
    gcols = lambda g: slice(g * HEAD_DIM, (g + 1) * HEAD_DIM)
    grows = lambda a, g: a[g * QROWS:(g + 1) * QROWS]
    both = lambda f: jnp.concatenate([f(g) for g in range(N_KV)], axis=0)

    s = both(lambda g: _dot_nt(q4[g], kw_ref[pl.ds(t0, WIN_KEYS), gcols(g)])) + bw_ref[...]
    s = jnp.concatenate(
        [s[:, b * Q_BLOCK:(b + 1) * Q_BLOCK] + jnp.where(b < WINDOW // Q_BLOCK - i, NEG_INF, 0.0)
         for b in range(WINDOW // Q_BLOCK)] + [s[:, WINDOW:]], axis=1)
    e = _softmax_numerators(s)
    r = both(lambda g: _dot(grows(e, g), jnp.concatenate([vw_ref[pl.ds(t0, WIN_KEYS), gcols(g)], ones_col], axis=1)))
    o_win = r[:, :HEAD_DIM] * (1.0 / r[:, HEAD_DIM:])

    jl = lax.broadcasted_iota(jnp.int32, (QROWS, LANES), 1)
    qx = [jnp.concatenate([q4[g], aug_ref[g]], axis=1) for g in range(N_KV)]
    qx_past = [jnp.concatenate([q4[g], jnp.where(jl < BLOCKS_PER_Q * i, aug_ref[g], NEG_INF).astype(BF16)], axis=1)
               for g in range(N_KV)]
    b1 = _lane_tile(b1_ref[...], SLC_CHUNK // LANES)

    def gathered(ref, g, first, cols):
        parts = []
        for k in range(SLC_BLOCKS):
            k0 = pl.multiple_of(lists_ref[g * nqb + i, first + k] * SEL_BLOCK, SEL_BLOCK)
            parts.append(ref[pl.ds(k0, SEL_BLOCK), cols])
        return jnp.concatenate(parts, axis=0)

    def slc_step(c, carry):
        m, acc = carry
        first = c * SLC_BLOCKS
        u = both(lambda g: _dot_nt(qx_past[g], jnp.concatenate(
            [gathered(ks_ref, g, first, gcols(g)), gathered(et_ref, g, first, slice(None))], axis=1))) + b1
        m_new = jnp.maximum(m, jnp.max(u, axis=1, keepdims=True))
        e = jnp.exp(u - m_new).astype(BF16)
        pv = both(lambda g: _dot(grows(e, g), jnp.concatenate(
            [gathered(vs_ref, g, first, gcols(g)), ones_col[:SLC_CHUNK]], axis=1)))
        return m_new, jnp.exp(m - m_new) * acc + pv

    n_blocks = jnp.maximum(counts_ref[i], counts_ref[nqb + i])
    n_trips = lax.shift_right_logical(n_blocks + (SLC_BLOCKS - 1), SLC_BLOCKS.bit_length() - 1)
    m, acc = lax.fori_loop(0, n_trips, slc_step,
                           (jnp.full((N_KV * QROWS, 1), NEG_INF, F32), jnp.zeros((N_KV * QROWS, 2 * HEAD_DIM), F32)))
    u = both(lambda g: _dot_nt(qx[g], jnp.concatenate(
        [ks_ref[pl.ds(t0, Q_BLOCK), gcols(g)], et_ref[pl.ds(t0, Q_BLOCK), :]], axis=1))) + b1d_ref[...]
    m_new = jnp.maximum(m, jnp.max(u, axis=1, keepdims=True))
    e = jnp.exp(u - m_new).astype(BF16)
    pv = both(lambda g: _dot(grows(e, g), jnp.concatenate(
        [vs_ref[pl.ds(t0, Q_BLOCK), gcols(g)], ones_col[:Q_BLOCK]], axis=1)))
    acc = jnp.exp(m - m_new) * acc + pv
    o_slc = acc[:, :HEAD_DIM] * (1.0 / acc[:, HEAD_DIM:])

    sg = jax.nn.sigmoid(ng_ref[...])
    gl = lax.broadcasted_iota(jnp.int32, (Q_BLOCK, N_KV * LANES), 1)
    o_cmp = ocmp_ref[...]

    def gate(col):
        return jnp.sum(jnp.where(gl == col, sg, 0.0), axis=1, keepdims=True)

    outs = []
    for hh in range(N_HEADS):
        g, h = divmod(hh, HEADS_PER_KV)
        rows = slice(hh * Q_BLOCK, (hh + 1) * Q_BLOCK)
        c0 = g * LANES + 3 * h
        outs.append(gate(c0) * o_cmp[:, hh * HEAD_DIM:(hh + 1) * HEAD_DIM]
                    + gate(c0 + 1) * o_slc[rows] + gate(c0 + 2) * o_win[rows])
    o_ref[...] = jnp.concatenate(outs, axis=1).astype(BF16)


def _nsa_tables(t):
    nc = t // CMP_STRIDE
    f32 = np.float32
    r = (np.arange(QROWS) % Q_BLOCK)[:, None].astype(f32)
    slopes = np.exp2(-8.0 * np.arange(1, N_HEADS + 1, dtype=f32) / N_HEADS).astype(f32)
    slr = np.repeat(slopes.reshape(N_KV, HEADS_PER_KV), Q_BLOCK, axis=1)[:, :, None]
    bc = slr * ((np.arange(nc) * CMP_STRIDE + CMP_BLOCK - 1)[None, None, :] - r[None])
    dw = r - np.arange(WIN_KEYS)[None, :] + WINDOW
    bw = np.where((dw >= 0) & (dw < WINDOW), -slr * dw[None], f32(NEG_INF))
    kk = np.arange(LANES)[None, :]
    b1 = slr * (kk % SEL_BLOCK)[None]
    b1d = np.where(kk <= r, b1, f32(NEG_INF))
    sj = slr * (SEL_BLOCK * kk)[None]
    et = (np.arange(t)[:, None] // SEL_BLOCK == kk)
    cmp_start = np.arange(nc) * CMP_STRIDE
    sel_start = np.arange(LANES) * SEL_BLOCK
    ov = np.clip(np.minimum(cmp_start[:, None] + CMP_BLOCK - 1, sel_start[None] + SEL_BLOCK - 1)
                 - np.maximum(cmp_start[:, None], sel_start[None]) + 1, 0, None)
    ov[nc - 1] = 0
    lt = np.tril(np.ones((LANES, LANES)), -1)
    slr = slr * np.ones((1, 1, LANES), f32)
    as_f32 = lambda a: jnp.asarray(a.astype(f32))
    as_bf16 = lambda a: jnp.asarray(a.astype(f32), BF16)
    return dict(bc=as_f32(bc), bw=as_f32(bw), b1=as_f32(b1), b1d=as_f32(b1d), sj=as_f32(sj), slr=as_f32(slr),
                et=as_bf16(et), ov=as_bf16(ov), lt=as_bf16(lt))


def _nsa(o16, kvw, kvc, ng, t):
    tb = _nsa_tables(t)
    nc = t // CMP_STRIDE
    nqb = t // Q_BLOCK
    whole = lambda a: pl.BlockSpec(a.shape, lambda i: (0,) * a.ndim)
    ocmp, aug, lists = pl.pallas_call(
        _nsa_select_kernel,
        grid=(nqb,),
        in_specs=[
            pl.BlockSpec((Q_BLOCK, Q_W), lambda i: (i, 0)),
            whole(kvc), whole(tb["bc"]), whole(tb["sj"]), whole(tb["slr"]), whole(tb["ov"]), whole(tb["lt"]),
        ],
        out_specs=[
            pl.BlockSpec((Q_BLOCK, Q_W), lambda i: (i, 0)),
            pl.BlockSpec((N_KV, QROWS, LANES), lambda i: (0, i, 0)),
            pl.BlockSpec((None, SUBLANES, LANES), lambda i: (i, 0, 0)),
        ],
        out_shape=[
            jax.ShapeDtypeStruct((t, Q_W), F32),
            jax.ShapeDtypeStruct((N_KV, nqb * QROWS, LANES), BF16),
            jax.ShapeDtypeStruct((nqb, SUBLANES, LANES), jnp.int32),
        ],
        compiler_params=_cparams(("arbitrary",)),
        name="nsa_select",
    )(o16, kvc, tb["bc"], tb["sj"], tb["slr"], tb["ov"], tb["lt"])
    block_lists = lists[:, :N_KV, :].transpose(1, 0, 2).reshape(N_KV * nqb, LANES)
    counts = lists[:, N_KV:2 * N_KV, 0].T.reshape(N_KV * nqb)

    stacked = lambda a: a.reshape((N_KV * QROWS,) + a.shape[2:])
    bw, b1, b1d = stacked(tb["bw"]), stacked(tb["b1"]), stacked(tb["b1d"])
    once = pl.Buffered(1)
    const = lambda a: pl.BlockSpec(a.shape, lambda i, *_: (0,) * a.ndim, pipeline_mode=once)
    grid_spec = pltpu.PrefetchScalarGridSpec(
        num_scalar_prefetch=2,
        grid=(nqb,),
        in_specs=[
            pl.BlockSpec((Q_BLOCK, Q_W), lambda i, *_: (i, 0)),
            pl.BlockSpec((t, KV_W), lambda i, *_: (0, Q_W // KV_W), pipeline_mode=once),
            pl.BlockSpec((t, KV_W), lambda i, *_: (0, Q_W // KV_W + 1), pipeline_mode=once),
            pl.BlockSpec((t + WINDOW, KV_W), lambda i, *_: (0, 0), pipeline_mode=once),
            pl.BlockSpec((t + WINDOW, KV_W), lambda i, *_: (0, 1), pipeline_mode=once),
            pl.BlockSpec((N_KV, QROWS, LANES), lambda i, *_: (0, i, 0)),
            pl.BlockSpec((Q_BLOCK, Q_W), lambda i, *_: (i, 0)),
            pl.BlockSpec((Q_BLOCK, N_KV * LANES), lambda i, *_: (i, 0)),
            const(bw), const(b1), const(b1d), const(tb["et"]),
        ],
        out_specs=pl.BlockSpec((Q_BLOCK, Q_W), lambda i, *_: (i, 0)),
    )
    return pl.pallas_call(
        _nsa_attend_kernel,
        grid_spec=grid_spec,
        out_shape=jax.ShapeDtypeStruct((t, Q_W), BF16),
        compiler_params=_cparams(("arbitrary",)),
        name="nsa_attend",
    )(block_lists, counts, o16, o16, o16, kvw, kvw, aug, ocmp, ng, bw, b1, b1d, tb["et"])


def _sgu_kernel(u_ref, v_ref, lng_ref, lnb_ref, ws_ref, bs_ref, o_ref):
    gu = jax.nn.gelu(u_ref[...])
    gv = jax.nn.gelu(v_ref[...])
    xc = gv - jnp.mean(gv, -1, keepdims=True)
    vn = (xc * lax.rsqrt(jnp.mean(xc * xc, -1, keepdims=True) + LN_EPS) * lng_ref[...] + lnb_ref[...]).astype(BF16)
    causal = (lax.broadcasted_iota(jnp.int32, (SGU_CHUNK, SGU_CHUNK), 0)
              >= lax.broadcasted_iota(jnp.int32, (SGU_CHUNK, SGU_CHUNK), 1))
    outs = []
    for g in range(SGU_GROUPS):
        w = jnp.where(causal, ws_ref[g], 0.0).astype(BF16)
        outs.append(_dot(w, vn[:, g * SGU_GROUP_DIM:(g + 1) * SGU_GROUP_DIM]) + bs_ref[g])
    o_ref[...] = (gu * jnp.concatenate(outs, axis=1)).astype(BF16)


def _sgu(o32, lng, lnb, ws, bs, t):
    return pl.pallas_call(
        _sgu_kernel,
        grid=(t // SGU_CHUNK,),
        in_specs=[
            pl.BlockSpec((SGU_CHUNK, SGU_WIDTH), lambda i: (i, 0)),
            pl.BlockSpec((SGU_CHUNK, SGU_WIDTH), lambda i: (i, 1)),
            pl.BlockSpec((1, SGU_WIDTH), lambda i: (0, 0)),
            pl.BlockSpec((1, SGU_WIDTH), lambda i: (0, 0)),
            pl.BlockSpec((SGU_GROUPS, SGU_CHUNK, SGU_CHUNK), lambda i: (0, 0, 0)),
            pl.BlockSpec((SGU_GROUPS, SGU_CHUNK, LANES), lambda i: (0, 0, 0)),
        ],
        out_specs=pl.BlockSpec((SGU_CHUNK, SGU_WIDTH), lambda i: (i, 0)),
        out_shape=jax.ShapeDtypeStruct((t, SGU_WIDTH), BF16),
        compiler_params=_cparams(("arbitrary",)),
        name="sgu",
    )(o32, o32, lng, lnb, ws, bs)


def _mix_kernel(ya_ref, yb_ref, ga_ref, gb_ref, x_ref, pa_ref, pb_ref, wo_ref, g2_ref, x1_ref, h2_ref):
    mixed = (jax.nn.sigmoid(ga_ref[...]) * _dot(ya_ref[...], pa_ref[...])
             + jax.nn.sigmoid(gb_ref[...]) * _dot(yb_ref[...], pb_ref[...]))
    x1 = x_ref[...] + _dot(mixed.astype(BF16), wo_ref[...])
    x1_ref[...] = x1
    h2_ref[...] = _rms(x1, g2_ref[...]).astype(BF16)


def _mix(ya, yb, o32, x, pa, pb, wo, g2, t):
    tm = min(256, t)
    once = pl.Buffered(1)
    return pl.pallas_call(
        _mix_kernel,
        grid=(t // tm,),
        in_specs=[
            pl.BlockSpec((tm, Q_W), lambda i: (i, 0)),
            pl.BlockSpec((tm, SGU_WIDTH), lambda i: (i, 0)),
            pl.BlockSpec((tm, D_MODEL), lambda i: (i, 1)),
            pl.BlockSpec((tm, D_MODEL), lambda i: (i, 2)),
            pl.BlockSpec((tm, D_MODEL), lambda i: (i, 0)),
            pl.BlockSpec((Q_W, D_MODEL), lambda i: (0, 0), pipeline_mode=once),
            pl.BlockSpec((SGU_WIDTH, D_MODEL), lambda i: (0, 0), pipeline_mode=once),
            pl.BlockSpec((D_MODEL, D_MODEL), lambda i: (0, 0), pipeline_mode=once),
            pl.BlockSpec((1, D_MODEL), lambda i: (0, 0)),
        ],
        out_specs=[
            pl.BlockSpec((tm, D_MODEL), lambda i: (i, 0)),
            pl.BlockSpec((tm, D_MODEL), lambda i: (i, 0)),
        ],
        out_shape=[jax.ShapeDtypeStruct((t, D_MODEL), F32), jax.ShapeDtypeStruct((t, D_MODEL), BF16)],
        compiler_params=_cparams(("arbitrary",)),
        name="mix_out",
    )(ya, yb, o32, o32, x, pa, pb, wo, g2)


FFN_TILES = pl.cdiv(D_FF, FFN_TN)
FFN_BACK = FFN_TILES * FFN_TN - D_FF
FFN_MAIN = (FFN_TILES - 1) * FFN_TN


def _ffn_col(j, base=0):
    return (base // LANES + j * (FFN_TN // LANES) - (j // (FFN_TILES - 1)) * (FFN_BACK // LANES)) * LANES


def _ffn_in_kernel(h_ref, wa_ref, wb_ref, cw_ref, cb_ref, o_ref, wa_scr, wb_scr, a_scr):
    tm = h_ref.shape[0]

    @pl.when(pl.program_id(1) == 0)
    def _():
        wa_scr[...] = wa_ref[...].astype(BF16)
        wb_scr[...] = wb_ref[...].astype(BF16)
        a_scr[0:SUBLANES, :] = jnp.zeros((SUBLANES, FFN_TN), F32)

    h = h_ref[...]
    a = _dot(h, wa_scr[...])
    b = _dot(h, wb_scr[...])
    a_scr[SUBLANES:SUBLANES + tm, :] = a
    cw = cw_ref[...]
    y = (cw[0:1] * a_scr[pl.ds(SUBLANES - 2, tm), :] + cw[1:2] * a_scr[pl.ds(SUBLANES - 1, tm), :]
         + cw[2:3] * a + cb_ref[...])
    o_ref[...] = (jax.nn.gelu(y) * b).astype(BF16)
    a_scr[0:SUBLANES, :] = a[tm - SUBLANES:tm, :]


def _ffn_in(h2, w, cw, cb, t):
    tm = min(1024, t)
    tile = lambda rows, base: pl.BlockSpec((pl.Element(rows), pl.Element(FFN_TN)),
                                           lambda j, i: (0, _ffn_col(j, base)))
    return pl.pallas_call(
        _ffn_in_kernel,
        grid=(FFN_TILES, t // tm),
        in_specs=[
            pl.BlockSpec((tm, D_MODEL), lambda j, i: (i, 0)),
            tile(D_MODEL, 0),
            tile(D_MODEL, D_FF),
            tile(SUBLANES, 0),
            tile(1, 0),
        ],
        out_specs=pl.BlockSpec((tm, FFN_TN), lambda j, i: (i, j)),
        out_shape=jax.ShapeDtypeStruct((t, FFN_TILES * FFN_TN), BF16),
        scratch_shapes=[pltpu.VMEM((D_MODEL, FFN_TN), BF16), pltpu.VMEM((D_MODEL, FFN_TN), BF16),
                        pltpu.VMEM((tm + SUBLANES, FFN_TN), F32)],
        compiler_params=_cparams(("arbitrary", "arbitrary")),
        name="ffn_in",
    )(h2, w, w, cw, cb)


def _ffn_out_kernel(a0_ref, a1_ref, w0_ref, w1_ref, x_ref, g_ref, o_ref, *, final_norm):
    x2 = x_ref[...] + _dot(a0_ref[...], w0_ref[...]) + _dot(a1_ref[...], w1_ref[...])
    o_ref[...] = _rms(x2, g_ref[...]) if final_norm else x2


def _ffn_out(act, wd, x1, g, t, final_norm):
    tm = min(256, t)
    once = pl.Buffered(1)
    rest = D_FF - FFN_MAIN
    return pl.pallas_call(
        functools.partial(_ffn_out_kernel, final_norm=final_norm),
        grid=(t // tm,),
        in_specs=[
            pl.BlockSpec((pl.Element(tm), pl.Element(FFN_MAIN)), lambda i: (i * tm, 0)),
            pl.BlockSpec((pl.Element(tm), pl.Element(rest)), lambda i: (i * tm, FFN_MAIN + FFN_BACK)),
            pl.BlockSpec((pl.Element(FFN_MAIN), pl.Element(D_MODEL)), lambda i: (0, 0), pipeline_mode=once),
            pl.BlockSpec((pl.Element(rest), pl.Element(D_MODEL)), lambda i: (FFN_MAIN, 0), pipeline_mode=once),
            pl.BlockSpec((tm, D_MODEL), lambda i: (i, 0)),
            pl.BlockSpec((1, D_MODEL), lambda i: (0, 0)),
        ],
        out_specs=pl.BlockSpec((tm, D_MODEL), lambda i: (i, 0)),
        out_shape=jax.ShapeDtypeStruct((t, D_MODEL), F32),
        compiler_params=_cparams(("arbitrary",)),
        name="ffn_out",
    )(act, act, wd, wd, x1, g)


def _layer(x, p, final_g, t):
    offs = np.cumsum((0,) + IN_SIZES)
    w_in = p["w_in"].astype(BF16)
    wg = w_in[:, offs[7]:offs[8]].reshape(D_MODEL, N_KV, HEADS_PER_KV * 3)
    wg = jnp.pad(wg, ((0, 0), (0, 0), (0, LANES - HEADS_PER_KV * 3))).reshape(D_MODEL, N_KV * LANES)
    o16, o32, ng = _in_proj(x, p["norm1_g"][None, :], w_in, w_in[:, offs[8]:], wg)

    pe = jnp.stack([p["cmp_k_pe"], p["cmp_v_pe"]])
    w1 = jnp.stack([p["cmp_k_w1"], p["cmp_v_w1"]]).reshape(2, CMP_BLOCK, HEAD_DIM, HEAD_DIM).astype(BF16)
    w2 = jnp.stack([p["cmp_k_w2"], p["cmp_v_w2"]]).astype(BF16)
    kvc = _compress(o32, KVC_TILE * PROJ_TN // HEAD_DIM, pe, w1, w2, t)

    kvw = jnp.pad(o16[:, Q_W + 2 * KV_W:], ((WINDOW, 0), (0, 0)))
    y_a = _nsa(o16, kvw, kvc, ng, t)
    bs = jnp.broadcast_to(p["sgu_b"][:, :, None], (SGU_GROUPS, SGU_CHUNK, LANES))
    y_b = _sgu(o32, p["sgu_ln_g"][None, :], p["sgu_ln_b"][None, :], p["sgu_w"], bs, t)
    x1, h2 = _mix(y_a, y_b, o32, x, p["w_branch_a"].astype(BF16), p["w_branch_b"].astype(BF16),
                  p["w_out"].astype(BF16), p["norm2_g"][None, :], t)

    cw = jnp.pad(p["ffn_conv_w"], ((0, SUBLANES - CONV_WIDTH), (0, 0)))
    act = _ffn_in(h2, p["ffn_w_in"], cw, p["ffn_conv_b"][None, :], t)
    wd = p["ffn_w_down"].astype(BF16)
    g = final_g[None, :] if final_g is not None else jnp.ones((1, D_MODEL), F32)
    return _ffn_out(act, wd, x1, g, t, final_g is not None)


def kernel(x, norm1_g, w_in, cmp_k_pe, cmp_k_w1, cmp_k_w2, cmp_v_pe, cmp_v_w1, cmp_v_w2, sgu_ln_g, sgu_ln_b, sgu_w, sgu_b, w_branch_a, w_branch_b, w_out, norm2_g, ffn_w_in, ffn_conv_w, ffn_conv_b, ffn_w_down, final_g):
    b, t, _ = x.shape
    assert b == 1 and t % 1024 == 0, "one sequence whose length is a multiple of 1024"
    params = dict(norm1_g=norm1_g, w_in=w_in, cmp_k_pe=cmp_k_pe, cmp_k_w1=cmp_k_w1, cmp_k_w2=cmp_k_w2,
                  cmp_v_pe=cmp_v_pe, cmp_v_w1=cmp_v_w1, cmp_v_w2=cmp_v_w2, sgu_ln_g=sgu_ln_g, sgu_ln_b=sgu_ln_b,
                  sgu_w=sgu_w, sgu_b=sgu_b, w_branch_a=w_branch_a, w_branch_b=w_branch_b, w_out=w_out,
                  norm2_g=norm2_g, ffn_w_in=ffn_w_in, ffn_conv_w=ffn_conv_w, ffn_conv_b=ffn_conv_b,
                  ffn_w_down=ffn_w_down)
    depth = norm1_g.shape[0]
    h = x[0]
    for l in range(depth):
        layer = {k: v[l] for k, v in params.items()}
        h = _layer(h, layer, final_g if l == depth - 1 else None, t)
    return h[None]
```

```python
import functools

import numpy as np
import jax
import jax.numpy as jnp
from jax import lax
from jax.experimental import pallas as pl
from jax.experimental.pallas import tpu as pltpu

F32 = jnp.float32
BF16 = jnp.bfloat16

D_MODEL = 2048
N_HEADS = 8
HEAD_DIM = 128
N_KV = 2
HEADS_PER_KV = N_HEADS // N_KV
CMP_BLOCK = 32
CMP_STRIDE = 16
SEL_BLOCK = 64
N_SELECT = 16
N_FORCED = 3
WINDOW = 512
Q_BLOCK = 128
SGU_WIDTH = 1024
SGU_GROUPS = 8
SGU_GROUP_DIM = SGU_WIDTH // SGU_GROUPS
SGU_CHUNK = 128
D_FF = 5504
CONV_WIDTH = 3
NORM_EPS = 1e-6
LN_EPS = 1e-5
NEG_INF = -1e30

Q_W = N_HEADS * HEAD_DIM
KV_W = N_KV * HEAD_DIM
NSA_GATE_W = 3 * N_HEADS
IN_SIZES = (Q_W, KV_W, KV_W, KV_W, KV_W, KV_W, KV_W, NSA_GATE_W, SGU_WIDTH, SGU_WIDTH, D_MODEL, D_MODEL)

LANES = 128
SUBLANES = 8
QROWS = HEADS_PER_KV * Q_BLOCK
SEL_SHIFT = SEL_BLOCK.bit_length() - 1
BLOCKS_PER_Q = Q_BLOCK // SEL_BLOCK
SLC_BLOCKS = 10
SLC_CHUNK = SLC_BLOCKS * SEL_BLOCK
WIN_KEYS = WINDOW + Q_BLOCK
PROJ_TN = 512
N16_TILES = 4
FFN_TN = 512
VMEM_LIMIT = 56 * 1024 * 1024


def _cparams(sem):
    return pltpu.CompilerParams(dimension_semantics=sem, vmem_limit_bytes=VMEM_LIMIT)


def _dot(a, b):
    return jnp.dot(a, b, preferred_element_type=F32)


def _dot_nt(a, b):
    return lax.dot_general(a, b, (((1,), (1,)), ((), ())), preferred_element_type=F32)


def _rms(x, g):
    return x * lax.rsqrt(jnp.mean(x * x, -1, keepdims=True) + NORM_EPS) * g


N_TAIL_TILES = (SGU_WIDTH * 2 + D_MODEL * 2) // PROJ_TN
N_PROJ_TILES = N_TAIL_TILES + N16_TILES + 1


def _head_tile(j):
    k = j - N_TAIL_TILES
    return jnp.where(k <= 1, jnp.maximum(k, 0), jnp.where(k <= 3, k + 1, 2))


def _in_proj_kernel(x_ref, g_ref, wh_ref, wt_ref, wg_ref, ohead_ref, otail_ref, okvc_ref, ong_ref, h_scr):
    j = pl.program_id(1)

    @pl.when(j == 0)
    def _():
        h = _rms(x_ref[...], g_ref[...]).astype(BF16)
        h_scr[...] = h
        ong_ref[...] = _dot(h, wg_ref[...])

    acc = _dot(h_scr[...], jnp.where(j < N_TAIL_TILES, wt_ref[...], wh_ref[...]))
    okvc_ref[...] = acc
    is_q = (j == N_TAIL_TILES) | (j == N_TAIL_TILES + 1)
    y = (acc * jnp.where(is_q, HEAD_DIM ** -0.5, 1.0)).astype(BF16)
    ohead_ref[...] = y
    otail_ref[...] = y


def _in_proj(x, g, w_head, w_tail, wg):
    t = x.shape[0]
    tm = min(1024, t)
    return pl.pallas_call(
        _in_proj_kernel,
        grid=(t // tm, N_PROJ_TILES),
        in_specs=[
            pl.BlockSpec((tm, D_MODEL), lambda i, j: (i, 0)),
            pl.BlockSpec((1, D_MODEL), lambda i, j: (0, 0)),
            pl.BlockSpec((D_MODEL, PROJ_TN), lambda i, j: (0, _head_tile(j))),
            pl.BlockSpec((D_MODEL, PROJ_TN), lambda i, j: (0, jnp.minimum(j, N_TAIL_TILES - 1))),
            pl.BlockSpec((D_MODEL, 2 * LANES), lambda i, j: (0, 0)),
        ],
        out_specs=[
            pl.BlockSpec((tm, PROJ_TN), lambda i, j: (i, jnp.clip(j - N_TAIL_TILES, 0, N16_TILES))),
            pl.BlockSpec((tm, PROJ_TN), lambda i, j: (i, jnp.minimum(j, N_TAIL_TILES))),
            pl.BlockSpec((tm, PROJ_TN), lambda i, j: (i, 0)),
            pl.BlockSpec((tm, 2 * LANES), lambda i, j: (i, 0)),
        ],
        out_shape=[
            jax.ShapeDtypeStruct((t, (N16_TILES + 1) * PROJ_TN), BF16),
            jax.ShapeDtypeStruct((t, (N_TAIL_TILES + 1) * PROJ_TN), BF16),
            jax.ShapeDtypeStruct((t, PROJ_TN), F32),
            jax.ShapeDtypeStruct((t, 2 * LANES), F32),
        ],
        scratch_shapes=[pltpu.VMEM((tm, D_MODEL), BF16)],
        compiler_params=_cparams(("arbitrary", "arbitrary")),
        name="in_proj",
    )(x, g, w_head, w_tail, wg)


def _compress_kernel(kv_ref, pe_ref, w1_ref, w2_ref, o_ref):
    nu = o_ref.shape[0]
    a = jnp.zeros((nu, HEAD_DIM), F32)
    b = jnp.zeros((nu, HEAD_DIM), F32)
    for r in range(CMP_STRIDE):
        x = kv_ref[pl.ds(r, nu, stride=CMP_STRIDE), :]
        a += _dot((x + pe_ref[pl.ds(r, 1), :]).astype(BF16), w1_ref[r])
        b += _dot((x + pe_ref[pl.ds(CMP_STRIDE + r, 1), :]).astype(BF16), w1_ref[CMP_STRIDE + r])
    hid = a + pltpu.roll(b, nu - 1, 0)
    o_ref[...] = _dot(jax.nn.gelu(hid).astype(BF16), w2_ref[...]).astype(BF16)


def _compress(o32, col0, pe, w1, w2, t):
    nu = t // CMP_STRIDE
    return pl.pallas_call(
        _compress_kernel,
        grid=(2, N_KV),
        in_specs=[
            pl.BlockSpec((t, HEAD_DIM), lambda a, g: (0, col0 + N_KV * a + g)),
            pl.BlockSpec((None, CMP_BLOCK, HEAD_DIM), lambda a, g: (a, 0, 0)),
            pl.BlockSpec((None, CMP_BLOCK, HEAD_DIM, HEAD_DIM), lambda a, g: (a, 0, 0, 0)),
            pl.BlockSpec((None, HEAD_DIM, HEAD_DIM), lambda a, g: (a, 0, 0)),
        ],
        out_specs=pl.BlockSpec((None, None, nu, HEAD_DIM), lambda a, g: (a, g, 0, 0)),
        out_shape=jax.ShapeDtypeStruct((2, N_KV, nu, HEAD_DIM), BF16),
        compiler_params=_cparams(("arbitrary", "arbitrary")),
        name="compress",
    )(o32, pe, w1, w2)


def _lane_tile(x, n):
    return jnp.concatenate([x] * n, axis=1)


def _row_tile(x, n):
    return jnp.concatenate([x] * n, axis=0)


def _softmax_numerators(s):
    return jnp.exp(s - jnp.max(s, axis=1, keepdims=True)).astype(BF16)


def _stack_heads(q, g):
    return jnp.concatenate([q[:, (g * HEADS_PER_KV + h) * HEAD_DIM:(g * HEADS_PER_KV + h + 1) * HEAD_DIM]
                            for h in range(HEADS_PER_KV)], axis=0)


def _nsa_select_kernel(q_ref, kvc_ref, bc_ref, sj_ref, slr_ref, ov_ref, lt_ref, ocmp_ref, aug_ref, lists_ref):
    i = pl.program_id(0)
    t0 = i * Q_BLOCK
    t0f = t0.astype(F32)
    q = q_ref[...]
    nc = kvc_ref.shape[2]
    row = lax.broadcasted_iota(jnp.int32, (QROWS, 1), 0) & (Q_BLOCK - 1)
    jr = lax.broadcasted_iota(jnp.int32, (LANES, Q_BLOCK), 0)
    kl = lax.broadcasted_iota(jnp.int32, (LANES, Q_BLOCK), 1)
    tq = t0 + kl
    cur = lax.shift_right_logical(tq, SEL_SHIFT)
    forced = (jr == 0) | (jr == cur) | (jr == cur - 1)
    visible = jr * SEL_BLOCK <= tq
    jf = jr.astype(F32)

    ocmp, imps = [], []
    for g in range(N_KV):
        bc = bc_ref[g]
        s = jnp.where(bc <= _lane_tile(slr_ref[g] * t0f, nc // LANES),
                      _dot_nt(_stack_heads(q, g), kvc_ref[0, g]) + bc, NEG_INF)
        r = _dot(_softmax_numerators(s), jnp.concatenate([kvc_ref[1, g], ov_ref[...]], axis=1))
        l = jnp.sum(r[:, HEAD_DIM:], axis=1, keepdims=True) * (1.0 / CMP_BLOCK)
        inv = jnp.where(t0 + row >= CMP_BLOCK - 1, 1.0 / l, 0.0)
        o = r[:, :HEAD_DIM] * inv
        ocmp += [o[h * Q_BLOCK:(h + 1) * Q_BLOCK] for h in range(HEADS_PER_KV)]
        imp = r[:, HEAD_DIM:] * inv
        imps.append(sum(imp[h * Q_BLOCK:(h + 1) * Q_BLOCK] for h in range(HEADS_PER_KV)).T)
    ocmp_ref[...] = jnp.concatenate(ocmp, axis=1)

    forced2, visible2, jf2 = (_lane_tile(a, N_KV) for a in (forced, visible, jf))
    score = jnp.where(forced2 | jnp.logical_not(visible2), NEG_INF, jnp.concatenate(imps, axis=1))
    sel_all = jnp.where(forced2, 1.0, 0.0)
    for _ in range(N_SELECT - N_FORCED):
        mx = jnp.max(score, axis=0, keepdims=True)
        idx = jnp.min(jnp.where(score == mx, jf2, float(LANES)), axis=0, keepdims=True)
        hit = jf2 == idx
        sel_all = jnp.where(hit, 1.0, sel_all)
        score = jnp.where(hit, -jnp.inf, score)
    sel_all = jnp.where(visible2, sel_all, 0.0)

    rows = []
    for g in range(N_KV):
        sel_t = sel_all[:, g * Q_BLOCK:(g + 1) * Q_BLOCK]
        aug_ref[g] = jnp.where(_row_tile(sel_t.T, HEADS_PER_KV) > 0.5, sj_ref[g], NEG_INF).astype(BF16)

        used = jnp.max(jnp.where(jr < BLOCKS_PER_Q * i, sel_t, 0.0), axis=1, keepdims=True)
        used = jnp.broadcast_to(used, (LANES, Q_BLOCK))
        slot = _dot(lt_ref[...], used.astype(BF16))
        lst = jnp.sum(jnp.where((slot == kl.astype(F32)) & (used > 0.5), jf, 0.0), axis=0, keepdims=True)
        cnt = jnp.sum(used, axis=0, keepdims=True)
        lst = jnp.where(kl[:1].astype(F32) < cnt, lst, (BLOCKS_PER_Q * i).astype(F32))
        rows += [lst, cnt]
    pad =[jnp.zeros((1, LANES), F32)] * (SUBLANES - len(rows))
    lists_ref[...] = jnp.concatenate(rows[0::2] + rows[1::2] + pad, axis=0).astype(jnp.int32)


def _nsa_attend_kernel(lists_ref, counts_ref, q_ref, ks_ref, vs_ref, kw_ref, vw_ref, aug_ref, ocmp_ref, ng_ref,
                       bw_ref, b1_ref, b1d_ref, et_ref, o_ref):
    i = pl.program_id(0)
    nqb = pl.num_programs(0)
    t0 = i * Q_BLOCK
    q = q_ref[...]
    q4 = [_stack_heads(q, g) for g in range(N_KV)]
    ones_col = jnp.ones((WIN_KEYS, LANES), BF16)
    gcols = lambda g: slice(g * HEAD_DIM, (g + 1) * HEAD_DIM)
    grows = lambda a, g: a[g * QROWS:(g + 1) * QROWS]
    both = lambda f: jnp.concatenate([f(g) for g in range(N_KV)], axis=0)

    s = both(lambda g: _dot_nt(q4[g], kw_ref[pl.ds(t0, WIN_KEYS), gcols(g)])) + bw_ref[...]
    s = jnp.concatenate(
        [s[:, b * Q_BLOCK:(b + 1) * Q_BLOCK] + jnp.where(b < WINDOW // Q_BLOCK - i, NEG_INF, 0.0)
         for b in range(WINDOW // Q_BLOCK)] + [s[:, WINDOW:]], axis=1)
    e = _softmax_numerators(s)
    r = both(lambda g: _dot(grows(e, g), jnp.concatenate([vw_ref[pl.ds(t0, WIN_KEYS), gcols(g)], ones_col], axis=1)))
    o_win = r[:, :HEAD_DIM] * (1.0 / r[:, HEAD_DIM:])

    jl = lax.broadcasted_iota(jnp.int32, (QROWS, LANES), 1)
    qx = [jnp.concatenate([q4[g], aug_ref[g]], axis=1) for g in range(N_KV)]
    qx_past = [jnp.concatenate([q4[g], jnp.where(jl < BLOCKS_PER_Q * i, aug_ref[g], NEG_INF).astype(BF16)], axis=1)
               for g in range(N_KV)]
    b1 = _lane_tile(b1_ref[...], SLC_CHUNK // LANES)

    def gathered(ref, g, first, cols):
        parts = []
        for k in range(SLC_BLOCKS):
            slot = jnp.minimum(first + k, LANES - 1)
            k0 = pl.multiple_of(lists_ref[g * nqb + i, slot] * SEL_BLOCK, SEL_BLOCK)
            parts.append(ref[pl.ds(k0, SEL_BLOCK), cols])
        return jnp.concatenate(parts, axis=0)

    def slc_step(c, carry):
        m, acc = carry
        first = c * SLC_BLOCKS
        u = both(lambda g: _dot_nt(qx_past[g], jnp.concatenate(
            [gathered(ks_ref, g, first, gcols(g)), gathered(et_ref, g, first, slice(None))], axis=1))) + b1
        m_new = jnp.maximum(m, jnp.max(u, axis=1, keepdims=True))
        e = jnp.exp(u - m_new).astype(BF16)
        pv = both(lambda g: _dot(grows(e, g), jnp.concatenate(
            [gathered(vs_ref, g, first, gcols(g)), ones_col[:SLC_CHUNK]], axis=1)))
        return m_new, jnp.exp(m - m_new) * acc + pv

    n_blocks = jnp.maximum(counts_ref[i], counts_ref[nqb + i])
    n_trips = (n_blocks + (SLC_BLOCKS - 1)) // SLC_BLOCKS
    m, acc = lax.fori_loop(0, n_trips, slc_step,
                           (jnp.full((N_KV * QROWS, 1), NEG_INF, F32), jnp.zeros((N_KV * QROWS, 2 * HEAD_DIM), F32)))
    u = both(lambda g: _dot_nt(qx[g], jnp.concatenate(
        [ks_ref[pl.ds(t0, Q_BLOCK), gcols(g)], et_ref[pl.ds(t0, Q_BLOCK), :]], axis=1))) + b1d_ref[...]
    m_new = jnp.maximum(m, jnp.max(u, axis=1, keepdims=True))
    e = jnp.exp(u - m_new).astype(BF16)
    pv = both(lambda g: _dot(grows(e, g), jnp.concatenate(
        [vs_ref[pl.ds(t0, Q_BLOCK), gcols(g)], ones_col[:Q_BLOCK]], axis=1)))
    acc = jnp.exp(m - m_new) * acc + pv
    o_slc = acc[:, :HEAD_DIM] * (1.0 / acc[:, HEAD_DIM:])

    sg = jax.nn.sigmoid(ng_ref[...])
    gl = lax.broadcasted_iota(jnp.int32, (Q_BLOCK, N_KV * LANES), 1)
    o_cmp = ocmp_ref[...]

    def gate(col):
        return jnp.sum(jnp.where(gl == col, sg, 0.0), axis=1, keepdims=True)

    outs = []
    for hh in range(N_HEADS):
        g, h = divmod(hh, HEADS_PER_KV)
        rows = slice(hh * Q_BLOCK, (hh + 1) * Q_BLOCK)
        c0 = g * LANES + 3 * h
        outs.append(gate(c0) * o_cmp[:, hh * HEAD_DIM:(hh + 1) * HEAD_DIM]
                    + gate(c0 + 1) * o_slc[rows] + gate(c0 + 2) * o_win[rows])
    o_ref[...] = jnp.concatenate(outs, axis=1).astype(BF16)


def _nsa_tables(t):
    nc = t // CMP_STRIDE
    f32 = np.float32
    r = (np.arange(QROWS) % Q_BLOCK)[:, None].astype(f32)
    slopes = np.exp2(-8.0 * np.arange(1, N_HEADS + 1, dtype=f32) / N_HEADS).astype(f32)
    slr = np.repeat(slopes.reshape(N_KV, HEADS_PER_KV), Q_BLOCK, axis=1)[:, :, None]
    bc = slr * ((np.arange(nc) * CMP_STRIDE + CMP_BLOCK - 1)[None, None, :] - r[None])
    dw = r - np.arange(WIN_KEYS)[None, :] + WINDOW
    bw = np.where((dw >= 0) & (dw < WINDOW), -slr * dw[None], f32(NEG_INF))
    kk = np.arange(LANES)[None, :]
    b1 = slr * (kk % SEL_BLOCK)[None]
    b1d = np.where(kk <= r, b1, f32(NEG_INF))
    sj = slr * (SEL_BLOCK * kk)[None]
    et = (np.arange(t)[:, None] // SEL_BLOCK == kk)
    cmp_start = np.arange(nc) * CMP_STRIDE
    sel_start = np.arange(LANES) * SEL_BLOCK
    ov = np.clip(np.minimum(cmp_start[:, None] + CMP_BLOCK - 1, sel_start[None] + SEL_BLOCK - 1)
                 - np.maximum(cmp_start[:, None], sel_start[None]) + 1, 0, None)
    ov[nc - 1] = 0
    lt = np.tril(np.ones((LANES, LANES)), -1)
    slr = slr * np.ones((1, 1, LANES), f32)
    as_f32 = lambda a: jnp.asarray(a.astype(f32))
    as_bf16 = lambda a: jnp.asarray(a.astype(f32), BF16)
    return dict(bc=as_f32(bc), bw=as_f32(bw), b1=as_f32(b1), b1d=as_f32(b1d), sj=as_f32(sj), slr=as_f32(slr),
                et=as_bf16(et), ov=as_bf16(ov), lt=as_bf16(lt))


def _nsa(o16, kvw, kvc, ng, t):
    tb = _nsa_tables(t)
    nc = t // CMP_STRIDE
    nqb = t // Q_BLOCK
    whole = lambda a: pl.BlockSpec(a.shape, lambda i: (0,) * a.ndim)
    ocmp, aug, lists = pl.pallas_call(
        _nsa_select_kernel,
        grid=(nqb,),
        in_specs=[
            pl.BlockSpec((Q_BLOCK, Q_W), lambda i: (i, 0)),
            whole(kvc), whole(tb["bc"]), whole(tb["sj"]), whole(tb["slr"]), whole(tb["ov"]), whole(tb["lt"]),
        ],
        out_specs=[
            pl.BlockSpec((Q_BLOCK, Q_W), lambda i: (i, 0)),
            pl.BlockSpec((N_KV, QROWS, LANES), lambda i: (0, i, 0)),
            pl.BlockSpec((None, SUBLANES, LANES), lambda i: (i, 0, 0)),
        ],
        out_shape=[
            jax.ShapeDtypeStruct((t, Q_W), F32),
            jax.ShapeDtypeStruct((N_KV, nqb * QROWS, LANES), BF16),
            jax.ShapeDtypeStruct((nqb, SUBLANES, LANES), jnp.int32),
        ],
        compiler_params=_cparams(("arbitrary",)),
        name="nsa_select",
    )(o16, kvc, tb["bc"], tb["sj"], tb["slr"], tb["ov"], tb["lt"])
    block_lists = lists[:, :N_KV, :].transpose(1, 0, 2).reshape(N_KV * nqb, LANES)
    counts = lists[:, N_KV:2 * N_KV, 0].T.reshape(N_KV * nqb)

    stacked = lambda a: a.reshape((N_KV * QROWS,) + a.shape[2:])
    bw, b1, b1d = stacked(tb["bw"]), stacked(tb["b1"]), stacked(tb["b1d"])
    once = pl.Buffered(1)
    const = lambda a: pl.BlockSpec(a.shape, lambda i, *_: (0,) * a.ndim, pipeline_mode=once)
    grid_spec = pltpu.PrefetchScalarGridSpec(
        num_scalar_prefetch=2,
        grid=(nqb,),
        in_specs=[
            pl.BlockSpec((Q_BLOCK, Q_W), lambda i, *_: (i, 0)),
            pl.BlockSpec((t, KV_W), lambda i, *_: (0, Q_W // KV_W), pipeline_mode=once),
            pl.BlockSpec((t, KV_W), lambda i, *_: (0, Q_W // KV_W + 1), pipeline_mode=once),
            pl.BlockSpec((t + WINDOW, KV_W), lambda i, *_: (0, 0), pipeline_mode=once),
            pl.BlockSpec((t + WINDOW, KV_W), lambda i, *_: (0, 1), pipeline_mode=once),
            pl.BlockSpec((N_KV, QROWS, LANES), lambda i, *_: (0, i, 0)),
            pl.BlockSpec((Q_BLOCK, Q_W), lambda i, *_: (i, 0)),
            pl.BlockSpec((Q_BLOCK, N_KV * LANES), lambda i, *_: (i, 0)),
            const(bw), const(b1), const(b1d), const(tb["et"]),
        ],
        out_specs=pl.BlockSpec((Q_BLOCK, Q_W), lambda i, *_: (i, 0)),
    )
    return pl.pallas_call(
        _nsa_attend_kernel,
        grid_spec=grid_spec,
        out_shape=jax.ShapeDtypeStruct((t, Q_W), BF16),
        compiler_params=_cparams(("arbitrary",)),
        name="nsa_attend",
    )(block_lists, counts, o16, o16, o16, kvw, kvw, aug, ocmp, ng, bw, b1, b1d, tb["et"])


SGU_CHUNKS_PER_STEP = 4


def _sgu_kernel(u_ref, v_ref, lng_ref, lnb_ref, ws_ref, bs_ref, o_ref):
    gu = jax.nn.gelu(u_ref[...].astype(F32))
    gv = jax.nn.gelu(v_ref[...].astype(F32))
    xc = gv - jnp.mean(gv, -1, keepdims=True)
    vn = (xc * lax.rsqrt(jnp.mean(xc * xc, -1, keepdims=True) + LN_EPS) * lng_ref[...] + lnb_ref[...]).astype(BF16)
    causal = (lax.broadcasted_iota(jnp.int32, (SGU_CHUNK, SGU_CHUNK), 0)
              >= lax.broadcasted_iota(jnp.int32, (SGU_CHUNK, SGU_CHUNK), 1))
    w = [jnp.where(causal, ws_ref[g], 0.0).astype(BF16) for g in range(SGU_GROUPS)]
    rows = []
    for c in range(SGU_CHUNKS_PER_STEP):
        vc = vn[c * SGU_CHUNK:(c + 1) * SGU_CHUNK]
        rows.append(jnp.concatenate(
            [_dot(w[g], vc[:, g * SGU_GROUP_DIM:(g + 1) * SGU_GROUP_DIM]) + bs_ref[g] for g in range(SGU_GROUPS)],
            axis=1))
    o_ref[...] = (gu * jnp.concatenate(rows, axis=0)).astype(BF16)


def _sgu(otail, lng, lnb, ws, bs, t):
    tm = SGU_CHUNKS_PER_STEP * SGU_CHUNK
    return pl.pallas_call(
        _sgu_kernel,
        grid=(t // tm,),
        in_specs=[
            pl.BlockSpec((tm, SGU_WIDTH), lambda i: (i, 0)),
            pl.BlockSpec((tm, SGU_WIDTH), lambda i: (i, 1)),
            pl.BlockSpec((1, SGU_WIDTH), lambda i: (0, 0)),
            pl.BlockSpec((1, SGU_WIDTH), lambda i: (0, 0)),
            pl.BlockSpec((SGU_GROUPS, SGU_CHUNK, SGU_CHUNK), lambda i: (0, 0, 0)),
            pl.BlockSpec((SGU_GROUPS, SGU_CHUNK, LANES), lambda i: (0, 0, 0)),
        ],
        out_specs=pl.BlockSpec((tm, SGU_WIDTH), lambda i: (i, 0)),
        out_shape=jax.ShapeDtypeStruct((t, SGU_WIDTH), BF16),
        compiler_params=_cparams(("arbitrary",)),
        name="sgu",
    )(otail, otail, lng, lnb, ws, bs)


def _mix_kernel(ya_ref, yb_ref, ga_ref, gb_ref, x_ref, pa_ref, pb_ref, wo_ref, g2_ref, x1_ref, h2_ref):
    mixed = (jax.nn.sigmoid(ga_ref[...].astype(F32)) * _dot(ya_ref[...], pa_ref[...])
             + jax.nn.sigmoid(gb_ref[...].astype(F32)) * _dot(yb_ref[...], pb_ref[...]))
    x1 = x_ref[...] + _dot(mixed.astype(BF16), wo_ref[...])
    x1_ref[...] = x1
    h2_ref[...] = _rms(x1, g2_ref[...]).astype(BF16)


def _mix(ya, yb, o32, x, pa, pb, wo, g2, t):
    tm = min(256, t)
    once = pl.Buffered(1)
    return pl.pallas_call(
        _mix_kernel,
        grid=(t // tm,),
        in_specs=[
            pl.BlockSpec((tm, Q_W), lambda i: (i, 0)),
            pl.BlockSpec((tm, SGU_WIDTH), lambda i: (i, 0)),
            pl.BlockSpec((tm, D_MODEL), lambda i: (i, 1)),
            pl.BlockSpec((tm, D_MODEL), lambda i: (i, 2)),
            pl.BlockSpec((tm, D_MODEL), lambda i: (i, 0)),
            pl.BlockSpec((Q_W, D_MODEL), lambda i: (0, 0), pipeline_mode=once),
            pl.BlockSpec((SGU_WIDTH, D_MODEL), lambda i: (0, 0), pipeline_mode=once),
            pl.BlockSpec((D_MODEL, D_MODEL), lambda i: (0, 0), pipeline_mode=once),
            pl.BlockSpec((1, D_MODEL), lambda i: (0, 0)),
        ],
        out_specs=[
            pl.BlockSpec((tm, D_MODEL), lambda i: (i, 0)),
            pl.BlockSpec((tm, D_MODEL), lambda i: (i, 0)),
        ],
        out_shape=[jax.ShapeDtypeStruct((t, D_MODEL), F32), jax.ShapeDtypeStruct((t, D_MODEL), BF16)],
        compiler_params=_cparams(("arbitrary",)),
        name="mix_out",
    )(ya, yb, o32, o32, x, pa, pb, wo, g2)


FFN_TILES = pl.cdiv(D_FF, FFN_TN)
FFN_BACK = FFN_TILES * FFN_TN - D_FF
FFN_MAIN = (FFN_TILES - 1) * FFN_TN


def _ffn_col(j, base=0):
    return (base // LANES + j * (FFN_TN // LANES) - (j // (FFN_TILES - 1)) * (FFN_BACK // LANES)) * LANES


def _ffn_in_kernel(h_ref, wa_ref, wb_ref, cw_ref, cb_ref, o_ref, wa_scr, wb_scr, a_scr):
    tm = h_ref.shape[0]

    @pl.when(pl.program_id(1) == 0)
    def _():
        wa_scr[...] = wa_ref[...].astype(BF16)
        wb_scr[...] = wb_ref[...].astype(BF16)
        a_scr[0:SUBLANES, :] = jnp.zeros((SUBLANES, FFN_TN), F32)

    h = h_ref[...]
    a = _dot(h, wa_scr[...])
    b = _dot(h, wb_scr[...])
    a_scr[SUBLANES:SUBLANES + tm, :] = a
    cw = cw_ref[...]
    y = (cw[0:1] * a_scr[pl.ds(SUBLANES - 2, tm), :] + cw[1:2] * a_scr[pl.ds(SUBLANES - 1, tm), :]
         + cw[2:3] * a + cb_ref[...])
    o_ref[...] = (jax.nn.gelu(y) * b).astype(BF16)
    a_scr[0:SUBLANES, :] = a[tm - SUBLANES:tm, :]


def _ffn_in(h2, w, cw, cb, t):
    tm = min(1024, t)
    tile = lambda rows, base: pl.BlockSpec((pl.Element(rows), pl.Element(FFN_TN)),
                                           lambda j, i: (0, _ffn_col(j, base)))
    return pl.pallas_call(
        _ffn_in_kernel,
        grid=(FFN_TILES, t // tm),
        in_specs=[
            pl.BlockSpec((tm, D_MODEL), lambda j, i: (i, 0)),
            tile(D_MODEL, 0),
            tile(D_MODEL, D_FF),
            tile(SUBLANES, 0),
            tile(1, 0),
        ],
        out_specs=pl.BlockSpec((tm, FFN_TN), lambda j, i: (i, j)),
        out_shape=jax.ShapeDtypeStruct((t, FFN_TILES * FFN_TN), BF16),
        scratch_shapes=[pltpu.VMEM((D_MODEL, FFN_TN), BF16), pltpu.VMEM((D_MODEL, FFN_TN), BF16),
                        pltpu.VMEM((tm + SUBLANES, FFN_TN), F32)],
        compiler_params=_cparams(("arbitrary", "arbitrary")),
        name="ffn_in",
    )(h2, w, w, cw, cb)


def _ffn_out_kernel(a0_ref, a1_ref, w0_ref, w1_ref, x_ref, g_ref, o_ref, *, final_norm):
    x2 = x_ref[...] + _dot(a0_ref[...], w0_ref[...]) + _dot(a1_ref[...], w1_ref[...])
    o_ref[...] = _rms(x2, g_ref[...]) if final_norm else x2


def _ffn_out(act, wd, x1, g, t, final_norm):
    tm = min(256, t)
    once = pl.Buffered(1)
    rest = D_FF - FFN_MAIN
    return pl.pallas_call(
        functools.partial(_ffn_out_kernel, final_norm=final_norm),
        grid=(t // tm,),
        in_specs=[
            pl.BlockSpec((pl.Element(tm), pl.Element(FFN_MAIN)), lambda i: (i * tm, 0)),
            pl.BlockSpec((pl.Element(tm), pl.Element(rest)), lambda i: (i * tm, FFN_MAIN + FFN_BACK)),
            pl.BlockSpec((pl.Element(FFN_MAIN), pl.Element(D_MODEL)), lambda i: (0, 0), pipeline_mode=once),
            pl.BlockSpec((pl.Element(rest), pl.Element(D_MODEL)), lambda i: (FFN_MAIN, 0), pipeline_mode=once),
            pl.BlockSpec((tm, D_MODEL), lambda i: (i, 0)),
            pl.BlockSpec((1, D_MODEL), lambda i: (0, 0)),
        ],
        out_specs=pl.BlockSpec((tm, D_MODEL), lambda i: (i, 0)),
        out_shape=jax.ShapeDtypeStruct((t, D_MODEL), F32),
        compiler_params=_cparams(("arbitrary",)),
        name="ffn_out",
    )(act, act, wd, wd, x1, g)


def _layer(x, p, final_g, t):
    offs = np.cumsum((0,) + IN_SIZES)
    w_in = p["w_in"].astype(BF16)
    wg = w_in[:, offs[7]:offs[8]].reshape(D_MODEL, N_KV, HEADS_PER_KV * 3)
    wg = jnp.pad(wg, ((0, 0), (0, 0), (0, LANES - HEADS_PER_KV * 3))).reshape(D_MODEL, N_KV * LANES)
    o16, otail, okvc, ng = _in_proj(x, p["norm1_g"][None, :], w_in, w_in[:, offs[8]:], wg)

    pe = jnp.stack([p["cmp_k_pe"], p["cmp_v_pe"]])
    w1 = jnp.stack([p["cmp_k_w1"], p["cmp_v_w1"]]).reshape(2, CMP_BLOCK, HEAD_DIM, HEAD_DIM).astype(BF16)
    w2 = jnp.stack([p["cmp_k_w2"], p["cmp_v_w2"]]).astype(BF16)
    kvc = _compress(okvc, 0, pe, w1, w2, t)

    kvw = jnp.pad(o16[:, Q_W + 2 * KV_W:Q_W + 4 * KV_W], ((WINDOW, 0), (0, 0)))
    y_a = _nsa(o16, kvw, kvc, ng, t)
    bs = jnp.broadcast_to(p["sgu_b"][:, :, None], (SGU_GROUPS, SGU_CHUNK, LANES))
    y_b = _sgu(otail, p["sgu_ln_g"][None, :], p["sgu_ln_b"][None, :], p["sgu_w"], bs, t)
    x1, h2 = _mix(y_a, y_b, otail, x, p["w_branch_a"].astype(BF16), p["w_branch_b"].astype(BF16),
                  p["w_out"].astype(BF16), p["norm2_g"][None, :], t)

    cw = jnp.pad(p["ffn_conv_w"], ((0, SUBLANES - CONV_WIDTH), (0, 0)))
    act = _ffn_in(h2, p["ffn_w_in"], cw, p["ffn_conv_b"][None, :], t)
    wd = p["ffn_w_down"].astype(BF16)
    g = final_g[None, :] if final_g is not None else jnp.ones((1, D_MODEL), F32)
    return _ffn_out(act, wd, x1, g, t, final_g is not None)


def kernel(x, norm1_g, w_in, cmp_k_pe, cmp_k_w1, cmp_k_w2, cmp_v_pe, cmp_v_w1, cmp_v_w2, sgu_ln_g, sgu_ln_b, sgu_w, sgu_b, w_branch_a, w_branch_b, w_out, norm2_g, ffn_w_in, ffn_conv_w, ffn_conv_b, ffn_w_down, final_g):
    b, t, _ = x.shape
    assert b == 1 and t % 1024 == 0, "one sequence whose length is a multiple of 1024"
    params = dict(norm1_g=norm1_g, w_in=w_in, cmp_k_pe=cmp_k_pe, cmp_k_w1=cmp_k_w1, cmp_k_w2=cmp_k_w2,
                  cmp_v_pe=cmp_v_pe, cmp_v_w1=cmp_v_w1, cmp_v_w2=cmp_v_w2, sgu_ln_g=sgu_ln_g, sgu_ln_b=sgu_ln_b,
                  sgu_w=sgu_w, sgu_b=sgu_b, w_branch_a=w_branch_a, w_branch_b=w_branch_b, w_out=w_out,
                  norm2_g=norm2_g, ffn_w_in=ffn_w_in, ffn_conv_w=ffn_conv_w, ffn_conv_b=ffn_conv_b,
                  ffn_w_down=ffn_w_down)
    depth = norm1_g.shape[0]
    h = x[0]
    for l in range(depth):
        layer = {k: v[l] for k, v in params.items()}
        h = _layer(h, layer, final_g if l == depth - 1 else None, t)
    return h[None]
```

```python
import functools

import numpy as np
import jax
import jax.numpy as jnp
from jax import lax
from jax.experimental import pallas as pl
from jax.experimental.pallas import tpu as pltpu

F32 = jnp.float32
BF16 = jnp.bfloat16

D_MODEL = 2048
N_HEADS = 8
HEAD_DIM = 128
N_KV = 2
HEADS_PER_KV = N_HEADS // N_KV
CMP_BLOCK = 32
CMP_STRIDE = 16
SEL_BLOCK = 64
N_SELECT = 16
N_FORCED = 3
WINDOW = 512
Q_BLOCK = 128
SGU_WIDTH = 1024
SGU_GROUPS = 8
SGU_GROUP_DIM = SGU_WIDTH // SGU_GROUPS
SGU_CHUNK = 128
D_FF = 5504
CONV_WIDTH = 3
NORM_EPS = 1e-6
LN_EPS = 1e-5
NEG_INF = -1e30

Q_W = N_HEADS * HEAD_DIM
KV_W = N_KV * HEAD_DIM
NSA_GATE_W = 3 * N_HEADS
IN_SIZES = (Q_W, KV_W, KV_W, KV_W, KV_W, KV_W, KV_W, NSA_GATE_W, SGU_WIDTH, SGU_WIDTH, D_MODEL, D_MODEL)

LANES = 128
SUBLANES = 8
QROWS = HEADS_PER_KV * Q_BLOCK
SEL_SHIFT = SEL_BLOCK.bit_length() - 1
BLOCKS_PER_Q = Q_BLOCK // SEL_BLOCK
SLC_BLOCKS = 10
SLC_CHUNK = SLC_BLOCKS * SEL_BLOCK
WIN_KEYS = WINDOW + Q_BLOCK
PROJ_TN = 512
N16_TILES = 4
FFN_TN = 512
VMEM_LIMIT = 56 * 1024 * 1024


def _cparams(sem):
    return pltpu.CompilerParams(dimension_semantics=sem, vmem_limit_bytes=VMEM_LIMIT)


def _dot(a, b):
    return jnp.dot(a, b, preferred_element_type=F32)


def _dot_nt(a, b):
    return lax.dot_general(a, b, (((1,), (1,)), ((), ())), preferred_element_type=F32)


def _rms(x, g):
    return x * lax.rsqrt(jnp.mean(x * x, -1, keepdims=True) + NORM_EPS) * g


N_TAIL_TILES = (SGU_WIDTH * 2 + D_MODEL * 2) // PROJ_TN
N_PROJ_TILES = N_TAIL_TILES + N16_TILES + 1


def _head_tile(j):
    k = j - N_TAIL_TILES
    return jnp.where(k <= 1, jnp.maximum(k, 0), jnp.where(k <= 3, k + 1, 2))


def _in_proj_kernel(x_ref, g_ref, wh_ref, wt_ref, wg_ref, ohead_ref, otail_ref, okvc_ref, ong_ref, h_scr):
    j = pl.program_id(1)

    @pl.when(j == 0)
    def _():
        h = _rms(x_ref[...], g_ref[...]).astype(BF16)
        h_scr[...] = h
        ong_ref[...] = _dot(h, wg_ref[...])

    acc = _dot(h_scr[...], jnp.where(j < N_TAIL_TILES, wt_ref[...], wh_ref[...]))
    okvc_ref[...] = acc
    is_q = (j == N_TAIL_TILES) | (j == N_TAIL_TILES + 1)
    y = (acc * jnp.where(is_q, HEAD_DIM ** -0.5, 1.0)).astype(BF16)
    ohead_ref[...] = y
    otail_ref[...] = y


def _in_proj(x, g, w_head, w_tail, wg):
    t = x.shape[0]
    tm = min(1024, t)
    return pl.pallas_call(
        _in_proj_kernel,
        grid=(t // tm, N_PROJ_TILES),
        in_specs=[
            pl.BlockSpec((tm, D_MODEL), lambda i, j: (i, 0)),
            pl.BlockSpec((1, D_MODEL), lambda i, j: (0, 0)),
            pl.BlockSpec((D_MODEL, PROJ_TN), lambda i, j: (0, _head_tile(j))),
            pl.BlockSpec((D_MODEL, PROJ_TN), lambda i, j: (0, jnp.minimum(j, N_TAIL_TILES - 1))),
            pl.BlockSpec((D_MODEL, 2 * LANES), lambda i, j: (0, 0)),
        ],
        out_specs=[
            pl.BlockSpec((tm, PROJ_TN), lambda i, j: (i, jnp.clip(j - N_TAIL_TILES, 0, N16_TILES))),
            pl.BlockSpec((tm, PROJ_TN), lambda i, j: (i, jnp.minimum(j, N_TAIL_TILES))),
            pl.BlockSpec((tm, PROJ_TN), lambda i, j: (i, 0)),
            pl.BlockSpec((tm, 2 * LANES), lambda i, j: (i, 0)),
        ],
        out_shape=[
            jax.ShapeDtypeStruct((t, (N16_TILES + 1) * PROJ_TN), BF16),
            jax.ShapeDtypeStruct((t, (N_TAIL_TILES + 1) * PROJ_TN), BF16),
            jax.ShapeDtypeStruct((t, PROJ_TN), F32),
            jax.ShapeDtypeStruct((t, 2 * LANES), F32),
        ],
        scratch_shapes=[pltpu.VMEM((tm, D_MODEL), BF16)],
        compiler_params=_cparams(("arbitrary", "arbitrary")),
        name="in_proj",
    )(x, g, w_head, w_tail, wg)


def _compress_kernel(kv_ref, pe_ref, w1_ref, w2_ref, o_ref):
    nu = o_ref.shape[0]
    a = jnp.zeros((nu, HEAD_DIM), F32)
    b = jnp.zeros((nu, HEAD_DIM), F32)
    for r in range(CMP_STRIDE):
        x = kv_ref[pl.ds(r, nu, stride=CMP_STRIDE), :]
        a += _dot((x + pe_ref[pl.ds(r, 1), :]).astype(BF16), w1_ref[r])
        b += _dot((x + pe_ref[pl.ds(CMP_STRIDE + r, 1), :]).astype(BF16), w1_ref[CMP_STRIDE + r])
    hid = a + pltpu.roll(b, nu - 1, 0)
    o_ref[...] = _dot(jax.nn.gelu(hid).astype(BF16), w2_ref[...]).astype(BF16)


def _compress(o32, col0, pe, w1, w2, t):
    nu = t // CMP_STRIDE
    return pl.pallas_call(
        _compress_kernel,
        grid=(2, N_KV),
        in_specs=[
            pl.BlockSpec((t, HEAD_DIM), lambda a, g: (0, col0 + N_KV * a + g)),
            pl.BlockSpec((None, CMP_BLOCK, HEAD_DIM), lambda a, g: (a, 0, 0)),
            pl.BlockSpec((None, CMP_BLOCK, HEAD_DIM, HEAD_DIM), lambda a, g: (a, 0, 0, 0)),
            pl.BlockSpec((None, HEAD_DIM, HEAD_DIM), lambda a, g: (a, 0, 0)),
        ],
        out_specs=pl.BlockSpec((None, None, nu, HEAD_DIM), lambda a, g: (a, g, 0, 0)),
        out_shape=jax.ShapeDtypeStruct((2, N_KV, nu, HEAD_DIM), BF16),
        compiler_params=_cparams(("arbitrary", "arbitrary")),
        name="compress",
    )(o32, pe, w1, w2)


def _lane_tile(x, n):
    return jnp.concatenate([x] * n, axis=1)


def _row_tile(x, n):
    return jnp.concatenate([x] * n, axis=0)


def _softmax_numerators(s):
    return jnp.exp(s - jnp.max(s, axis=1, keepdims=True)).astype(BF16)


def _stack_heads(q, g):
    return jnp.concatenate([q[:, (g * HEADS_PER_KV + h) * HEAD_DIM:(g * HEADS_PER_KV + h + 1) * HEAD_DIM]
                            for h in range(HEADS_PER_KV)], axis=0)


def _nsa_select_kernel(q_ref, kvc_ref, bc_ref, sj_ref, slr_ref, ov_ref, lt_ref, ocmp_ref, aug_ref, lists_ref):
    i = pl.program_id(0)
    t0 = i * Q_BLOCK
    t0f = t0.astype(F32)
    q = q_ref[...]
    nc = kvc_ref.shape[2]
    row = lax.broadcasted_iota(jnp.int32, (QROWS, 1), 0) & (Q_BLOCK - 1)
    jr = lax.broadcasted_iota(jnp.int32, (LANES, Q_BLOCK), 0)
    kl = lax.broadcasted_iota(jnp.int32, (LANES, Q_BLOCK), 1)
    tq = t0 + kl
    cur = lax.shift_right_logical(tq, SEL_SHIFT)
    forced = (jr == 0) | (jr == cur) | (jr == cur - 1)
    visible = jr * SEL_BLOCK <= tq
    jf = jr.astype(F32)

    ocmp, imps = [], []
    for g in range(N_KV):
        bc = bc_ref[g]
        s = jnp.where(bc <= _lane_tile(slr_ref[g] * t0f, nc // LANES),
                      _dot_nt(_stack_heads(q, g), kvc_ref[0, g]) + bc, NEG_INF)
        r = _dot(_softmax_numerators(s), jnp.concatenate([kvc_ref[1, g], ov_ref[...]], axis=1))
        l = jnp.sum(r[:, HEAD_DIM:], axis=1, keepdims=True) * (1.0 / CMP_BLOCK)
        inv = jnp.where(t0 + row >= CMP_BLOCK - 1, 1.0 / l, 0.0)
        o = r[:, :HEAD_DIM] * inv
        ocmp += [o[h * Q_BLOCK:(h + 1) * Q_BLOCK] for h in range(HEADS_PER_KV)]
        imp = r[:, HEAD_DIM:] * inv
        imps.append(sum(imp[h * Q_BLOCK:(h + 1) * Q_BLOCK] for h in range(HEADS_PER_KV)).T)
    ocmp_ref[...] = jnp.concatenate(ocmp, axis=1)

    forced2, visible2, jf2 = (_lane_tile(a, N_KV) for a in (forced, visible, jf))
    score = jnp.where(forced2 | jnp.logical_not(visible2), NEG_INF, jnp.concatenate(imps, axis=1))
    for _ in range(N_SELECT - N_FORCED):
        mx = jnp.max(score, axis=0, keepdims=True)
        idx = jnp.min(jnp.where(score == mx, jf2, float(LANES)), axis=0, keepdims=True)
        score = jnp.where(jf2 == idx, -jnp.inf, score)
    sel_all = jnp.where(visible2 & (forced2 | (score == -jnp.inf)), 1.0, 0.0)

    rows = []
    for g in range(N_KV):
        sel_t = sel_all[:, g * Q_BLOCK:(g + 1) * Q_BLOCK]
        aug_ref[g] = jnp.where(_row_tile(sel_t.T, HEADS_PER_KV) > 0.5, sj_ref[g], NEG_INF).astype(BF16)

        used = jnp.max(jnp.where(jr < BLOCKS_PER_Q * i, sel_t, 0.0), axis=1, keepdims=True)
        used = jnp.broadcast_to(used, (LANES, Q_BLOCK))
        slot = _dot(lt_ref[...], used.astype(BF16))
        lst = jnp.sum(jnp.where((slot == kl.astype(F32)) & (used > 0.5), jf, 0.0), axis=0, keepdims=True)
        cnt = jnp.sum(used, axis=0, keepdims=True)
        lst = jnp.where(kl[:1].astype(F32) < cnt, lst, (BLOCKS_PER_Q * i).astype(F32))
        rows += [lst, cnt]
    pad =[jnp.zeros((1, LANES), F32)] * (SUBLANES - len(rows))
    lists_ref[...] = jnp.concatenate(rows[0::2] + rows[1::2] + pad, axis=0).astype(jnp.int32)


def _nsa_attend_kernel(lists_ref, counts_ref, q_ref, ks_ref, vs_ref, kw_ref, vw_ref, aug_ref, ocmp_ref, ng_ref,
                       bw_ref, b1_ref, b1d_ref, et_ref, o_ref):
    i = pl.program_id(0)
    nqb = pl.num_programs(0)
    t0 = i * Q_BLOCK
    q = q_ref[...]
    q4 = [_stack_heads(q, g) for g in range(N_KV)]
    ones_col = jnp.ones((WIN_KEYS, LANES), BF16)
    gcols = lambda g: slice(g * HEAD_DIM, (g + 1) * HEAD_DIM)
    grows = lambda a, g: a[g * QROWS:(g + 1) * QROWS]
    both = lambda f: jnp.concatenate([f(g) for g in range(N_KV)], axis=0)

    s = both(lambda g: _dot_nt(q4[g], kw_ref[pl.ds(t0, WIN_KEYS), gcols(g)])) + bw_ref[...]
    s = jnp.concatenate(
        [s[:, b * Q_BLOCK:(b + 1) * Q_BLOCK] + jnp.where(b < WINDOW // Q_BLOCK - i, NEG_INF, 0.0)
         for b in range(WINDOW // Q_BLOCK)] + [s[:, WINDOW:]], axis=1)
    e = _softmax_numerators(s)
    r = both(lambda g: _dot(grows(e, g), jnp.concatenate([vw_ref[pl.ds(t0, WIN_KEYS), gcols(g)], ones_col], axis=1)))
    o_win = r[:, :HEAD_DIM] * (1.0 / r[:, HEAD_DIM:])

    gl = lax.broadcasted_iota(jnp.int32, (Q_BLOCK, LANES), 1)
    o_cmp = ocmp_ref[...]
    slc_gate, partial = [], []
    for g in range(N_KV):
        sg = jax.nn.sigmoid(ng_ref[:, gcols(g)])
        gate = lambda col: jnp.sum(jnp.where(gl == col, sg, 0.0), axis=1, keepdims=True)
        for h in range(HEADS_PER_KV):
            hh = g * HEADS_PER_KV + h
            slc_gate.append(gate(3 * h + 1))
            partial.append(gate(3 * h) * o_cmp[:, hh * HEAD_DIM:(hh + 1) * HEAD_DIM]
                           + gate(3 * h + 2) * o_win[hh * Q_BLOCK:(hh + 1) * Q_BLOCK])

    jl = lax.broadcasted_iota(jnp.int32, (QROWS, LANES), 1)
    qx = [jnp.concatenate([q4[g], aug_ref[g]], axis=1) for g in range(N_KV)]
    qx_past = [jnp.concatenate([q4[g], jnp.where(jl < BLOCKS_PER_Q * i, aug_ref[g], NEG_INF).astype(BF16)], axis=1)
               for g in range(N_KV)]
    b1 = _lane_tile(b1_ref[...], SLC_CHUNK // LANES)

    def gathered(ref, g, first, cols):
        parts = []
        for k in range(SLC_BLOCKS):
            slot = jnp.minimum(first + k, LANES - 1)
            k0 = pl.multiple_of(lists_ref[g * nqb + i, slot] * SEL_BLOCK, SEL_BLOCK)
            parts.append(ref[pl.ds(k0, SEL_BLOCK), cols])
        return jnp.concatenate(parts, axis=0)

    def slc_step(c, carry):
        m, acc = carry
        first = c * SLC_BLOCKS
        u = both(lambda g: _dot_nt(qx_past[g], jnp.concatenate(
            [gathered(ks_ref, g, first, gcols(g)), gathered(et_ref, g, first, slice(None))], axis=1))) + b1
        m_new = jnp.maximum(m, jnp.max(u, axis=1, keepdims=True))
        e = jnp.exp(u - m_new).astype(BF16)
        pv = both(lambda g: _dot(grows(e, g), jnp.concatenate(
            [gathered(vs_ref, g, first, gcols(g)), ones_col[:SLC_CHUNK]], axis=1)))
        return m_new, jnp.exp(m - m_new) * acc + pv

    n_blocks = jnp.maximum(counts_ref[i], counts_ref[nqb + i])
    n_trips = (n_blocks + (SLC_BLOCKS - 1)) // SLC_BLOCKS
    m, acc = lax.fori_loop(0, n_trips, slc_step,
                           (jnp.full((N_KV * QROWS, 1), NEG_INF, F32), jnp.zeros((N_KV * QROWS, 2 * HEAD_DIM), F32)))
    u = both(lambda g: _dot_nt(qx[g], jnp.concatenate(
        [ks_ref[pl.ds(t0, Q_BLOCK), gcols(g)], et_ref[pl.ds(t0, Q_BLOCK), :]], axis=1))) + b1d_ref[...]
    m_new = jnp.maximum(m, jnp.max(u, axis=1, keepdims=True))
    e = jnp.exp(u - m_new).astype(BF16)
    pv = both(lambda g: _dot(grows(e, g), jnp.concatenate(
        [vs_ref[pl.ds(t0, Q_BLOCK), gcols(g)], ones_col[:Q_BLOCK]], axis=1)))
    acc = jnp.exp(m - m_new) * acc + pv
    o_slc = acc[:, :HEAD_DIM] * (1.0 / acc[:, HEAD_DIM:])

    outs = [partial[hh] + slc_gate[hh] * o_slc[hh * Q_BLOCK:(hh + 1) * Q_BLOCK] for hh in range(N_HEADS)]
    o_ref[...] = jnp.concatenate(outs, axis=1).astype(BF16)


def _nsa_tables(t):
    nc = t // CMP_STRIDE
    f32 = np.float32
    r = (np.arange(QROWS) % Q_BLOCK)[:, None].astype(f32)
    slopes = np.exp2(-8.0 * np.arange(1, N_HEADS + 1, dtype=f32) / N_HEADS).astype(f32)
    slr = np.repeat(slopes.reshape(N_KV, HEADS_PER_KV), Q_BLOCK, axis=1)[:, :, None]
    bc = slr * ((np.arange(nc) * CMP_STRIDE + CMP_BLOCK - 1)[None, None, :] - r[None])
    dw = r - np.arange(WIN_KEYS)[None, :] + WINDOW
    bw = np.where((dw >= 0) & (dw < WINDOW), -slr * dw[None], f32(NEG_INF))
    kk = np.arange(LANES)[None, :]
    b1 = slr * (kk % SEL_BLOCK)[None]
    b1d = np.where(kk <= r, b1, f32(NEG_INF))
    sj = slr * (SEL_BLOCK * kk)[None]
    et = (np.arange(t)[:, None] // SEL_BLOCK == kk)
    cmp_start = np.arange(nc) * CMP_STRIDE
    sel_start = np.arange(LANES) * SEL_BLOCK
    ov = np.clip(np.minimum(cmp_start[:, None] + CMP_BLOCK - 1, sel_start[None] + SEL_BLOCK - 1)
                 - np.maximum(cmp_start[:, None], sel_start[None]) + 1, 0, None)
    ov[nc - 1] = 0
    lt = np.tril(np.ones((LANES, LANES)), -1)
    slr = slr * np.ones((1, 1, LANES), f32)
    as_f32 = lambda a: jnp.asarray(a.astype(f32))
    as_bf16 = lambda a: jnp.asarray(a.astype(f32), BF16)
    return dict(bc=as_f32(bc), bw=as_f32(bw), b1=as_f32(b1), b1d=as_f32(b1d), sj=as_f32(sj), slr=as_f32(slr),
                et=as_bf16(et), ov=as_bf16(ov), lt=as_bf16(lt))


def _nsa(o16, kvw, kvc, ng, t):
    tb = _nsa_tables(t)
    nc = t // CMP_STRIDE
    nqb = t // Q_BLOCK
    whole = lambda a: pl.BlockSpec(a.shape, lambda i: (0,) * a.ndim)
    ocmp, aug, lists = pl.pallas_call(
        _nsa_select_kernel,
        grid=(nqb,),
        in_specs=[
            pl.BlockSpec((Q_BLOCK, Q_W), lambda i: (i, 0)),
            whole(kvc), whole(tb["bc"]), whole(tb["sj"]), whole(tb["slr"]), whole(tb["ov"]), whole(tb["lt"]),
        ],
        out_specs=[
            pl.BlockSpec((Q_BLOCK, Q_W), lambda i: (i, 0)),
            pl.BlockSpec((N_KV, QROWS, LANES), lambda i: (0, i, 0)),
            pl.BlockSpec((None, SUBLANES, LANES), lambda i: (i, 0, 0)),
        ],
        out_shape=[
            jax.ShapeDtypeStruct((t, Q_W), F32),
            jax.ShapeDtypeStruct((N_KV, nqb * QROWS, LANES), BF16),
            jax.ShapeDtypeStruct((nqb, SUBLANES, LANES), jnp.int32),
        ],
        compiler_params=_cparams(("arbitrary",)),
        name="nsa_select",
    )(o16, kvc, tb["bc"], tb["sj"], tb["slr"], tb["ov"], tb["lt"])
    block_lists = lists[:, :N_KV, :].transpose(1, 0, 2).reshape(N_KV * nqb, LANES)
    counts = lists[:, N_KV:2 * N_KV, 0].T.reshape(N_KV * nqb)

    stacked = lambda a: a.reshape((N_KV * QROWS,) + a.shape[2:])
    bw, b1, b1d = stacked(tb["bw"]), stacked(tb["b1"]), stacked(tb["b1d"])
    once = pl.Buffered(1)
    const = lambda a: pl.BlockSpec(a.shape, lambda i, *_: (0,) * a.ndim, pipeline_mode=once)
    grid_spec = pltpu.PrefetchScalarGridSpec(
        num_scalar_prefetch=2,
        grid=(nqb,),
        in_specs=[
            pl.BlockSpec((Q_BLOCK, Q_W), lambda i, *_: (i, 0)),
            pl.BlockSpec((t, KV_W), lambda i, *_: (0, Q_W // KV_W), pipeline_mode=once),
            pl.BlockSpec((t, KV_W), lambda i, *_: (0, Q_W // KV_W + 1), pipeline_mode=once),
            pl.BlockSpec((t + WINDOW, KV_W), lambda i, *_: (0, 0), pipeline_mode=once),
            pl.BlockSpec((t + WINDOW, KV_W), lambda i, *_: (0, 1), pipeline_mode=once),
            pl.BlockSpec((N_KV, QROWS, LANES), lambda i, *_: (0, i, 0)),
            pl.BlockSpec((Q_BLOCK, Q_W), lambda i, *_: (i, 0)),
            pl.BlockSpec((Q_BLOCK, N_KV * LANES), lambda i, *_: (i, 0)),
            const(bw), const(b1), const(b1d), const(tb["et"]),
        ],
        out_specs=pl.BlockSpec((Q_BLOCK, Q_W), lambda i, *_: (i, 0)),
    )
    return pl.pallas_call(
        _nsa_attend_kernel,
        grid_spec=grid_spec,
        out_shape=jax.ShapeDtypeStruct((t, Q_W), BF16),
        compiler_params=_cparams(("arbitrary",)),
        name="nsa_attend",
    )(block_lists, counts, o16, o16, o16, kvw, kvw, aug, ocmp, ng, bw, b1, b1d, tb["et"])


SGU_CHUNKS_PER_STEP = 4


def _sgu_kernel(u_ref, v_ref, lng_ref, lnb_ref, ws_ref, bs_ref, o_ref):
    gu = jax.nn.gelu(u_ref[...].astype(F32))
    gv = jax.nn.gelu(v_ref[...].astype(F32))
    xc = gv - jnp.mean(gv, -1, keepdims=True)
    vn = (xc * lax.rsqrt(jnp.mean(xc * xc, -1, keepdims=True) + LN_EPS) * lng_ref[...] + lnb_ref[...]).astype(BF16)
    causal = (lax.broadcasted_iota(jnp.int32, (SGU_CHUNK, SGU_CHUNK), 0)
              >= lax.broadcasted_iota(jnp.int32, (SGU_CHUNK, SGU_CHUNK), 1))
    w = [jnp.where(causal, ws_ref[g], 0.0).astype(BF16) for g in range(SGU_GROUPS)]
    rows = []
    for c in range(SGU_CHUNKS_PER_STEP):
        vc = vn[c * SGU_CHUNK:(c + 1) * SGU_CHUNK]
        rows.append(jnp.concatenate(
            [_dot(w[g], vc[:, g * SGU_GROUP_DIM:(g + 1) * SGU_GROUP_DIM]) + bs_ref[g] for g in range(SGU_GROUPS)],
            axis=1))
    o_ref[...] = (gu * jnp.concatenate(rows, axis=0)).astype(BF16)


def _sgu(otail, lng, lnb, ws, bs, t):
    tm = SGU_CHUNKS_PER_STEP * SGU_CHUNK
    return pl.pallas_call(
        _sgu_kernel,
        grid=(t // tm,),
        in_specs=[
            pl.BlockSpec((tm, SGU_WIDTH), lambda i: (i, 0)),
            pl.BlockSpec((tm, SGU_WIDTH), lambda i: (i, 1)),
            pl.BlockSpec((1, SGU_WIDTH), lambda i: (0, 0)),
            pl.BlockSpec((1, SGU_WIDTH), lambda i: (0, 0)),
            pl.BlockSpec((SGU_GROUPS, SGU_CHUNK, SGU_CHUNK), lambda i: (0, 0, 0)),
            pl.BlockSpec((SGU_GROUPS, SGU_CHUNK, LANES), lambda i: (0, 0, 0)),
        ],
        out_specs=pl.BlockSpec((tm, SGU_WIDTH), lambda i: (i, 0)),
        out_shape=jax.ShapeDtypeStruct((t, SGU_WIDTH), BF16),
        compiler_params=_cparams(("arbitrary",)),
        name="sgu",
    )(otail, otail, lng, lnb, ws, bs)


def _mix_kernel(ya_ref, yb_ref, ga_ref, gb_ref, x_ref, pa_ref, pb_ref, wo_ref, g2_ref, x1_ref, h2_ref):
    mixed = (jax.nn.sigmoid(ga_ref[...].astype(F32)) * _dot(ya_ref[...], pa_ref[...])
             + jax.nn.sigmoid(gb_ref[...].astype(F32)) * _dot(yb_ref[...], pb_ref[...]))
    x1 = x_ref[...] + _dot(mixed.astype(BF16), wo_ref[...])
    x1_ref[...] = x1
    h2_ref[...] = _rms(x1, g2_ref[...]).astype(BF16)


def _mix(ya, yb, o32, x, pa, pb, wo, g2, t):
    tm = min(256, t)
    once = pl.Buffered(1)
    return pl.pallas_call(
        _mix_kernel,
        grid=(t // tm,),
        in_specs=[
            pl.BlockSpec((tm, Q_W), lambda i: (i, 0)),
            pl.BlockSpec((tm, SGU_WIDTH), lambda i: (i, 0)),
            pl.BlockSpec((tm, D_MODEL), lambda i: (i, 1)),
            pl.BlockSpec((tm, D_MODEL), lambda i: (i, 2)),
            pl.BlockSpec((tm, D_MODEL), lambda i: (i, 0)),
            pl.BlockSpec((Q_W, D_MODEL), lambda i: (0, 0), pipeline_mode=once),
            pl.BlockSpec((SGU_WIDTH, D_MODEL), lambda i: (0, 0), pipeline_mode=once),
            pl.BlockSpec((D_MODEL, D_MODEL), lambda i: (0, 0), pipeline_mode=once),
            pl.BlockSpec((1, D_MODEL), lambda i: (0, 0)),
        ],
        out_specs=[
            pl.BlockSpec((tm, D_MODEL), lambda i: (i, 0)),
            pl.BlockSpec((tm, D_MODEL), lambda i: (i, 0)),
        ],
        out_shape=[jax.ShapeDtypeStruct((t, D_MODEL), F32), jax.ShapeDtypeStruct((t, D_MODEL), BF16)],
        compiler_params=_cparams(("arbitrary",)),
        name="mix_out",
    )(ya, yb, o32, o32, x, pa, pb, wo, g2)


FFN_TILES = pl.cdiv(D_FF, FFN_TN)
FFN_BACK = FFN_TILES * FFN_TN - D_FF
FFN_MAIN = (FFN_TILES - 1) * FFN_TN


def _ffn_col(j, base=0):
    return (base // LANES + j * (FFN_TN // LANES) - (j // (FFN_TILES - 1)) * (FFN_BACK // LANES)) * LANES


def _ffn_in_kernel(h_ref, wa_ref, wb_ref, cw_ref, cb_ref, o_ref, wa_scr, wb_scr, a_scr):
    tm = h_ref.shape[0]

    @pl.when(pl.program_id(1) == 0)
    def _():
        wa_scr[...] = wa_ref[...].astype(BF16)
        wb_scr[...] = wb_ref[...].astype(BF16)
        a_scr[0:SUBLANES, :] = jnp.zeros((SUBLANES, FFN_TN), F32)

    h = h_ref[...]
    a = _dot(h, wa_scr[...])
    b = _dot(h, wb_scr[...])
    a_scr[SUBLANES:SUBLANES + tm, :] = a
    cw = cw_ref[...]
    y = (cw[0:1] * a_scr[pl.ds(SUBLANES - 2, tm), :] + cw[1:2] * a_scr[pl.ds(SUBLANES - 1, tm), :]
         + cw[2:3] * a + cb_ref[...])
    o_ref[...] = (jax.nn.gelu(y) * b).astype(BF16)
    a_scr[0:SUBLANES, :] = a[tm - SUBLANES:tm, :]


def _ffn_in(h2, w, cw, cb, t):
    tm = min(1024, t)
    tile = lambda rows, base: pl.BlockSpec((pl.Element(rows), pl.Element(FFN_TN)),
                                           lambda j, i: (0, _ffn_col(j, base)))
    return pl.pallas_call(
        _ffn_in_kernel,
        grid=(FFN_TILES, t // tm),
        in_specs=[
            pl.BlockSpec((tm, D_MODEL), lambda j, i: (i, 0)),
            tile(D_MODEL, 0),
            tile(D_MODEL, D_FF),
            tile(SUBLANES, 0),
            tile(1, 0),
        ],
        out_specs=pl.BlockSpec((tm, FFN_TN), lambda j, i: (i, j)),
        out_shape=jax.ShapeDtypeStruct((t, FFN_TILES * FFN_TN), BF16),
        scratch_shapes=[pltpu.VMEM((D_MODEL, FFN_TN), BF16), pltpu.VMEM((D_MODEL, FFN_TN), BF16),
                        pltpu.VMEM((tm + SUBLANES, FFN_TN), F32)],
        compiler_params=_cparams(("arbitrary", "arbitrary")),
        name="ffn_in",
    )(h2, w, w, cw, cb)


def _ffn_out_kernel(a0_ref, a1_ref, w0_ref, w1_ref, x_ref, g_ref, o_ref, *, final_norm):
    x2 = x_ref[...] + _dot(a0_ref[...], w0_ref[...]) + _dot(a1_ref[...], w1_ref[...])
    o_ref[...] = _rms(x2, g_ref[...]) if final_norm else x2


def _ffn_out(act, wd, x1, g, t, final_norm):
    tm = min(256, t)
    once = pl.Buffered(1)
    rest = D_FF - FFN_MAIN
    return pl.pallas_call(
        functools.partial(_ffn_out_kernel, final_norm=final_norm),
        grid=(t // tm,),
        in_specs=[
            pl.BlockSpec((pl.Element(tm), pl.Element(FFN_MAIN)), lambda i: (i * tm, 0)),
            pl.BlockSpec((pl.Element(tm), pl.Element(rest)), lambda i: (i * tm, FFN_MAIN + FFN_BACK)),
            pl.BlockSpec((pl.Element(FFN_MAIN), pl.Element(D_MODEL)), lambda i: (0, 0), pipeline_mode=once),
            pl.BlockSpec((pl.Element(rest), pl.Element(D_MODEL)), lambda i: (FFN_MAIN, 0), pipeline_mode=once),
            pl.BlockSpec((tm, D_MODEL), lambda i: (i, 0)),
            pl.BlockSpec((1, D_MODEL), lambda i: (0, 0)),
        ],
        out_specs=pl.BlockSpec((tm, D_MODEL), lambda i: (i, 0)),
        out_shape=jax.ShapeDtypeStruct((t, D_MODEL), F32),
        compiler_params=_cparams(("arbitrary",)),
        name="ffn_out",
    )(act, act, wd, wd, x1, g)


def _layer(x, p, final_g, t):
    offs = np.cumsum((0,) + IN_SIZES)
    w_in = p["w_in"].astype(BF16)
    wg = w_in[:, offs[7]:offs[8]].reshape(D_MODEL, N_KV, HEADS_PER_KV * 3)
    wg = jnp.pad(wg, ((0, 0), (0, 0), (0, LANES - HEADS_PER_KV * 3))).reshape(D_MODEL, N_KV * LANES)
    o16, otail, okvc, ng = _in_proj(x, p["norm1_g"][None, :], w_in, w_in[:, offs[8]:], wg)

    pe = jnp.stack([p["cmp_k_pe"], p["cmp_v_pe"]])
    w1 = jnp.stack([p["cmp_k_w1"], p["cmp_v_w1"]]).reshape(2, CMP_BLOCK, HEAD_DIM, HEAD_DIM).astype(BF16)
    w2 = jnp.stack([p["cmp_k_w2"], p["cmp_v_w2"]]).astype(BF16)
    kvc = _compress(okvc, 0, pe, w1, w2, t)

    kvw = jnp.pad(o16[:, Q_W + 2 * KV_W:Q_W + 4 * KV_W], ((WINDOW, 0), (0, 0)))
    y_a = _nsa(o16, kvw, kvc, ng, t)
    bs = jnp.broadcast_to(p["sgu_b"][:, :, None], (SGU_GROUPS, SGU_CHUNK, LANES))
    y_b = _sgu(otail, p["sgu_ln_g"][None, :], p["sgu_ln_b"][None, :], p["sgu_w"], bs, t)
    x1, h2 = _mix(y_a, y_b, otail, x, p["w_branch_a"].astype(BF16), p["w_branch_b"].astype(BF16),
                  p["w_out"].astype(BF16), p["norm2_g"][None, :], t)

    cw = jnp.pad(p["ffn_conv_w"], ((0, SUBLANES - CONV_WIDTH), (0, 0)))
    act = _ffn_in(h2, p["ffn_w_in"], cw, p["ffn_conv_b"][None, :], t)
    wd = p["ffn_w_down"].astype(BF16)
    g = final_g[None, :] if final_g is not None else jnp.ones((1, D_MODEL), F32)
    return _ffn_out(act, wd, x1, g, t, final_g is not None)


def kernel(x, norm1_g, w_in, cmp_k_pe, cmp_k_w1, cmp_k_w2, cmp_v_pe, cmp_v_w1, cmp_v_w2, sgu_ln_g, sgu_ln_b, sgu_w, sgu_b, w_branch_a, w_branch_b, w_out, norm2_g, ffn_w_in, ffn_conv_w, ffn_conv_b, ffn_w_down, final_g):
    b, t, _ = x.shape
    assert b == 1 and t % 1024 == 0, "one sequence whose length is a multiple of 1024"
    params = dict(norm1_g=norm1_g, w_in=w_in, cmp_k_pe=cmp_k_pe, cmp_k_w1=cmp_k_w1, cmp_k_w2=cmp_k_w2,
                  cmp_v_pe=cmp_v_pe, cmp_v_w1=cmp_v_w1, cmp_v_w2=cmp_v_w2, sgu_ln_g=sgu_ln_g, sgu_ln_b=sgu_ln_b,
                  sgu_w=sgu_w, sgu_b=sgu_b, w_branch_a=w_branch_a, w_branch_b=w_branch_b, w_out=w_out,
                  norm2_g=norm2_g, ffn_w_in=ffn_w_in, ffn_conv_w=ffn_conv_w, ffn_conv_b=ffn_conv_b,
                  ffn_w_down=ffn_w_down)
    depth = norm1_g.shape[0]
    h = x[0]
    for l in range(depth):
        layer = {k: v[l] for k, v in params.items()}
        h = _layer(h, layer, final_g if l == depth - 1 else None, t)
    return h[None]
```

```python
import functools

import numpy as np
import jax
import jax.numpy as jnp
from jax import lax
from jax.experimental import pallas as pl
from jax.experimental.pallas import tpu as pltpu

F32 = jnp.float32
BF16 = jnp.bfloat16

D_MODEL = 2048
N_HEADS = 8
HEAD_DIM = 128
N_KV = 2
HEADS_PER_KV = N_HEADS // N_KV
CMP_BLOCK = 32
CMP_STRIDE = 16
SEL_BLOCK = 64
N_SELECT = 16
N_FORCED = 3
WINDOW = 512
Q_BLOCK = 128
SGU_WIDTH = 1024
SGU_GROUPS = 8
SGU_GROUP_DIM = SGU_WIDTH // SGU_GROUPS
SGU_CHUNK = 128
D_FF = 5504
CONV_WIDTH = 3
NORM_EPS = 1e-6
LN_EPS = 1e-5
NEG_INF = -1e30

Q_W = N_HEADS * HEAD_DIM
KV_W = N_KV * HEAD_DIM
NSA_GATE_W = 3 * N_HEADS
IN_SIZES = (Q_W, KV_W, KV_W, KV_W, KV_W, KV_W, KV_W, NSA_GATE_W, SGU_WIDTH, SGU_WIDTH, D_MODEL, D_MODEL)

LANES = 128
SUBLANES = 8
QROWS = HEADS_PER_KV * Q_BLOCK
SEL_SHIFT = SEL_BLOCK.bit_length() - 1
BLOCKS_PER_Q = Q_BLOCK // SEL_BLOCK
SLC_BLOCKS = 10
SLC_CHUNK = SLC_BLOCKS * SEL_BLOCK
WIN_KEYS = WINDOW + Q_BLOCK
PROJ_TN = 512
N16_TILES = 4
FFN_TN = 512
VMEM_LIMIT = 56 * 1024 * 1024


def _cparams(sem):
    return pltpu.CompilerParams(dimension_semantics=sem, vmem_limit_bytes=VMEM_LIMIT)


def _dot(a, b):
    return jnp.dot(a, b, preferred_element_type=F32)


def _dot_nt(a, b):
    return lax.dot_general(a, b, (((1,), (1,)), ((), ())), preferred_element_type=F32)


def _rms(x, g):
    return x * lax.rsqrt(jnp.mean(x * x, -1, keepdims=True) + NORM_EPS) * g


N_TAIL_TILES = (SGU_WIDTH * 2 + D_MODEL * 2) // PROJ_TN
N_PROJ_TILES = N_TAIL_TILES + N16_TILES + 1


def _head_tile(j):
    k = j - N_TAIL_TILES
    return jnp.where(k <= 1, jnp.maximum(k, 0), jnp.where(k <= 3, k + 1, 2))


def _in_proj_kernel(x_ref, g_ref, wh_ref, wt_ref, wg_ref, ohead_ref, otail_ref, okvc_ref, ong_ref, h_scr):
    j = pl.program_id(1)

    @pl.when(j == 0)
    def _():
        h = _rms(x_ref[...], g_ref[...]).astype(BF16)
        h_scr[...] = h
        ong_ref[...] = _dot(h, wg_ref[...])

    acc = _dot(h_scr[...], jnp.where(j < N_TAIL_TILES, wt_ref[...], wh_ref[...]))
    okvc_ref[...] = acc
    is_q = (j == N_TAIL_TILES) | (j == N_TAIL_TILES + 1)
    y = (acc * jnp.where(is_q, HEAD_DIM ** -0.5, 1.0)).astype(BF16)
    ohead_ref[...] = y
    otail_ref[...] = y


def _in_proj(x, g, w_head, w_tail, wg):
    t = x.shape[0]
    tm = min(1024, t)
    return pl.pallas_call(
        _in_proj_kernel,
        grid=(t // tm, N_PROJ_TILES),
        in_specs=[
            pl.BlockSpec((tm, D_MODEL), lambda i, j: (i, 0)),
            pl.BlockSpec((1, D_MODEL), lambda i, j: (0, 0)),
            pl.BlockSpec((D_MODEL, PROJ_TN), lambda i, j: (0, _head_tile(j))),
            pl.BlockSpec((D_MODEL, PROJ_TN), lambda i, j: (0, jnp.minimum(j, N_TAIL_TILES - 1))),
            pl.BlockSpec((D_MODEL, 2 * LANES), lambda i, j: (0, 0)),
        ],
        out_specs=[
            pl.BlockSpec((tm, PROJ_TN), lambda i, j: (i, jnp.clip(j - N_TAIL_TILES, 0, N16_TILES))),
            pl.BlockSpec((tm, PROJ_TN), lambda i, j: (i, jnp.minimum(j, N_TAIL_TILES))),
            pl.BlockSpec((tm, PROJ_TN), lambda i, j: (i, 0)),
            pl.BlockSpec((tm, 2 * LANES), lambda i, j: (i, 0)),
        ],
        out_shape=[
            jax.ShapeDtypeStruct((t, (N16_TILES + 1) * PROJ_TN), BF16),
            jax.ShapeDtypeStruct((t, (N_TAIL_TILES + 1) * PROJ_TN), BF16),
            jax.ShapeDtypeStruct((t, PROJ_TN), F32),
            jax.ShapeDtypeStruct((t, 2 * LANES), F32),
        ],
        scratch_shapes=[pltpu.VMEM((tm, D_MODEL), BF16)],
        compiler_params=_cparams(("arbitrary", "arbitrary")),
        name="in_proj",
    )(x, g, w_head, w_tail, wg)


def _compress_kernel(kv_ref, pe_ref, w1_ref, w2_ref, o_ref):
    nu = o_ref.shape[0]
    a = jnp.zeros((nu, HEAD_DIM), F32)
    b = jnp.zeros((nu, HEAD_DIM), F32)
    for r in range(CMP_STRIDE):
        x = kv_ref[pl.ds(r, nu, stride=CMP_STRIDE), :]
        a += _dot((x + pe_ref[pl.ds(r, 1), :]).astype(BF16), w1_ref[r])
        b += _dot((x + pe_ref[pl.ds(CMP_STRIDE + r, 1), :]).astype(BF16), w1_ref[CMP_STRIDE + r])
    hid = a + pltpu.roll(b, nu - 1, 0)
    o_ref[...] = _dot(jax.nn.gelu(hid).astype(BF16), w2_ref[...]).astype(BF16)


def _compress(o32, col0, pe, w1, w2, t):
    nu = t // CMP_STRIDE
    return pl.pallas_call(
        _compress_kernel,
        grid=(2, N_KV),
        in_specs=[
            pl.BlockSpec((t, HEAD_DIM), lambda a, g: (0, col0 + N_KV * a + g)),
            pl.BlockSpec((None, CMP_BLOCK, HEAD_DIM), lambda a, g: (a, 0, 0)),
            pl.BlockSpec((None, CMP_BLOCK, HEAD_DIM, HEAD_DIM), lambda a, g: (a, 0, 0, 0)),
            pl.BlockSpec((None, HEAD_DIM, HEAD_DIM), lambda a, g: (a, 0, 0)),
        ],
        out_specs=pl.BlockSpec((None, None, nu, HEAD_DIM), lambda a, g: (a, g, 0, 0)),
        out_shape=jax.ShapeDtypeStruct((2, N_KV, nu, HEAD_DIM), BF16),
        compiler_params=_cparams(("arbitrary", "arbitrary")),
        name="compress",
    )(o32, pe, w1, w2)


def _lane_tile(x, n):
    return jnp.concatenate([x] * n, axis=1)


def _row_tile(x, n):
    return jnp.concatenate([x] * n, axis=0)


def _softmax_numerators(s):
    return jnp.exp(s - jnp.max(s, axis=1, keepdims=True)).astype(BF16)


def _stack_heads(q, g):
    return jnp.concatenate([q[:, (g * HEADS_PER_KV + h) * HEAD_DIM:(g * HEADS_PER_KV + h + 1) * HEAD_DIM]
                            for h in range(HEADS_PER_KV)], axis=0)


def _nsa_select_kernel(q_ref, kvc_ref, bc_ref, sj_ref, slr_ref, ov_ref, lt_ref, ocmp_ref, aug_ref, lists_ref):
    i = pl.program_id(0)
    t0 = i * Q_BLOCK
    t0f = t0.astype(F32)
    q = q_ref[...]
    nc = kvc_ref.shape[2]
    row = lax.broadcasted_iota(jnp.int32, (QROWS, 1), 0) & (Q_BLOCK - 1)
    jr = lax.broadcasted_iota(jnp.int32, (LANES, Q_BLOCK), 0)
    kl = lax.broadcasted_iota(jnp.int32, (LANES, Q_BLOCK), 1)
    tq = t0 + kl
    cur = lax.shift_right_logical(tq, SEL_SHIFT)
    forced = (jr == 0) | (jr == cur) | (jr == cur - 1)
    visible = jr * SEL_BLOCK <= tq
    jf = jr.astype(F32)

    ocmp, imps = [], []
    for g in range(N_KV):
        bc = bc_ref[g]
        s = jnp.where(bc <= _lane_tile(slr_ref[g] * t0f, nc // LANES),
                      _dot_nt(_stack_heads(q, g), kvc_ref[0, g]) + bc, NEG_INF)
        r = _dot(_softmax_numerators(s), jnp.concatenate([kvc_ref[1, g], ov_ref[...]], axis=1))
        l = jnp.sum(r[:, HEAD_DIM:], axis=1, keepdims=True) * (1.0 / CMP_BLOCK)
        inv = jnp.where(t0 + row >= CMP_BLOCK - 1, 1.0 / l, 0.0)
        o = r[:, :HEAD_DIM] * inv
        ocmp += [o[h * Q_BLOCK:(h + 1) * Q_BLOCK] for h in range(HEADS_PER_KV)]
        imp = r[:, HEAD_DIM:] * inv
        imps.append(sum(imp[h * Q_BLOCK:(h + 1) * Q_BLOCK] for h in range(HEADS_PER_KV)).T)
    ocmp_ref[...] = jnp.concatenate(ocmp, axis=1)

    forced2, visible2, jf2 = (_lane_tile(a, N_KV) for a in (forced, visible, jf))
    score = jnp.where(forced2 | jnp.logical_not(visible2), NEG_INF, jnp.concatenate(imps, axis=1))
    for _ in range(N_SELECT - N_FORCED):
        mx = jnp.max(score, axis=0, keepdims=True)
        idx = jnp.min(jnp.where(score == mx, jf2, float(LANES)), axis=0, keepdims=True)
        score = jnp.where(jf2 == idx, -jnp.inf, score)
    sel_all = jnp.where(visible2 & (forced2 | (score == -jnp.inf)), 1.0, 0.0)

    rows = []
    for g in range(N_KV):
        sel_t = sel_all[:, g * Q_BLOCK:(g + 1) * Q_BLOCK]
        aug_ref[g] = jnp.where(_row_tile(sel_t.T, HEADS_PER_KV) > 0.5, sj_ref[g], NEG_INF).astype(BF16)

        used = jnp.max(jnp.where(jr < BLOCKS_PER_Q * i, sel_t, 0.0), axis=1, keepdims=True)
        used = jnp.broadcast_to(used, (LANES, Q_BLOCK))
        slot = _dot(lt_ref[...], used.astype(BF16))
        lst = jnp.sum(jnp.where((slot == kl.astype(F32)) & (used > 0.5), jf, 0.0), axis=0, keepdims=True)
        cnt = jnp.sum(used, axis=0, keepdims=True)
        lst = jnp.where(kl[:1].astype(F32) < cnt, lst, (BLOCKS_PER_Q * i).astype(F32))
        rows += [lst, cnt]
    pad =[jnp.zeros((1, LANES), F32)] * (SUBLANES - len(rows))
    lists_ref[...] = jnp.concatenate(rows[0::2] + rows[1::2] + pad, axis=0).astype(jnp.int32)


def _nsa_attend_kernel(lists_ref, counts_ref, q_ref, ks_ref, vs_ref, kw_ref, vw_ref, aug_ref, ocmp_ref, ng_ref,
                       bw_ref, b1_ref, b1d_ref, et_ref, o_ref):
    i = pl.program_id(0)
    nqb = pl.num_programs(0)
    t0 = i * Q_BLOCK
    q = q_ref[...]
    q4 = [_stack_heads(q, g) for g in range(N_KV)]
    ones_col = jnp.ones((WIN_KEYS, LANES), BF16)
    gcols = lambda g: slice(g * HEAD_DIM, (g + 1) * HEAD_DIM)
    grows = lambda a, g: a[g * QROWS:(g + 1) * QROWS]
    both = lambda f: jnp.concatenate([f(g) for g in range(N_KV)], axis=0)

    w0 = pl.multiple_of(jnp.maximum(t0 - WINDOW, 0), Q_BLOCK)
    shift = pl.multiple_of(w0 - (t0 - WINDOW), Q_BLOCK)
    s = both(lambda g: _dot_nt(q4[g], kw_ref[pl.ds(w0, WIN_KEYS), gcols(g)])) + bw_ref[:, pl.ds(shift, WIN_KEYS)]
    e = _softmax_numerators(s)
    r = both(lambda g: _dot(grows(e, g), jnp.concatenate([vw_ref[pl.ds(w0, WIN_KEYS), gcols(g)], ones_col], axis=1)))
    o_win = r[:, :HEAD_DIM] * (1.0 / r[:, HEAD_DIM:])

    gl = lax.broadcasted_iota(jnp.int32, (Q_BLOCK, LANES), 1)
    o_cmp = ocmp_ref[...]
    slc_gate, partial = [], []
    for g in range(N_KV):
        sg = jax.nn.sigmoid(ng_ref[:, gcols(g)])
        gate = lambda col: jnp.sum(jnp.where(gl == col, sg, 0.0), axis=1, keepdims=True)
        for h in range(HEADS_PER_KV):
            hh = g * HEADS_PER_KV + h
            slc_gate.append(gate(3 * h + 1))
            partial.append(gate(3 * h) * o_cmp[:, hh * HEAD_DIM:(hh + 1) * HEAD_DIM]
                           + gate(3 * h + 2) * o_win[hh * Q_BLOCK:(hh + 1) * Q_BLOCK])

    jl = lax.broadcasted_iota(jnp.int32, (QROWS, LANES), 1)
    qx = [jnp.concatenate([q4[g], aug_ref[g]], axis=1) for g in range(N_KV)]
    qx_past = [jnp.concatenate([q4[g], jnp.where(jl < BLOCKS_PER_Q * i, aug_ref[g], NEG_INF).astype(BF16)], axis=1)
               for g in range(N_KV)]
    b1 = _lane_tile(b1_ref[...], SLC_CHUNK // LANES)

    def gathered(ref, g, first, cols):
        parts = []
        for k in range(SLC_BLOCKS):
            slot = jnp.minimum(first + k, LANES - 1)
            k0 = pl.multiple_of(lists_ref[g * nqb + i, slot] * SEL_BLOCK, SEL_BLOCK)
            parts.append(ref[pl.ds(k0, SEL_BLOCK), cols])
        return jnp.concatenate(parts, axis=0)

    def slc_step(c, carry):
        m, acc = carry
        first = c * SLC_BLOCKS
        u = both(lambda g: _dot_nt(qx_past[g], jnp.concatenate(
            [gathered(ks_ref, g, first, gcols(g)), gathered(et_ref, g, first, slice(None))], axis=1))) + b1
        m_new = jnp.maximum(m, jnp.max(u, axis=1, keepdims=True))
        e = jnp.exp(u - m_new).astype(BF16)
        pv = both(lambda g: _dot(grows(e, g), jnp.concatenate(
            [gathered(vs_ref, g, first, gcols(g)), ones_col[:SLC_CHUNK]], axis=1)))
        return m_new, jnp.exp(m - m_new) * acc + pv

    n_blocks = jnp.maximum(counts_ref[i], counts_ref[nqb + i])
    n_trips = (n_blocks + (SLC_BLOCKS - 1)) // SLC_BLOCKS
    m, acc = lax.fori_loop(0, n_trips, slc_step,
                           (jnp.full((N_KV * QROWS, 1), NEG_INF, F32), jnp.zeros((N_KV * QROWS, 2 * HEAD_DIM), F32)))
    u = both(lambda g: _dot_nt(qx[g], jnp.concatenate(
        [ks_ref[pl.ds(t0, Q_BLOCK), gcols(g)], et_ref[pl.ds(t0, Q_BLOCK), :]], axis=1))) + b1d_ref[...]
    m_new = jnp.maximum(m, jnp.max(u, axis=1, keepdims=True))
    e = jnp.exp(u - m_new).astype(BF16)
    pv = both(lambda g: _dot(grows(e, g), jnp.concatenate(
        [vs_ref[pl.ds(t0, Q_BLOCK), gcols(g)], ones_col[:Q_BLOCK]], axis=1)))
    acc = jnp.exp(m - m_new) * acc + pv
    o_slc = acc[:, :HEAD_DIM] * (1.0 / acc[:, HEAD_DIM:])

    outs = [partial[hh] + slc_gate[hh] * o_slc[hh * Q_BLOCK:(hh + 1) * Q_BLOCK] for hh in range(N_HEADS)]
    o_ref[...] = jnp.concatenate(outs, axis=1).astype(BF16)


def _nsa_tables(t):
    nc = t // CMP_STRIDE
    f32 = np.float32
    r = (np.arange(QROWS) % Q_BLOCK)[:, None].astype(f32)
    slopes = np.exp2(-8.0 * np.arange(1, N_HEADS + 1, dtype=f32) / N_HEADS).astype(f32)
    slr = np.repeat(slopes.reshape(N_KV, HEADS_PER_KV), Q_BLOCK, axis=1)[:, :, None]
    bc = slr * ((np.arange(nc) * CMP_STRIDE + CMP_BLOCK - 1)[None, None, :] - r[None])
    dw = r - np.arange(WIN_KEYS)[None, :] + WINDOW
    bw = np.where((dw >= 0) & (dw < WINDOW), -slr * dw[None], f32(NEG_INF))
    bw = np.concatenate([bw, np.full(bw.shape[:2] + (WINDOW,), NEG_INF, f32)], axis=2)
    kk = np.arange(LANES)[None, :]
    b1 = slr * (kk % SEL_BLOCK)[None]
    b1d = np.where(kk <= r, b1, f32(NEG_INF))
    sj = slr * (SEL_BLOCK * kk)[None]
    et = (np.arange(t)[:, None] // SEL_BLOCK == kk)
    cmp_start = np.arange(nc) * CMP_STRIDE
    sel_start = np.arange(LANES) * SEL_BLOCK
    ov = np.clip(np.minimum(cmp_start[:, None] + CMP_BLOCK - 1, sel_start[None] + SEL_BLOCK - 1)
                 - np.maximum(cmp_start[:, None], sel_start[None]) + 1, 0, None)
    ov[nc - 1] = 0
    lt = np.tril(np.ones((LANES, LANES)), -1)
    slr = slr * np.ones((1, 1, LANES), f32)
    as_f32 = lambda a: jnp.asarray(a.astype(f32))
    as_bf16 = lambda a: jnp.asarray(a.astype(f32), BF16)
    return dict(bc=as_f32(bc), bw=as_f32(bw), b1=as_f32(b1), b1d=as_f32(b1d), sj=as_f32(sj), slr=as_f32(slr),
                et=as_bf16(et), ov=as_bf16(ov), lt=as_bf16(lt))


def _nsa(o16, kvc, ng, t):
    tb = _nsa_tables(t)
    nc = t // CMP_STRIDE
    nqb = t // Q_BLOCK
    whole = lambda a: pl.BlockSpec(a.shape, lambda i: (0,) * a.ndim)
    ocmp, aug, lists = pl.pallas_call(
        _nsa_select_kernel,
        grid=(nqb,),
        in_specs=[
            pl.BlockSpec((Q_BLOCK, Q_W), lambda i: (i, 0)),
            whole(kvc), whole(tb["bc"]), whole(tb["sj"]), whole(tb["slr"]), whole(tb["ov"]), whole(tb["lt"]),
        ],
        out_specs=[
            pl.BlockSpec((Q_BLOCK, Q_W), lambda i: (i, 0)),
            pl.BlockSpec((N_KV, QROWS, LANES), lambda i: (0, i, 0)),
            pl.BlockSpec((None, SUBLANES, LANES), lambda i: (i, 0, 0)),
        ],
        out_shape=[
            jax.ShapeDtypeStruct((t, Q_W), F32),
            jax.ShapeDtypeStruct((N_KV, nqb * QROWS, LANES), BF16),
            jax.ShapeDtypeStruct((nqb, SUBLANES, LANES), jnp.int32),
        ],
        compiler_params=_cparams(("arbitrary",)),
        name="nsa_select",
    )(o16, kvc, tb["bc"], tb["sj"], tb["slr"], tb["ov"], tb["lt"])
    block_lists = lists[:, :N_KV, :].transpose(1, 0, 2).reshape(N_KV * nqb, LANES)
    counts = lists[:, N_KV:2 * N_KV, 0].T.reshape(N_KV * nqb)

    stacked = lambda a: a.reshape((N_KV * QROWS,) + a.shape[2:])
    bw, b1, b1d = stacked(tb["bw"]), stacked(tb["b1"]), stacked(tb["b1d"])
    once = pl.Buffered(1)
    const = lambda a: pl.BlockSpec(a.shape, lambda i, *_: (0,) * a.ndim, pipeline_mode=once)
    grid_spec = pltpu.PrefetchScalarGridSpec(
        num_scalar_prefetch=2,
        grid=(nqb,),
        in_specs=[
            pl.BlockSpec((Q_BLOCK, Q_W), lambda i, *_: (i, 0)),
            pl.BlockSpec((t, KV_W), lambda i, *_: (0, Q_W // KV_W), pipeline_mode=once),
            pl.BlockSpec((t, KV_W), lambda i, *_: (0, Q_W // KV_W + 1), pipeline_mode=once),
            pl.BlockSpec((t, KV_W), lambda i, *_: (0, Q_W // KV_W + 2), pipeline_mode=once),
            pl.BlockSpec((t, KV_W), lambda i, *_: (0, Q_W // KV_W + 3), pipeline_mode=once),
            pl.BlockSpec((N_KV, QROWS, LANES), lambda i, *_: (0, i, 0)),
            pl.BlockSpec((Q_BLOCK, Q_W), lambda i, *_: (i, 0)),
            pl.BlockSpec((Q_BLOCK, N_KV * LANES), lambda i, *_: (i, 0)),
            const(bw), const(b1), const(b1d), const(tb["et"]),
        ],
        out_specs=pl.BlockSpec((Q_BLOCK, Q_W), lambda i, *_: (i, 0)),
    )
    return pl.pallas_call(
        _nsa_attend_kernel,
        grid_spec=grid_spec,
        out_shape=jax.ShapeDtypeStruct((t, Q_W), BF16),
        compiler_params=_cparams(("arbitrary",)),
        name="nsa_attend",
    )(block_lists, counts, o16, o16, o16, o16, o16, aug, ocmp, ng, bw, b1, b1d, tb["et"])


SGU_CHUNKS_PER_STEP = 4


def _sgu_kernel(u_ref, v_ref, lng_ref, lnb_ref, ws_ref, bs_ref, o_ref):
    gu = jax.nn.gelu(u_ref[...].astype(F32))
    gv = jax.nn.gelu(v_ref[...].astype(F32))
    xc = gv - jnp.mean(gv, -1, keepdims=True)
    vn = (xc * lax.rsqrt(jnp.mean(xc * xc, -1, keepdims=True) + LN_EPS) * lng_ref[...] + lnb_ref[...]).astype(BF16)
    causal = (lax.broadcasted_iota(jnp.int32, (SGU_CHUNK, SGU_CHUNK), 0)
              >= lax.broadcasted_iota(jnp.int32, (SGU_CHUNK, SGU_CHUNK), 1))
    w = [jnp.where(causal, ws_ref[g], 0.0).astype(BF16) for g in range(SGU_GROUPS)]
    rows = []
    for c in range(SGU_CHUNKS_PER_STEP):
        vc = vn[c * SGU_CHUNK:(c + 1) * SGU_CHUNK]
        rows.append(jnp.concatenate(
            [_dot(w[g], vc[:, g * SGU_GROUP_DIM:(g + 1) * SGU_GROUP_DIM]) + bs_ref[g] for g in range(SGU_GROUPS)],
            axis=1))
    o_ref[...] = (gu * jnp.concatenate(rows, axis=0)).astype(BF16)


def _sgu(otail, lng, lnb, ws, bs, t):
    tm = SGU_CHUNKS_PER_STEP * SGU_CHUNK
    return pl.pallas_call(
        _sgu_kernel,
        grid=(t // tm,),
        in_specs=[
            pl.BlockSpec((tm, SGU_WIDTH), lambda i: (i, 0)),
            pl.BlockSpec((tm, SGU_WIDTH), lambda i: (i, 1)),
            pl.BlockSpec((1, SGU_WIDTH), lambda i: (0, 0)),
            pl.BlockSpec((1, SGU_WIDTH), lambda i: (0, 0)),
            pl.BlockSpec((SGU_GROUPS, SGU_CHUNK, SGU_CHUNK), lambda i: (0, 0, 0)),
            pl.BlockSpec((SGU_GROUPS, SGU_CHUNK, LANES), lambda i: (0, 0, 0)),
        ],
        out_specs=pl.BlockSpec((tm, SGU_WIDTH), lambda i: (i, 0)),
        out_shape=jax.ShapeDtypeStruct((t, SGU_WIDTH), BF16),
        compiler_params=_cparams(("arbitrary",)),
        name="sgu",
    )(otail, otail, lng, lnb, ws, bs)


def _mix_kernel(ya_ref, yb_ref, ga_ref, gb_ref, x_ref, pa_ref, pb_ref, wo_ref, g2_ref, x1_ref, h2_ref):
    mixed = (jax.nn.sigmoid(ga_ref[...].astype(F32)) * _dot(ya_ref[...], pa_ref[...])
             + jax.nn.sigmoid(gb_ref[...].astype(F32)) * _dot(yb_ref[...], pb_ref[...]))
    x1 = x_ref[...] + _dot(mixed.astype(BF16), wo_ref[...])
    x1_ref[...] = x1
    h2_ref[...] = _rms(x1, g2_ref[...]).astype(BF16)


def _mix(ya, yb, o32, x, pa, pb, wo, g2, t):
    tm = min(256, t)
    once = pl.Buffered(1)
    return pl.pallas_call(
        _mix_kernel,
        grid=(t // tm,),
        in_specs=[
            pl.BlockSpec((tm, Q_W), lambda i: (i, 0)),
            pl.BlockSpec((tm, SGU_WIDTH), lambda i: (i, 0)),
            pl.BlockSpec((tm, D_MODEL), lambda i: (i, 1)),
            pl.BlockSpec((tm, D_MODEL), lambda i: (i, 2)),
            pl.BlockSpec((tm, D_MODEL), lambda i: (i, 0)),
            pl.BlockSpec((Q_W, D_MODEL), lambda i: (0, 0), pipeline_mode=once),
            pl.BlockSpec((SGU_WIDTH, D_MODEL), lambda i: (0, 0), pipeline_mode=once),
            pl.BlockSpec((D_MODEL, D_MODEL), lambda i: (0, 0), pipeline_mode=once),
            pl.BlockSpec((1, D_MODEL), lambda i: (0, 0)),
        ],
        out_specs=[
            pl.BlockSpec((tm, D_MODEL), lambda i: (i, 0)),
            pl.BlockSpec((tm, D_MODEL), lambda i: (i, 0)),
        ],
        out_shape=[jax.ShapeDtypeStruct((t, D_MODEL), F32), jax.ShapeDtypeStruct((t, D_MODEL), BF16)],
        compiler_params=_cparams(("arbitrary",)),
        name="mix_out",
    )(ya, yb, o32, o32, x, pa, pb, wo, g2)


FFN_TILES = pl.cdiv(D_FF, FFN_TN)
FFN_BACK = FFN_TILES * FFN_TN - D_FF
FFN_MAIN = (FFN_TILES - 1) * FFN_TN


def _ffn_col(j, base=0):
    return (base // LANES + j * (FFN_TN // LANES) - (j // (FFN_TILES - 1)) * (FFN_BACK // LANES)) * LANES


def _ffn_in_kernel(h_ref, wa_ref, wb_ref, cw_ref, cb_ref, o_ref, wa_scr, wb_scr, a_scr):
    tm = h_ref.shape[0]

    @pl.when(pl.program_id(1) == 0)
    def _():
        wa_scr[...] = wa_ref[...].astype(BF16)
        wb_scr[...] = wb_ref[...].astype(BF16)
        a_scr[0:SUBLANES, :] = jnp.zeros((SUBLANES, FFN_TN), F32)

    h = h_ref[...]
    a = _dot(h, wa_scr[...])
    b = _dot(h, wb_scr[...])
    a_scr[SUBLANES:SUBLANES + tm, :] = a
    cw = cw_ref[...]
    y = (cw[0:1] * a_scr[pl.ds(SUBLANES - 2, tm), :] + cw[1:2] * a_scr[pl.ds(SUBLANES - 1, tm), :]
         + cw[2:3] * a + cb_ref[...])
    o_ref[...] = (jax.nn.gelu(y) * b).astype(BF16)
    a_scr[0:SUBLANES, :] = a[tm - SUBLANES:tm, :]


def _ffn_in(h2, w, cw, cb, t):
    tm = min(1024, t)
    tile = lambda rows, base: pl.BlockSpec((pl.Element(rows), pl.Element(FFN_TN)),
                                           lambda j, i: (0, _ffn_col(j, base)))
    return pl.pallas_call(
        _ffn_in_kernel,
        grid=(FFN_TILES, t // tm),
        in_specs=[
            pl.BlockSpec((tm, D_MODEL), lambda j, i: (i, 0)),
            tile(D_MODEL, 0),
            tile(D_MODEL, D_FF),
            tile(SUBLANES, 0),
            tile(1, 0),
        ],
        out_specs=pl.BlockSpec((tm, FFN_TN), lambda j, i: (i, j)),
        out_shape=jax.ShapeDtypeStruct((t, FFN_TILES * FFN_TN), BF16),
        scratch_shapes=[pltpu.VMEM((D_MODEL, FFN_TN), BF16), pltpu.VMEM((D_MODEL, FFN_TN), BF16),
                        pltpu.VMEM((tm + SUBLANES, FFN_TN), F32)],
        compiler_params=_cparams(("arbitrary", "arbitrary")),
        name="ffn_in",
    )(h2, w, w, cw, cb)


def _ffn_out_kernel(a0_ref, a1_ref, w0_ref, w1_ref, x_ref, g_ref, o_ref, *, final_norm):
    x2 = x_ref[...] + _dot(a0_ref[...], w0_ref[...]) + _dot(a1_ref[...], w1_ref[...])
    o_ref[...] = _rms(x2, g_ref[...]) if final_norm else x2


def _ffn_out(act, wd, x1, g, t, final_norm):
    tm = min(256, t)
    once = pl.Buffered(1)
    rest = D_FF - FFN_MAIN
    return pl.pallas_call(
        functools.partial(_ffn_out_kernel, final_norm=final_norm),
        grid=(t // tm,),
        in_specs=[
            pl.BlockSpec((pl.Element(tm), pl.Element(FFN_MAIN)), lambda i: (i * tm, 0)),
            pl.BlockSpec((pl.Element(tm), pl.Element(rest)), lambda i: (i * tm, FFN_MAIN + FFN_BACK)),
            pl.BlockSpec((pl.Element(FFN_MAIN), pl.Element(D_MODEL)), lambda i: (0, 0), pipeline_mode=once),
            pl.BlockSpec((pl.Element(rest), pl.Element(D_MODEL)), lambda i: (FFN_MAIN, 0), pipeline_mode=once),
            pl.BlockSpec((tm, D_MODEL), lambda i: (i, 0)),
            pl.BlockSpec((1, D_MODEL), lambda i: (0, 0)),
        ],
        out_specs=pl.BlockSpec((tm, D_MODEL), lambda i: (i, 0)),
        out_shape=jax.ShapeDtypeStruct((t, D_MODEL), F32),
        compiler_params=_cparams(("arbitrary",)),
        name="ffn_out",
    )(act, act, wd, wd, x1, g)


def _layer(x, p, final_g, t):
    offs = np.cumsum((0,) + IN_SIZES)
    w_in = p["w_in"]
    wg = w_in[:, offs[7]:offs[8]].astype(BF16).reshape(D_MODEL, N_KV, HEADS_PER_KV * 3)
    wg = jnp.pad(wg, ((0, 0), (0, 0), (0, LANES - HEADS_PER_KV * 3))).reshape(D_MODEL, N_KV * LANES)
    o16, otail, okvc, ng = _in_proj(x, p["norm1_g"][None, :], w_in[:, :offs[7]].astype(BF16),
                                    w_in[:, offs[8]:].astype(BF16), wg)

    pe = jnp.stack([p["cmp_k_pe"], p["cmp_v_pe"]])
    w1 = jnp.stack([p["cmp_k_w1"], p["cmp_v_w1"]]).reshape(2, CMP_BLOCK, HEAD_DIM, HEAD_DIM).astype(BF16)
    w2 = jnp.stack([p["cmp_k_w2"], p["cmp_v_w2"]]).astype(BF16)
    kvc = _compress(okvc, 0, pe, w1, w2, t)

    y_a = _nsa(o16, kvc, ng, t)
    bs = jnp.broadcast_to(p["sgu_b"][:, :, None], (SGU_GROUPS, SGU_CHUNK, LANES))
    y_b = _sgu(otail, p["sgu_ln_g"][None, :], p["sgu_ln_b"][None, :], p["sgu_w"], bs, t)
    x1, h2 = _mix(y_a, y_b, otail, x, p["w_branch_a"].astype(BF16), p["w_branch_b"].astype(BF16),
                  p["w_out"].astype(BF16), p["norm2_g"][None, :], t)

    cw = jnp.pad(p["ffn_conv_w"], ((0, SUBLANES - CONV_WIDTH), (0, 0)))
    act = _ffn_in(h2, p["ffn_w_in"], cw, p["ffn_conv_b"][None, :], t)
    wd = p["ffn_w_down"].astype(BF16)
    g = final_g[None, :] if final_g is not None else jnp.ones((1, D_MODEL), F32)
    return _ffn_out(act, wd, x1, g, t, final_g is not None)


def kernel(x, norm1_g, w_in, cmp_k_pe, cmp_k_w1, cmp_k_w2, cmp_v_pe, cmp_v_w1, cmp_v_w2, sgu_ln_g, sgu_ln_b, sgu_w, sgu_b, w_branch_a, w_branch_b, w_out, norm2_g, ffn_w_in, ffn_conv_w, ffn_conv_b, ffn_w_down, final_g):
    b, t, _ = x.shape
    assert b == 1 and t % 1024 == 0, "one sequence whose length is a multiple of 1024"
    params = dict(norm1_g=norm1_g, w_in=w_in, cmp_k_pe=cmp_k_pe, cmp_k_w1=cmp_k_w1, cmp_k_w2=cmp_k_w2,
                  cmp_v_pe=cmp_v_pe, cmp_v_w1=cmp_v_w1, cmp_v_w2=cmp_v_w2, sgu_ln_g=sgu_ln_g, sgu_ln_b=sgu_ln_b,
                  sgu_w=sgu_w, sgu_b=sgu_b, w_branch_a=w_branch_a, w_branch_b=w_branch_b, w_out=w_out,
                  norm2_g=norm2_g, ffn_w_in=ffn_w_in, ffn_conv_w=ffn_conv_w, ffn_conv_b=ffn_conv_b,
                  ffn_w_down=ffn_w_down)
    depth = norm1_g.shape[0]
    h = x[0]
    for l in range(depth):
        layer = {k: v[l] for k, v in params.items()}
        h = _layer(h, layer, final_g if l == depth - 1 else None, t)
    return h[None]
```

```python
import functools

import numpy as np
import jax
import jax.numpy as jnp
from jax import lax
from jax.experimental import pallas as pl
from jax.experimental.pallas import tpu as pltpu

F32 = jnp.float32
BF16 = jnp.bfloat16

D_MODEL = 2048
N_HEADS = 8
HEAD_DIM = 128
N_KV = 2
HEADS_PER_KV = N_HEADS // N_KV
CMP_BLOCK = 32
CMP_STRIDE = 16
SEL_BLOCK = 64
N_SELECT = 16
N_FORCED = 3
WINDOW = 512
Q_BLOCK = 128
SGU_WIDTH = 1024
SGU_GROUPS = 8
SGU_GROUP_DIM = SGU_WIDTH // SGU_GROUPS
SGU_CHUNK = 128
D_FF = 5504
CONV_WIDTH = 3
NORM_EPS = 1e-6
LN_EPS = 1e-5
NEG_INF = -1e30

Q_W = N_HEADS * HEAD_DIM
KV_W = N_KV * HEAD_DIM
NSA_GATE_W = 3 * N_HEADS
IN_SIZES = (Q_W, KV_W, KV_W, KV_W, KV_W, KV_W, KV_W, NSA_GATE_W, SGU_WIDTH, SGU_WIDTH, D_MODEL, D_MODEL)

LANES = 128
SUBLANES = 8
QROWS = HEADS_PER_KV * Q_BLOCK
SEL_SHIFT = SEL_BLOCK.bit_length() - 1
BLOCKS_PER_Q = Q_BLOCK // SEL_BLOCK
SLC_BLOCKS = 10
SLC_CHUNK = SLC_BLOCKS * SEL_BLOCK
WIN_KEYS = WINDOW + Q_BLOCK
PROJ_TN = 512
N16_TILES = 4
FFN_TN = 512
VMEM_LIMIT = 56 * 1024 * 1024


def _cparams(sem):
    return pltpu.CompilerParams(dimension_semantics=sem, vmem_limit_bytes=VMEM_LIMIT)


def _dot(a, b):
    return jnp.dot(a, b, preferred_element_type=F32)


def _dot_nt(a, b):
    return lax.dot_general(a, b, (((1,), (1,)), ((), ())), preferred_element_type=F32)


def _rms(x, g):
    return x * lax.rsqrt(jnp.mean(x * x, -1, keepdims=True) + NORM_EPS) * g


N_TAIL_TILES = (SGU_WIDTH * 2 + D_MODEL * 2) // PROJ_TN
N_PROJ_TILES = N_TAIL_TILES + N16_TILES + 1


def _head_tile(j):
    k = j - N_TAIL_TILES
    return jnp.where(k <= 1, jnp.maximum(k, 0), jnp.where(k <= 3, k + 1, 2))


def _in_proj_kernel(x_ref, g_ref, wh_ref, wt_ref, wg_ref, ohead_ref, otail_ref, okvc_ref, ong_ref, h_scr):
    j = pl.program_id(1)

    @pl.when(j == 0)
    def _():
        h = _rms(x_ref[...], g_ref[...]).astype(BF16)
        h_scr[...] = h
        ong_ref[...] = _dot(h, wg_ref[...])

    acc = _dot(h_scr[...], jnp.where(j < N_TAIL_TILES, wt_ref[...], wh_ref[...]))
    okvc_ref[...] = acc
    is_q = (j == N_TAIL_TILES) | (j == N_TAIL_TILES + 1)
    y = (acc * jnp.where(is_q, HEAD_DIM ** -0.5, 1.0)).astype(BF16)
    ohead_ref[...] = y
    otail_ref[...] = y


def _in_proj(x, g, w_head, w_tail, wg):
    t = x.shape[0]
    tm = min(1024, t)
    return pl.pallas_call(
        _in_proj_kernel,
        grid=(t // tm, N_PROJ_TILES),
        in_specs=[
            pl.BlockSpec((tm, D_MODEL), lambda i, j: (i, 0)),
            pl.BlockSpec((1, D_MODEL), lambda i, j: (0, 0)),
            pl.BlockSpec((D_MODEL, PROJ_TN), lambda i, j: (0, _head_tile(j))),
            pl.BlockSpec((D_MODEL, PROJ_TN), lambda i, j: (0, jnp.minimum(j, N_TAIL_TILES - 1))),
            pl.BlockSpec((D_MODEL, 2 * LANES), lambda i, j: (0, 0)),
        ],
        out_specs=[
            pl.BlockSpec((tm, PROJ_TN), lambda i, j: (i, jnp.clip(j - N_TAIL_TILES, 0, N16_TILES))),
            pl.BlockSpec((tm, PROJ_TN), lambda i, j: (i, jnp.minimum(j, N_TAIL_TILES))),
            pl.BlockSpec((tm, PROJ_TN), lambda i, j: (i, 0)),
            pl.BlockSpec((tm, 2 * LANES), lambda i, j: (i, 0)),
        ],
        out_shape=[
            jax.ShapeDtypeStruct((t, (N16_TILES + 1) * PROJ_TN), BF16),
            jax.ShapeDtypeStruct((t, (N_TAIL_TILES + 1) * PROJ_TN), BF16),
            jax.ShapeDtypeStruct((t, PROJ_TN), F32),
            jax.ShapeDtypeStruct((t, 2 * LANES), F32),
        ],
        scratch_shapes=[pltpu.VMEM((tm, D_MODEL), BF16)],
        compiler_params=_cparams(("arbitrary", "arbitrary")),
        name="in_proj",
    )(x, g, w_head, w_tail, wg)


def _compress_kernel(kv_ref, pe_ref, w1_ref, w2_ref, o_ref):
    nu = o_ref.shape[0]
    a = jnp.zeros((nu, HEAD_DIM), F32)
    b = jnp.zeros((nu, HEAD_DIM), F32)
    for r in range(CMP_STRIDE):
        x = kv_ref[pl.ds(r, nu, stride=CMP_STRIDE), :]
        a += _dot((x + pe_ref[pl.ds(r, 1), :]).astype(BF16), w1_ref[r])
        b += _dot((x + pe_ref[pl.ds(CMP_STRIDE + r, 1), :]).astype(BF16), w1_ref[CMP_STRIDE + r])
    hid = a + pltpu.roll(b, nu - 1, 0)
    o_ref[...] = _dot(jax.nn.gelu(hid).astype(BF16), w2_ref[...]).astype(BF16)


def _compress(o32, col0, pe, w1, w2, t):
    nu = t // CMP_STRIDE
    return pl.pallas_call(
        _compress_kernel,
        grid=(2, N_KV),
        in_specs=[
            pl.BlockSpec((t, HEAD_DIM), lambda a, g: (0, col0 + N_KV * a + g)),
            pl.BlockSpec((None, CMP_BLOCK, HEAD_DIM), lambda a, g: (a, 0, 0)),
            pl.BlockSpec((None, CMP_BLOCK, HEAD_DIM, HEAD_DIM), lambda a, g: (a, 0, 0, 0)),
            pl.BlockSpec((None, HEAD_DIM, HEAD_DIM), lambda a, g: (a, 0, 0)),
        ],
        out_specs=pl.BlockSpec((None, None, nu, HEAD_DIM), lambda a, g: (a, g, 0, 0)),
        out_shape=jax.ShapeDtypeStruct((2, N_KV, nu, HEAD_DIM), BF16),
        compiler_params=_cparams(("arbitrary", "arbitrary")),
        name="compress",
    )(o32, pe, w1, w2)


def _lane_tile(x, n):
    return jnp.concatenate([x] * n, axis=1)


def _row_tile(x, n):
    return jnp.concatenate([x] * n, axis=0)


def _softmax_numerators(s):
    return jnp.exp(s - jnp.max(s, axis=1, keepdims=True)).astype(BF16)


def _stack_heads(q, g):
    return jnp.concatenate([q[:, (g * HEADS_PER_KV + h) * HEAD_DIM:(g * HEADS_PER_KV + h + 1) * HEAD_DIM]
                            for h in range(HEADS_PER_KV)], axis=0)


def _nsa_select_kernel(q_ref, kvc_ref, bc_ref, sj_ref, slr_ref, ov_ref, lt_ref, ocmp_ref, aug_ref, lists_ref):
    i = pl.program_id(0)
    t0 = i * Q_BLOCK
    t0f = t0.astype(F32)
    q = q_ref[...]
    nc = kvc_ref.shape[2]
    row = lax.broadcasted_iota(jnp.int32, (QROWS, 1), 0) & (Q_BLOCK - 1)
    jr = lax.broadcasted_iota(jnp.int32, (LANES, Q_BLOCK), 0)
    kl = lax.broadcasted_iota(jnp.int32, (LANES, Q_BLOCK), 1)
    tq = t0 + kl
    cur = lax.shift_right_logical(tq, SEL_SHIFT)
    forced = (jr == 0) | (jr == cur) | (jr == cur - 1)
    visible = jr * SEL_BLOCK <= tq
    jf = jr.astype(F32)

    ocmp, imps = [], []
    for g in range(N_KV):
        bc = bc_ref[g]
        s = jnp.where(bc <= _lane_tile(slr_ref[g] * t0f, nc // LANES),
                      _dot_nt(_stack_heads(q, g), kvc_ref[0, g]) + bc, NEG_INF)
        r = _dot(_softmax_numerators(s), jnp.concatenate([kvc_ref[1, g], ov_ref[...]], axis=1))
        l = jnp.sum(r[:, HEAD_DIM:], axis=1, keepdims=True) * (1.0 / CMP_BLOCK)
        inv = jnp.where(t0 + row >= CMP_BLOCK - 1, 1.0 / l, 0.0)
        o = r[:, :HEAD_DIM] * inv
        ocmp += [o[h * Q_BLOCK:(h + 1) * Q_BLOCK] for h in range(HEADS_PER_KV)]
        imp = r[:, HEAD_DIM:] * inv
        imps.append(sum(imp[h * Q_BLOCK:(h + 1) * Q_BLOCK] for h in range(HEADS_PER_KV)).T)
    ocmp_ref[...] = jnp.concatenate(ocmp, axis=1)

    forced2, visible2, jf2 = (_lane_tile(a, N_KV) for a in (forced, visible, jf))
    score = jnp.where(forced2 | jnp.logical_not(visible2), NEG_INF, jnp.concatenate(imps, axis=1))
    for _ in range(N_SELECT - N_FORCED):
        mx = jnp.max(score, axis=0, keepdims=True)
        idx = jnp.min(jnp.where(score == mx, jf2, float(LANES)), axis=0, keepdims=True)
        score = jnp.where(jf2 == idx, -jnp.inf, score)
    sel_all = jnp.where(visible2 & (forced2 | (score == -jnp.inf)), 1.0, 0.0)

    rows = []
    for g in range(N_KV):
        sel_t = sel_all[:, g * Q_BLOCK:(g + 1) * Q_BLOCK]
        aug_ref[g] = jnp.where(_row_tile(sel_t.T, HEADS_PER_KV) > 0.5, sj_ref[g], NEG_INF).astype(BF16)

        used = jnp.max(jnp.where(jr < BLOCKS_PER_Q * i, sel_t, 0.0), axis=1, keepdims=True)
        used = jnp.broadcast_to(used, (LANES, Q_BLOCK))
        slot = _dot(lt_ref[...], used.astype(BF16))
        lst = jnp.sum(jnp.where((slot == kl.astype(F32)) & (used > 0.5), jf, 0.0), axis=0, keepdims=True)
        cnt = jnp.sum(used, axis=0, keepdims=True)
        lst = jnp.where(kl[:1].astype(F32) < cnt, lst, (BLOCKS_PER_Q * i).astype(F32))
        rows += [lst, cnt]
    pad =[jnp.zeros((1, LANES), F32)] * (SUBLANES - len(rows))
    lists_ref[...] = jnp.concatenate(rows[0::2] + rows[1::2] + pad, axis=0).astype(jnp.int32)


def _nsa_attend_kernel(lists_ref, counts_ref, q_ref, ks_ref, vs_ref, kw_ref, vw_ref, aug_ref, ocmp_ref, ng_ref,
                       bw_ref, b1_ref, b1d_ref, et_ref, o_ref):
    i = pl.program_id(0)
    nqb = pl.num_programs(0)
    t0 = i * Q_BLOCK
    q = q_ref[...]
    q4 = [_stack_heads(q, g) for g in range(N_KV)]
    ones_col = jnp.ones((WIN_KEYS, LANES), BF16)
    gcols = lambda g: slice(g * HEAD_DIM, (g + 1) * HEAD_DIM)
    grows = lambda a, g: a[g * QROWS:(g + 1) * QROWS]
    both = lambda f: jnp.concatenate([f(g) for g in range(N_KV)], axis=0)

    w0 = pl.multiple_of(jnp.maximum(t0 - WINDOW, 0), Q_BLOCK)
    shift = pl.multiple_of(w0 - (t0 - WINDOW), Q_BLOCK)
    s = both(lambda g: _dot_nt(q4[g], kw_ref[pl.ds(w0, WIN_KEYS), gcols(g)])) + bw_ref[:, pl.ds(shift, WIN_KEYS)]
    e = _softmax_numerators(s)
    r = both(lambda g: _dot(grows(e, g), jnp.concatenate([vw_ref[pl.ds(w0, WIN_KEYS), gcols(g)], ones_col], axis=1)))
    o_win = r[:, :HEAD_DIM] * (1.0 / r[:, HEAD_DIM:])

    gl = lax.broadcasted_iota(jnp.int32, (Q_BLOCK, LANES), 1)
    o_cmp = ocmp_ref[...]
    slc_gate, partial = [], []
    for g in range(N_KV):
        sg = jax.nn.sigmoid(ng_ref[:, gcols(g)])
        gate = lambda col: jnp.sum(jnp.where(gl == col, sg, 0.0), axis=1, keepdims=True)
        for h in range(HEADS_PER_KV):
            hh = g * HEADS_PER_KV + h
            slc_gate.append(gate(3 * h + 1))
            partial.append(gate(3 * h) * o_cmp[:, hh * HEAD_DIM:(hh + 1) * HEAD_DIM]
                           + gate(3 * h + 2) * o_win[hh * Q_BLOCK:(hh + 1) * Q_BLOCK])

    jl = lax.broadcasted_iota(jnp.int32, (QROWS, LANES), 1)
    qx = [jnp.concatenate([q4[g], aug_ref[g]], axis=1) for g in range(N_KV)]
    qx_past = [jnp.concatenate([q4[g], jnp.where(jl < BLOCKS_PER_Q * i, aug_ref[g], NEG_INF).astype(BF16)], axis=1)
               for g in range(N_KV)]
    b1 = _lane_tile(b1_ref[...], SLC_CHUNK // LANES)

    def gathered(ref, g, first, cols):
        parts = []
        for k in range(SLC_BLOCKS):
            slot = jnp.minimum(first + k, LANES - 1)
            k0 = pl.multiple_of(lists_ref[g * nqb + i, slot] * SEL_BLOCK, SEL_BLOCK)
            parts.append(ref[pl.ds(k0, SEL_BLOCK), cols])
        return jnp.concatenate(parts, axis=0)

    def slc_step(c, carry):
        m, acc = carry
        first = c * SLC_BLOCKS
        u = both(lambda g: _dot_nt(qx_past[g], jnp.concatenate(
            [gathered(ks_ref, g, first, gcols(g)), gathered(et_ref, g, first, slice(None))], axis=1))) + b1
        m_new = jnp.maximum(m, jnp.max(u, axis=1, keepdims=True))
        e = jnp.exp(u - m_new).astype(BF16)
        pv = both(lambda g: _dot(grows(e, g), jnp.concatenate(
            [gathered(vs_ref, g, first, gcols(g)), ones_col[:SLC_CHUNK]], axis=1)))
        return m_new, jnp.exp(m - m_new) * acc + pv

    n_blocks = jnp.maximum(counts_ref[i], counts_ref[nqb + i])
    n_trips = (n_blocks + (SLC_BLOCKS - 1)) // SLC_BLOCKS
    m, acc = lax.fori_loop(0, n_trips, slc_step,
                           (jnp.full((N_KV * QROWS, 1), NEG_INF, F32), jnp.zeros((N_KV * QROWS, 2 * HEAD_DIM), F32)))
    u = both(lambda g: _dot_nt(qx[g], jnp.concatenate(
        [ks_ref[pl.ds(t0, Q_BLOCK), gcols(g)], et_ref[pl.ds(t0, Q_BLOCK), :]], axis=1))) + b1d_ref[...]
    m_new = jnp.maximum(m, jnp.max(u, axis=1, keepdims=True))
    e = jnp.exp(u - m_new).astype(BF16)
    pv = both(lambda g: _dot(grows(e, g), jnp.concatenate(
        [vs_ref[pl.ds(t0, Q_BLOCK), gcols(g)], ones_col[:Q_BLOCK]], axis=1)))
    acc = jnp.exp(m - m_new) * acc + pv
    o_slc = acc[:, :HEAD_DIM] * (1.0 / acc[:, HEAD_DIM:])

    outs = [partial[hh] + slc_gate[hh] * o_slc[hh * Q_BLOCK:(hh + 1) * Q_BLOCK] for hh in range(N_HEADS)]
    o_ref[...] = jnp.concatenate(outs, axis=1).astype(BF16)


def _nsa_tables(t):
    nc = t // CMP_STRIDE
    f32 = np.float32
    r = (np.arange(QROWS) % Q_BLOCK)[:, None].astype(f32)
    slopes = np.exp2(-8.0 * np.arange(1, N_HEADS + 1, dtype=f32) / N_HEADS).astype(f32)
    slr = np.repeat(slopes.reshape(N_KV, HEADS_PER_KV), Q_BLOCK, axis=1)[:, :, None]
    bc = slr * ((np.arange(nc) * CMP_STRIDE + CMP_BLOCK - 1)[None, None, :] - r[None])
    dw = r - np.arange(WIN_KEYS)[None, :] + WINDOW
    bw = np.where((dw >= 0) & (dw < WINDOW), -slr * dw[None], f32(NEG_INF))
    bw = np.concatenate([bw, np.full(bw.shape[:2] + (WINDOW,), NEG_INF, f32)], axis=2)
    kk = np.arange(LANES)[None, :]
    b1 = slr * (kk % SEL_BLOCK)[None]
    b1d = np.where(kk <= r, b1, f32(NEG_INF))
    sj = slr * (SEL_BLOCK * kk)[None]
    et = (np.arange(t)[:, None] // SEL_BLOCK == kk)
    cmp_start = np.arange(nc) * CMP_STRIDE
    sel_start = np.arange(LANES) * SEL_BLOCK
    ov = np.clip(np.minimum(cmp_start[:, None] + CMP_BLOCK - 1, sel_start[None] + SEL_BLOCK - 1)
                 - np.maximum(cmp_start[:, None], sel_start[None]) + 1, 0, None)
    ov[nc - 1] = 0
    lt = np.tril(np.ones((LANES, LANES)), -1)
    slr = slr * np.ones((1, 1, LANES), f32)
    as_f32 = lambda a: jnp.asarray(a.astype(f32))
    as_bf16 = lambda a: jnp.asarray(a.astype(f32), BF16)
    return dict(bc=as_f32(bc), bw=as_f32(bw), b1=as_f32(b1), b1d=as_f32(b1d), sj=as_f32(sj), slr=as_f32(slr),
                et=as_bf16(et), ov=as_bf16(ov), lt=as_bf16(lt))


def _nsa(o16, kvc, ng, t):
    tb = _nsa_tables(t)
    nc = t // CMP_STRIDE
    nqb = t // Q_BLOCK
    whole = lambda a: pl.BlockSpec(a.shape, lambda i: (0,) * a.ndim)
    ocmp, aug, lists = pl.pallas_call(
        _nsa_select_kernel,
        grid=(nqb,),
        in_specs=[
            pl.BlockSpec((Q_BLOCK, Q_W), lambda i: (i, 0)),
            whole(kvc), whole(tb["bc"]), whole(tb["sj"]), whole(tb["slr"]), whole(tb["ov"]), whole(tb["lt"]),
        ],
        out_specs=[
            pl.BlockSpec((Q_BLOCK, Q_W), lambda i: (i, 0)),
            pl.BlockSpec((N_KV, QROWS, LANES), lambda i: (0, i, 0)),
            pl.BlockSpec((None, SUBLANES, LANES), lambda i: (i, 0, 0)),
        ],
        out_shape=[
            jax.ShapeDtypeStruct((t, Q_W), F32),
            jax.ShapeDtypeStruct((N_KV, nqb * QROWS, LANES), BF16),
            jax.ShapeDtypeStruct((nqb, SUBLANES, LANES), jnp.int32),
        ],
        compiler_params=_cparams(("arbitrary",)),
        name="nsa_select",
    )(o16, kvc, tb["bc"], tb["sj"], tb["slr"], tb["ov"], tb["lt"])
    block_lists = lists[:, :N_KV, :].transpose(1, 0, 2).reshape(N_KV * nqb, LANES)
    counts = lists[:, N_KV:2 * N_KV, 0].T.reshape(N_KV * nqb)

    stacked = lambda a: a.reshape((N_KV * QROWS,) + a.shape[2:])
    bw, b1, b1d = stacked(tb["bw"]), stacked(tb["b1"]), stacked(tb["b1d"])
    once = pl.Buffered(1)
    const = lambda a: pl.BlockSpec(a.shape, lambda i, *_: (0,) * a.ndim, pipeline_mode=once)
    grid_spec = pltpu.PrefetchScalarGridSpec(
        num_scalar_prefetch=2,
        grid=(nqb,),
        in_specs=[
            pl.BlockSpec((Q_BLOCK, Q_W), lambda i, *_: (i, 0)),
            pl.BlockSpec((t, KV_W), lambda i, *_: (0, Q_W // KV_W), pipeline_mode=once),
            pl.BlockSpec((t, KV_W), lambda i, *_: (0, Q_W // KV_W + 1), pipeline_mode=once),
            pl.BlockSpec((t, KV_W), lambda i, *_: (0, Q_W // KV_W + 2), pipeline_mode=once),
            pl.BlockSpec((t, KV_W), lambda i, *_: (0, Q_W // KV_W + 3), pipeline_mode=once),
            pl.BlockSpec((N_KV, QROWS, LANES), lambda i, *_: (0, i, 0)),
            pl.BlockSpec((Q_BLOCK, Q_W), lambda i, *_: (i, 0)),
            pl.BlockSpec((Q_BLOCK, N_KV * LANES), lambda i, *_: (i, 0)),
            const(bw), const(b1), const(b1d), const(tb["et"]),
        ],
        out_specs=pl.BlockSpec((Q_BLOCK, Q_W), lambda i, *_: (i, 0)),
    )
    return pl.pallas_call(
        _nsa_attend_kernel,
        grid_spec=grid_spec,
        out_shape=jax.ShapeDtypeStruct((t, Q_W), BF16),
        compiler_params=_cparams(("arbitrary",)),
        name="nsa_attend",
    )(block_lists, counts, o16, o16, o16, o16, o16, aug, ocmp, ng, bw, b1, b1d, tb["et"])


SGU_CHUNKS_PER_STEP = 4


def _sgu_kernel(u_ref, v_ref, lng_ref, lnb_ref, ws_ref, bs_ref, o_ref):
    gu = jax.nn.gelu(u_ref[...].astype(F32))
    gv = jax.nn.gelu(v_ref[...].astype(F32))
    xc = gv - jnp.mean(gv, -1, keepdims=True)
    vn = (xc * lax.rsqrt(jnp.mean(xc * xc, -1, keepdims=True) + LN_EPS) * lng_ref[...] + lnb_ref[...]).astype(BF16)
    causal = (lax.broadcasted_iota(jnp.int32, (SGU_CHUNK, SGU_CHUNK), 0)
              >= lax.broadcasted_iota(jnp.int32, (SGU_CHUNK, SGU_CHUNK), 1))
    w = [jnp.where(causal, ws_ref[g], 0.0).astype(BF16) for g in range(SGU_GROUPS)]
    rows = []
    for c in range(SGU_CHUNKS_PER_STEP):
        vc = vn[c * SGU_CHUNK:(c + 1) * SGU_CHUNK]
        rows.append(jnp.concatenate(
            [_dot(w[g], vc[:, g * SGU_GROUP_DIM:(g + 1) * SGU_GROUP_DIM]) + bs_ref[g] for g in range(SGU_GROUPS)],
            axis=1))
    o_ref[...] = (gu * jnp.concatenate(rows, axis=0)).astype(BF16)


def _sgu(otail, lng, lnb, ws, bs, t):
    tm = SGU_CHUNKS_PER_STEP * SGU_CHUNK
    return pl.pallas_call(
        _sgu_kernel,
        grid=(t // tm,),
        in_specs=[
            pl.BlockSpec((tm, SGU_WIDTH), lambda i: (i, 0)),
            pl.BlockSpec((tm, SGU_WIDTH), lambda i: (i, 1)),
            pl.BlockSpec((1, SGU_WIDTH), lambda i: (0, 0)),
            pl.BlockSpec((1, SGU_WIDTH), lambda i: (0, 0)),
            pl.BlockSpec((SGU_GROUPS, SGU_CHUNK, SGU_CHUNK), lambda i: (0, 0, 0)),
            pl.BlockSpec((SGU_GROUPS, SGU_CHUNK, LANES), lambda i: (0, 0, 0)),
        ],
        out_specs=pl.BlockSpec((tm, SGU_WIDTH), lambda i: (i, 0)),
        out_shape=jax.ShapeDtypeStruct((t, SGU_WIDTH), BF16),
        compiler_params=_cparams(("arbitrary",)),
        name="sgu",
    )(otail, otail, lng, lnb, ws, bs)


def _mix_kernel(ya_ref, yb_ref, ga_ref, gb_ref, x_ref, pa_ref, pb_ref, wo_ref, g2_ref, x1_ref, h2_ref):
    mixed = (jax.nn.sigmoid(ga_ref[...].astype(F32)) * _dot(ya_ref[...], pa_ref[...])
             + jax.nn.sigmoid(gb_ref[...].astype(F32)) * _dot(yb_ref[...], pb_ref[...]))
    x1 = x_ref[...] + _dot(mixed.astype(BF16), wo_ref[...])
    x1_ref[...] = x1
    h2_ref[...] = _rms(x1, g2_ref[...]).astype(BF16)


def _mix(ya, yb, o32, x, pa, pb, wo, g2, t):
    tm = min(512, t)
    once = pl.Buffered(1)
    return pl.pallas_call(
        _mix_kernel,
        grid=(t // tm,),
        in_specs=[
            pl.BlockSpec((tm, Q_W), lambda i: (i, 0)),
            pl.BlockSpec((tm, SGU_WIDTH), lambda i: (i, 0)),
            pl.BlockSpec((tm, D_MODEL), lambda i: (i, 1)),
            pl.BlockSpec((tm, D_MODEL), lambda i: (i, 2)),
            pl.BlockSpec((tm, D_MODEL), lambda i: (i, 0)),
            pl.BlockSpec((Q_W, D_MODEL), lambda i: (0, 0), pipeline_mode=once),
            pl.BlockSpec((SGU_WIDTH, D_MODEL), lambda i: (0, 0), pipeline_mode=once),
            pl.BlockSpec((D_MODEL, D_MODEL), lambda i: (0, 0), pipeline_mode=once),
            pl.BlockSpec((1, D_MODEL), lambda i: (0, 0)),
        ],
        out_specs=[
            pl.BlockSpec((tm, D_MODEL), lambda i: (i, 0)),
            pl.BlockSpec((tm, D_MODEL), lambda i: (i, 0)),
        ],
        out_shape=[jax.ShapeDtypeStruct((t, D_MODEL), F32), jax.ShapeDtypeStruct((t, D_MODEL), BF16)],
        compiler_params=_cparams(("arbitrary",)),
        name="mix_out",
    )(ya, yb, o32, o32, x, pa, pb, wo, g2)


FFN_TILES = pl.cdiv(D_FF, FFN_TN)
FFN_BACK = FFN_TILES * FFN_TN - D_FF
FFN_MAIN = (FFN_TILES - 1) * FFN_TN


def _ffn_col(j, base=0):
    return (base // LANES + j * (FFN_TN // LANES) - (j // (FFN_TILES - 1)) * (FFN_BACK // LANES)) * LANES


def _ffn_in_kernel(h_ref, wa_ref, wb_ref, cw_ref, cb_ref, o_ref, wa_scr, wb_scr, a_scr):
    tm = h_ref.shape[0]

    @pl.when(pl.program_id(1) == 0)
    def _():
        wa_scr[...] = wa_ref[...].astype(BF16)
        wb_scr[...] = wb_ref[...].astype(BF16)
        a_scr[0:SUBLANES, :] = jnp.zeros((SUBLANES, FFN_TN), F32)

    h = h_ref[...]
    a = _dot(h, wa_scr[...])
    b = _dot(h, wb_scr[...])
    a_scr[SUBLANES:SUBLANES + tm, :] = a
    cw = cw_ref[...]
    y = (cw[0:1] * a_scr[pl.ds(SUBLANES - 2, tm), :] + cw[1:2] * a_scr[pl.ds(SUBLANES - 1, tm), :]
         + cw[2:3] * a + cb_ref[...])
    o_ref[...] = (jax.nn.gelu(y) * b).astype(BF16)
    a_scr[0:SUBLANES, :] = a[tm - SUBLANES:tm, :]


def _ffn_in(h2, w, cw, cb, t):
    tm = min(1024, t)
    tile = lambda rows, base: pl.BlockSpec((pl.Element(rows), pl.Element(FFN_TN)),
                                           lambda j, i: (0, _ffn_col(j, base)))
    return pl.pallas_call(
        _ffn_in_kernel,
        grid=(FFN_TILES, t // tm),
        in_specs=[
            pl.BlockSpec((tm, D_MODEL), lambda j, i: (i, 0)),
            tile(D_MODEL, 0),
            tile(D_MODEL, D_FF),
            tile(SUBLANES, 0),
            tile(1, 0),
        ],
        out_specs=pl.BlockSpec((tm, FFN_TN), lambda j, i: (i, j)),
        out_shape=jax.ShapeDtypeStruct((t, FFN_TILES * FFN_TN), BF16),
        scratch_shapes=[pltpu.VMEM((D_MODEL, FFN_TN), BF16), pltpu.VMEM((D_MODEL, FFN_TN), BF16),
                        pltpu.VMEM((tm + SUBLANES, FFN_TN), F32)],
        compiler_params=_cparams(("arbitrary", "arbitrary")),
        name="ffn_in",
    )(h2, w, w, cw, cb)


def _ffn_out_kernel(a0_ref, a1_ref, w0_ref, w1_ref, x_ref, g_ref, o_ref, *, final_norm):
    x2 = x_ref[...] + _dot(a0_ref[...], w0_ref[...]) + _dot(a1_ref[...], w1_ref[...])
    o_ref[...] = _rms(x2, g_ref[...]) if final_norm else x2


def _ffn_out(act, wd, x1, g, t, final_norm):
    tm = min(512, t)
    once = pl.Buffered(1)
    rest = D_FF - FFN_MAIN
    return pl.pallas_call(
        functools.partial(_ffn_out_kernel, final_norm=final_norm),
        grid=(t // tm,),
        in_specs=[
            pl.BlockSpec((pl.Element(tm), pl.Element(FFN_MAIN)), lambda i: (i * tm, 0)),
            pl.BlockSpec((pl.Element(tm), pl.Element(rest)), lambda i: (i * tm, FFN_MAIN + FFN_BACK)),
            pl.BlockSpec((pl.Element(FFN_MAIN), pl.Element(D_MODEL)), lambda i: (0, 0), pipeline_mode=once),
            pl.BlockSpec((pl.Element(rest), pl.Element(D_MODEL)), lambda i: (FFN_MAIN, 0), pipeline_mode=once),
            pl.BlockSpec((tm, D_MODEL), lambda i: (i, 0)),
            pl.BlockSpec((1, D_MODEL), lambda i: (0, 0)),
        ],
        out_specs=pl.BlockSpec((tm, D_MODEL), lambda i: (i, 0)),
        out_shape=jax.ShapeDtypeStruct((t, D_MODEL), F32),
        compiler_params=_cparams(("arbitrary",)),
        name="ffn_out",
    )(act, act, wd, wd, x1, g)


def _layer(x, p, final_g, t):
    offs = np.cumsum((0,) + IN_SIZES)
    w_in = p["w_in"]
    wg = w_in[:, offs[7]:offs[8]].astype(BF16).reshape(D_MODEL, N_KV, HEADS_PER_KV * 3)
    wg = jnp.pad(wg, ((0, 0), (0, 0), (0, LANES - HEADS_PER_KV * 3))).reshape(D_MODEL, N_KV * LANES)
    o16, otail, okvc, ng = _in_proj(x, p["norm1_g"][None, :], w_in[:, :offs[7]].astype(BF16),
                                    w_in[:, offs[8]:].astype(BF16), wg)

    pe = jnp.stack([p["cmp_k_pe"], p["cmp_v_pe"]])
    w1 = jnp.stack([p["cmp_k_w1"], p["cmp_v_w1"]]).reshape(2, CMP_BLOCK, HEAD_DIM, HEAD_DIM).astype(BF16)
    w2 = jnp.stack([p["cmp_k_w2"], p["cmp_v_w2"]]).astype(BF16)
    kvc = _compress(okvc, 0, pe, w1, w2, t)

    y_a = _nsa(o16, kvc, ng, t)
    bs = jnp.broadcast_to(p["sgu_b"][:, :, None], (SGU_GROUPS, SGU_CHUNK, LANES))
    y_b = _sgu(otail, p["sgu_ln_g"][None, :], p["sgu_ln_b"][None, :], p["sgu_w"], bs, t)
    x1, h2 = _mix(y_a, y_b, otail, x, p["w_branch_a"].astype(BF16), p["w_branch_b"].astype(BF16),
                  p["w_out"].astype(BF16), p["norm2_g"][None, :], t)

    cw = jnp.pad(p["ffn_conv_w"], ((0, SUBLANES - CONV_WIDTH), (0, 0)))
    act = _ffn_in(h2, p["ffn_w_in"], cw, p["ffn_conv_b"][None, :], t)
    wd = p["ffn_w_down"].astype(BF16)
    g = final_g[None, :] if final_g is not None else jnp.ones((1, D_MODEL), F32)
    return _ffn_out(act, wd, x1, g, t, final_g is not None)


def kernel(x, norm1_g, w_in, cmp_k_pe, cmp_k_w1, cmp_k_w2, cmp_v_pe, cmp_v_w1, cmp_v_w2, sgu_ln_g, sgu_ln_b, sgu_w, sgu_b, w_branch_a, w_branch_b, w_out, norm2_g, ffn_w_in, ffn_conv_w, ffn_conv_b, ffn_w_down, final_g):
    b, t, _ = x.shape
    assert b == 1 and t % 1024 == 0, "one sequence whose length is a multiple of 1024"
    params = dict(norm1_g=norm1_g, w_in=w_in, cmp_k_pe=cmp_k_pe, cmp_k_w1=cmp_k_w1, cmp_k_w2=cmp_k_w2,
                  cmp_v_pe=cmp_v_pe, cmp_v_w1=cmp_v_w1, cmp_v_w2=cmp_v_w2, sgu_ln_g=sgu_ln_g, sgu_ln_b=sgu_ln_b,
                  sgu_w=sgu_w, sgu_b=sgu_b, w_branch_a=w_branch_a, w_branch_b=w_branch_b, w_out=w_out,
                  norm2_g=norm2_g, ffn_w_in=ffn_w_in, ffn_conv_w=ffn_conv_w, ffn_conv_b=ffn_conv_b,
                  ffn_w_down=ffn_w_down)
    depth = norm1_g.shape[0]
    h = x[0]
    for l in range(depth):
        layer = {k: v[l] for k, v in params.items()}
        h = _layer(h, layer, final_g if l == depth - 1 else None, t)
    return h[None]
```

```python
import functools

import numpy as np
import jax
import jax.numpy as jnp
from jax import lax
from jax.experimental import pallas as pl
from jax.experimental.pallas import tpu as pltpu

F32 = jnp.float32
BF16 = jnp.bfloat16

D_MODEL = 2048
N_HEADS = 8
HEAD_DIM = 128
N_KV = 2
HEADS_PER_KV = N_HEADS // N_KV
CMP_BLOCK = 32
CMP_STRIDE = 16
SEL_BLOCK = 64
N_SELECT = 16
N_FORCED = 3
WINDOW = 512
Q_BLOCK = 128
SGU_WIDTH = 1024
SGU_GROUPS = 8
SGU_GROUP_DIM = SGU_WIDTH // SGU_GROUPS
SGU_CHUNK = 128
D_FF = 5504
CONV_WIDTH = 3
NORM_EPS = 1e-6
LN_EPS = 1e-5
NEG_INF = -1e30

Q_W = N_HEADS * HEAD_DIM
KV_W = N_KV * HEAD_DIM
NSA_GATE_W = 3 * N_HEADS
IN_SIZES = (Q_W, KV_W, KV_W, KV_W, KV_W, KV_W, KV_W, NSA_GATE_W, SGU_WIDTH, SGU_WIDTH, D_MODEL, D_MODEL)

LANES = 128
SUBLANES = 8
QROWS = HEADS_PER_KV * Q_BLOCK
SEL_SHIFT = SEL_BLOCK.bit_length() - 1
BLOCKS_PER_Q = Q_BLOCK // SEL_BLOCK
SLC_BLOCKS = 10
SLC_CHUNK = SLC_BLOCKS * SEL_BLOCK
WIN_KEYS = WINDOW + Q_BLOCK
PROJ_TN = 512
N16_TILES = 4
FFN_TN = 512
VMEM_LIMIT = 56 * 1024 * 1024


def _cparams(sem):
    return pltpu.CompilerParams(dimension_semantics=sem, vmem_limit_bytes=VMEM_LIMIT)


def _dot(a, b):
    return jnp.dot(a, b, preferred_element_type=F32)


def _dot_nt(a, b):
    return lax.dot_general(a, b, (((1,), (1,)), ((), ())), preferred_element_type=F32)


def _rms(x, g):
    return x * lax.rsqrt(jnp.mean(x * x, -1, keepdims=True) + NORM_EPS) * g


TAIL_TN = 2 * PROJ_TN
N_HEAD_TILES = N16_TILES + 1


def _proj_tail_kernel(x_ref, g_ref, w_ref, o_ref, h_ref):
    @pl.when(pl.program_id(1) == 0)
    def _():
        h_ref[...] = _rms(x_ref[...], g_ref[...]).astype(BF16)

    o_ref[...] = _dot(h_ref[...], w_ref[...]).astype(BF16)


def _proj_head_kernel(h_ref, w_ref, wg_ref, ohead_ref, okvc_ref, ong_ref):
    j = pl.program_id(1)

    @pl.when(j == 0)
    def _():
        ong_ref[...] = _dot(h_ref[...], wg_ref[...])

    acc = _dot(h_ref[...], w_ref[...])
    okvc_ref[...] = acc
    ohead_ref[...] = (acc * jnp.where(j <= 1, HEAD_DIM ** -0.5, 1.0)).astype(BF16)


def _in_proj(x, g, w_head, w_tail, wg):
    t = x.shape[0]
    tm = min(1024, t)
    otail, h = pl.pallas_call(
        _proj_tail_kernel,
        grid=(t // tm, w_tail.shape[1] // TAIL_TN),
        in_specs=[
            pl.BlockSpec((tm, D_MODEL), lambda i, j: (i, 0)),
            pl.BlockSpec((1, D_MODEL), lambda i, j: (0, 0)),
            pl.BlockSpec((D_MODEL, TAIL_TN), lambda i, j: (0, j)),
        ],
        out_specs=[
            pl.BlockSpec((tm, TAIL_TN), lambda i, j: (i, j)),
            pl.BlockSpec((tm, D_MODEL), lambda i, j: (i, 0)),
        ],
        out_shape=[jax.ShapeDtypeStruct((t, w_tail.shape[1]), BF16), jax.ShapeDtypeStruct((t, D_MODEL), BF16)],
        compiler_params=_cparams(("arbitrary", "arbitrary")),
        name="proj_tail",
    )(x, g, w_tail)
    head_tile = lambda j: jnp.where(j <= 1, j, jnp.where(j <= 3, j + 1, 2))
    ohead, okvc, ng = pl.pallas_call(
        _proj_head_kernel,
        grid=(t // tm, N_HEAD_TILES),
        in_specs=[
            pl.BlockSpec((tm, D_MODEL), lambda i, j: (i, 0)),
            pl.BlockSpec((D_MODEL, PROJ_TN), lambda i, j: (0, head_tile(j))),
            pl.BlockSpec((D_MODEL, 2 * LANES), lambda i, j: (0, 0)),
        ],
        out_specs=[
            pl.BlockSpec((tm, PROJ_TN), lambda i, j: (i, j)),
            pl.BlockSpec((tm, PROJ_TN), lambda i, j: (i, 0)),
            pl.BlockSpec((tm, 2 * LANES), lambda i, j: (i, 0)),
        ],
        out_shape=[
            jax.ShapeDtypeStruct((t, N_HEAD_TILES * PROJ_TN), BF16),
            jax.ShapeDtypeStruct((t, PROJ_TN), F32),
            jax.ShapeDtypeStruct((t, 2 * LANES), F32),
        ],
        compiler_params=_cparams(("arbitrary", "arbitrary")),
        name="proj_head",
    )(h, w_head, wg)
    return ohead, otail, okvc, ng


def _compress_kernel(kv_ref, pe_ref, w1_ref, w2_ref, o_ref):
    nu = o_ref.shape[0]
    a = jnp.zeros((nu, HEAD_DIM), F32)
    b = jnp.zeros((nu, HEAD_DIM), F32)
    for r in range(CMP_STRIDE):
        x = kv_ref[pl.ds(r, nu, stride=CMP_STRIDE), :]
        a += _dot((x + pe_ref[pl.ds(r, 1), :]).astype(BF16), w1_ref[r])
        b += _dot((x + pe_ref[pl.ds(CMP_STRIDE + r, 1), :]).astype(BF16), w1_ref[CMP_STRIDE + r])
    hid = a + pltpu.roll(b, nu - 1, 0)
    o_ref[...] = _dot(jax.nn.gelu(hid).astype(BF16), w2_ref[...]).astype(BF16)


def _compress(o32, col0, pe, w1, w2, t):
    nu = t // CMP_STRIDE
    return pl.pallas_call(
        _compress_kernel,
        grid=(2, N_KV),
        in_specs=[
            pl.BlockSpec((t, HEAD_DIM), lambda a, g: (0, col0 + N_KV * a + g)),
            pl.BlockSpec((None, CMP_BLOCK, HEAD_DIM), lambda a, g: (a, 0, 0)),
            pl.BlockSpec((None, CMP_BLOCK, HEAD_DIM, HEAD_DIM), lambda a, g: (a, 0, 0, 0)),
            pl.BlockSpec((None, HEAD_DIM, HEAD_DIM), lambda a, g: (a, 0, 0)),
        ],
        out_specs=pl.BlockSpec((None, None, nu, HEAD_DIM), lambda a, g: (a, g, 0, 0)),
        out_shape=jax.ShapeDtypeStruct((2, N_KV, nu, HEAD_DIM), BF16),
        compiler_params=_cparams(("arbitrary", "arbitrary")),
        name="compress",
    )(o32, pe, w1, w2)


def _lane_tile(x, n):
    return jnp.concatenate([x] * n, axis=1)


def _row_tile(x, n):
    return jnp.concatenate([x] * n, axis=0)


def _softmax_numerators(s):
    return jnp.exp(s - jnp.max(s, axis=1, keepdims=True)).astype(BF16)


def _stack_heads(q, g):
    return jnp.concatenate([q[:, (g * HEADS_PER_KV + h) * HEAD_DIM:(g * HEADS_PER_KV + h + 1) * HEAD_DIM]
                            for h in range(HEADS_PER_KV)], axis=0)


def _nsa_select_kernel(q_ref, kvc_ref, bc_ref, sj_ref, slr_ref, ov_ref, lt_ref, ocmp_ref, aug_ref, lists_ref):
    i = pl.program_id(0)
    t0 = i * Q_BLOCK
    t0f = t0.astype(F32)
    q = q_ref[...]
    nc = kvc_ref.shape[2]
    row = lax.broadcasted_iota(jnp.int32, (QROWS, 1), 0) & (Q_BLOCK - 1)
    jr = lax.broadcasted_iota(jnp.int32, (LANES, Q_BLOCK), 0)
    kl = lax.broadcasted_iota(jnp.int32, (LANES, Q_BLOCK), 1)
    tq = t0 + kl
    cur = lax.shift_right_logical(tq, SEL_SHIFT)
    forced = (jr == 0) | (jr == cur) | (jr == cur - 1)
    visible = jr * SEL_BLOCK <= tq
    jf = jr.astype(F32)

    ocmp, imps = [], []
    for g in range(N_KV):
        bc = bc_ref[g]
        s = jnp.where(bc <= _lane_tile(slr_ref[g] * t0f, nc // LANES),
                      _dot_nt(_stack_heads(q, g), kvc_ref[0, g]) + bc, NEG_INF)
        r = _dot(_softmax_numerators(s), jnp.concatenate([kvc_ref[1, g], ov_ref[...]], axis=1))
        l = jnp.sum(r[:, HEAD_DIM:], axis=1, keepdims=True) * (1.0 / CMP_BLOCK)
        inv = jnp.where(t0 + row >= CMP_BLOCK - 1, 1.0 / l, 0.0)
        o = r[:, :HEAD_DIM] * inv
        ocmp += [o[h * Q_BLOCK:(h + 1) * Q_BLOCK] for h in range(HEADS_PER_KV)]
        imp = r[:, HEAD_DIM:] * inv
        imps.append(sum(imp[h * Q_BLOCK:(h + 1) * Q_BLOCK] for h in range(HEADS_PER_KV)).T)
    ocmp_ref[...] = jnp.concatenate(ocmp, axis=1)

    forced2, visible2, jf2 = (_lane_tile(a, N_KV) for a in (forced, visible, jf))
    score = jnp.where(forced2 | jnp.logical_not(visible2), NEG_INF, jnp.concatenate(imps, axis=1))
    for _ in range(N_SELECT - N_FORCED):
        mx = jnp.max(score, axis=0, keepdims=True)
        idx = jnp.min(jnp.where(score == mx, jf2, float(LANES)), axis=0, keepdims=True)
        score = jnp.where(jf2 == idx, -jnp.inf, score)
    sel_all = jnp.where(visible2 & (forced2 | (score == -jnp.inf)), 1.0, 0.0)

    rows = []
    for g in range(N_KV):
        sel_t = sel_all[:, g * Q_BLOCK:(g + 1) * Q_BLOCK]
        aug_ref[g] = jnp.where(_row_tile(sel_t.T, HEADS_PER_KV) > 0.5, sj_ref[g], NEG_INF).astype(BF16)

        used = jnp.max(jnp.where(jr < BLOCKS_PER_Q * i, sel_t, 0.0), axis=1, keepdims=True)
        used = jnp.broadcast_to(used, (LANES, Q_BLOCK))
        slot = _dot(lt_ref[...], used.astype(BF16))
        lst = jnp.sum(jnp.where((slot == kl.astype(F32)) & (used > 0.5), jf, 0.0), axis=0, keepdims=True)
        cnt = jnp.sum(used, axis=0, keepdims=True)
        lst = jnp.where(kl[:1].astype(F32) < cnt, lst, (BLOCKS_PER_Q * i).astype(F32))
        rows += [lst, cnt]
    pad =[jnp.zeros((1, LANES), F32)] * (SUBLANES - len(rows))
    lists_ref[...] = jnp.concatenate(rows[0::2] + rows[1::2] + pad, axis=0).astype(jnp.int32)


def _nsa_attend_kernel(lists_ref, counts_ref, q_ref, ks_ref, vs_ref, kw_ref, vw_ref, aug_ref, ocmp_ref, ng_ref,
                       bw_ref, b1_ref, b1d_ref, et_ref, o_ref):
    i = pl.program_id(0)
    nqb = pl.num_programs(0)
    t0 = i * Q_BLOCK
    q = q_ref[...]
    q4 = [_stack_heads(q, g) for g in range(N_KV)]
    ones_col = jnp.ones((WIN_KEYS, LANES), BF16)
    gcols = lambda g: slice(g * HEAD_DIM, (g + 1) * HEAD_DIM)
    grows = lambda a, g: a[g * QROWS:(g + 1) * QROWS]
    both = lambda f: jnp.concatenate([f(g) for g in range(N_KV)], axis=0)

    w0 = pl.multiple_of(jnp.maximum(t0 - WINDOW, 0), Q_BLOCK)
    shift = pl.multiple_of(w0 - (t0 - WINDOW), Q_BLOCK)
    s = both(lambda g: _dot_nt(q4[g], kw_ref[pl.ds(w0, WIN_KEYS), gcols(g)])) + bw_ref[:, pl.ds(shift, WIN_KEYS)]
    e = _softmax_numerators(s)
    r = both(lambda g: _dot(grows(e, g), jnp.concatenate([vw_ref[pl.ds(w0, WIN_KEYS), gcols(g)], ones_col], axis=1)))
    o_win = r[:, :HEAD_DIM] * (1.0 / r[:, HEAD_DIM:])

    gl = lax.broadcasted_iota(jnp.int32, (Q_BLOCK, LANES), 1)
    o_cmp = ocmp_ref[...]
    slc_gate, partial = [], []
    for g in range(N_KV):
        sg = jax.nn.sigmoid(ng_ref[:, gcols(g)])
        gate = lambda col: jnp.sum(jnp.where(gl == col, sg, 0.0), axis=1, keepdims=True)
        for h in range(HEADS_PER_KV):
            hh = g * HEADS_PER_KV + h
            slc_gate.append(gate(3 * h + 1))
            partial.append(gate(3 * h) * o_cmp[:, hh * HEAD_DIM:(hh + 1) * HEAD_DIM]
                           + gate(3 * h + 2) * o_win[hh * Q_BLOCK:(hh + 1) * Q_BLOCK])

    jl = lax.broadcasted_iota(jnp.int32, (QROWS, LANES), 1)
    qx = [jnp.concatenate([q4[g], aug_ref[g]], axis=1) for g in range(N_KV)]
    qx_past = [jnp.concatenate([q4[g], jnp.where(jl < BLOCKS_PER_Q * i, aug_ref[g], NEG_INF).astype(BF16)], axis=1)
               for g in range(N_KV)]
    b1 = _lane_tile(b1_ref[...], SLC_CHUNK // LANES)

    def gathered(ref, g, first, cols):
        parts = []
        for k in range(SLC_BLOCKS):
            slot = jnp.minimum(first + k, LANES - 1)
            k0 = pl.multiple_of(lists_ref[g * nqb + i, slot] * SEL_BLOCK, SEL_BLOCK)
            parts.append(ref[pl.ds(k0, SEL_BLOCK), cols])
        return jnp.concatenate(parts, axis=0)

    def slc_step(c, carry):
        m, acc = carry
        first = c * SLC_BLOCKS
        u = both(lambda g: _dot_nt(qx_past[g], jnp.concatenate(
            [gathered(ks_ref, g, first, gcols(g)), gathered(et_ref, g, first, slice(None))], axis=1))) + b1
        m_new = jnp.maximum(m, jnp.max(u, axis=1, keepdims=True))
        e = jnp.exp(u - m_new).astype(BF16)
        pv = both(lambda g: _dot(grows(e, g), jnp.concatenate(
            [gathered(vs_ref, g, first, gcols(g)), ones_col[:SLC_CHUNK]], axis=1)))
        return m_new, jnp.exp(m - m_new) * acc + pv

    n_blocks = jnp.maximum(counts_ref[i], counts_ref[nqb + i])
    n_trips = (n_blocks + (SLC_BLOCKS - 1)) // SLC_BLOCKS
    m, acc = lax.fori_loop(0, n_trips, slc_step,
                           (jnp.full((N_KV * QROWS, 1), NEG_INF, F32), jnp.zeros((N_KV * QROWS, 2 * HEAD_DIM), F32)))
    u = both(lambda g: _dot_nt(qx[g], jnp.concatenate(
        [ks_ref[pl.ds(t0, Q_BLOCK), gcols(g)], et_ref[pl.ds(t0, Q_BLOCK), :]], axis=1))) + b1d_ref[...]
    m_new = jnp.maximum(m, jnp.max(u, axis=1, keepdims=True))
    e = jnp.exp(u - m_new).astype(BF16)
    pv = both(lambda g: _dot(grows(e, g), jnp.concatenate(
        [vs_ref[pl.ds(t0, Q_BLOCK), gcols(g)], ones_col[:Q_BLOCK]], axis=1)))
    acc = jnp.exp(m - m_new) * acc + pv
    o_slc = acc[:, :HEAD_DIM] * (1.0 / acc[:, HEAD_DIM:])

    outs = [partial[hh] + slc_gate[hh] * o_slc[hh * Q_BLOCK:(hh + 1) * Q_BLOCK] for hh in range(N_HEADS)]
    o_ref[...] = jnp.concatenate(outs, axis=1).astype(BF16)


def _nsa_tables(t):
    nc = t // CMP_STRIDE
    f32 = np.float32
    r = (np.arange(QROWS) % Q_BLOCK)[:, None].astype(f32)
    slopes = np.exp2(-8.0 * np.arange(1, N_HEADS + 1, dtype=f32) / N_HEADS).astype(f32)
    slr = np.repeat(slopes.reshape(N_KV, HEADS_PER_KV), Q_BLOCK, axis=1)[:, :, None]
    bc = slr * ((np.arange(nc) * CMP_STRIDE + CMP_BLOCK - 1)[None, None, :] - r[None])
    dw = r - np.arange(WIN_KEYS)[None, :] + WINDOW
    bw = np.where((dw >= 0) & (dw < WINDOW), -slr * dw[None], f32(NEG_INF))
    bw = np.concatenate([bw, np.full(bw.shape[:2] + (WINDOW,), NEG_INF, f32)], axis=2)
    kk = np.arange(LANES)[None, :]
    b1 = slr * (kk % SEL_BLOCK)[None]
    b1d = np.where(kk <= r, b1, f32(NEG_INF))
    sj = slr * (SEL_BLOCK * kk)[None]
    et = (np.arange(t)[:, None] // SEL_BLOCK == kk)
    cmp_start = np.arange(nc) * CMP_STRIDE
    sel_start = np.arange(LANES) * SEL_BLOCK
    ov = np.clip(np.minimum(cmp_start[:, None] + CMP_BLOCK - 1, sel_start[None] + SEL_BLOCK - 1)
                 - np.maximum(cmp_start[:, None], sel_start[None]) + 1, 0, None)
    ov[nc - 1] = 0
    lt = np.tril(np.ones((LANES, LANES)), -1)
    slr = slr * np.ones((1, 1, LANES), f32)
    as_f32 = lambda a: jnp.asarray(a.astype(f32))
    as_bf16 = lambda a: jnp.asarray(a.astype(f32), BF16)
    return dict(bc=as_f32(bc), bw=as_f32(bw), b1=as_f32(b1), b1d=as_f32(b1d), sj=as_f32(sj), slr=as_f32(slr),
                et=as_bf16(et), ov=as_bf16(ov), lt=as_bf16(lt))


def _nsa(o16, kvc, ng, t):
    tb = _nsa_tables(t)
    nc = t // CMP_STRIDE
    nqb = t // Q_BLOCK
    whole = lambda a: pl.BlockSpec(a.shape, lambda i: (0,) * a.ndim)
    ocmp, aug, lists = pl.pallas_call(
        _nsa_select_kernel,
        grid=(nqb,),
        in_specs=[
            pl.BlockSpec((Q_BLOCK, Q_W), lambda i: (i, 0)),
            whole(kvc), whole(tb["bc"]), whole(tb["sj"]), whole(tb["slr"]), whole(tb["ov"]), whole(tb["lt"]),
        ],
        out_specs=[
            pl.BlockSpec((Q_BLOCK, Q_W), lambda i: (i, 0)),
            pl.BlockSpec((N_KV, QROWS, LANES), lambda i: (0, i, 0)),
            pl.BlockSpec((None, SUBLANES, LANES), lambda i: (i, 0, 0)),
        ],
        out_shape=[
            jax.ShapeDtypeStruct((t, Q_W), F32),
            jax.ShapeDtypeStruct((N_KV, nqb * QROWS, LANES), BF16),
            jax.ShapeDtypeStruct((nqb, SUBLANES, LANES), jnp.int32),
        ],
        compiler_params=_cparams(("arbitrary",)),
        name="nsa_select",
    )(o16, kvc, tb["bc"], tb["sj"], tb["slr"], tb["ov"], tb["lt"])
    block_lists = lists[:, :N_KV, :].transpose(1, 0, 2).reshape(N_KV * nqb, LANES)
    counts = lists[:, N_KV:2 * N_KV, 0].T.reshape(N_KV * nqb)

    stacked = lambda a: a.reshape((N_KV * QROWS,) + a.shape[2:])
    bw, b1, b1d = stacked(tb["bw"]), stacked(tb["b1"]), stacked(tb["b1d"])
    once = pl.Buffered(1)
    const = lambda a: pl.BlockSpec(a.shape, lambda i, *_: (0,) * a.ndim, pipeline_mode=once)
    grid_spec = pltpu.PrefetchScalarGridSpec(
        num_scalar_prefetch=2,
        grid=(nqb,),
        in_specs=[
            pl.BlockSpec((Q_BLOCK, Q_W), lambda i, *_: (i, 0)),
            pl.BlockSpec((t, KV_W), lambda i, *_: (0, Q_W // KV_W), pipeline_mode=once),
            pl.BlockSpec((t, KV_W), lambda i, *_: (0, Q_W // KV_W + 1), pipeline_mode=once),
            pl.BlockSpec((t, KV_W), lambda i, *_: (0, Q_W // KV_W + 2), pipeline_mode=once),
            pl.BlockSpec((t, KV_W), lambda i, *_: (0, Q_W // KV_W + 3), pipeline_mode=once),
            pl.BlockSpec((N_KV, QROWS, LANES), lambda i, *_: (0, i, 0)),
            pl.BlockSpec((Q_BLOCK, Q_W), lambda i, *_: (i, 0)),
            pl.BlockSpec((Q_BLOCK, N_KV * LANES), lambda i, *_: (i, 0)),
            const(bw), const(b1), const(b1d), const(tb["et"]),
        ],
        out_specs=pl.BlockSpec((Q_BLOCK, Q_W), lambda i, *_: (i, 0)),
    )
    return pl.pallas_call(
        _nsa_attend_kernel,
        grid_spec=grid_spec,
        out_shape=jax.ShapeDtypeStruct((t, Q_W), BF16),
        compiler_params=_cparams(("arbitrary",)),
        name="nsa_attend",
    )(block_lists, counts, o16, o16, o16, o16, o16, aug, ocmp, ng, bw, b1, b1d, tb["et"])


SGU_CHUNKS_PER_STEP = 4


def _sgu_kernel(u_ref, v_ref, lng_ref, lnb_ref, ws_ref, bs_ref, o_ref):
    gu = jax.nn.gelu(u_ref[...].astype(F32))
    gv = jax.nn.gelu(v_ref[...].astype(F32))
    xc = gv - jnp.mean(gv, -1, keepdims=True)
    vn = (xc * lax.rsqrt(jnp.mean(xc * xc, -1, keepdims=True) + LN_EPS) * lng_ref[...] + lnb_ref[...]).astype(BF16)
    causal = (lax.broadcasted_iota(jnp.int32, (SGU_CHUNK, SGU_CHUNK), 0)
              >= lax.broadcasted_iota(jnp.int32, (SGU_CHUNK, SGU_CHUNK), 1))
    w = [jnp.where(causal, ws_ref[g], 0.0).astype(BF16) for g in range(SGU_GROUPS)]
    rows = []
    for c in range(SGU_CHUNKS_PER_STEP):
        vc = vn[c * SGU_CHUNK:(c + 1) * SGU_CHUNK]
        rows.append(jnp.concatenate(
            [_dot(w[g], vc[:, g * SGU_GROUP_DIM:(g + 1) * SGU_GROUP_DIM]) + bs_ref[g] for g in range(SGU_GROUPS)],
            axis=1))
    o_ref[...] = (gu * jnp.concatenate(rows, axis=0)).astype(BF16)


def _sgu(otail, lng, lnb, ws, bs, t):
    tm = SGU_CHUNKS_PER_STEP * SGU_CHUNK
    return pl.pallas_call(
        _sgu_kernel,
        grid=(t // tm,),
        in_specs=[
            pl.BlockSpec((tm, SGU_WIDTH), lambda i: (i, 0)),
            pl.BlockSpec((tm, SGU_WIDTH), lambda i: (i, 1)),
            pl.BlockSpec((1, SGU_WIDTH), lambda i: (0, 0)),
            pl.BlockSpec((1, SGU_WIDTH), lambda i: (0, 0)),
            pl.BlockSpec((SGU_GROUPS, SGU_CHUNK, SGU_CHUNK), lambda i: (0, 0, 0)),
            pl.BlockSpec((SGU_GROUPS, SGU_CHUNK, LANES), lambda i: (0, 0, 0)),
        ],
        out_specs=pl.BlockSpec((tm, SGU_WIDTH), lambda i: (i, 0)),
        out_shape=jax.ShapeDtypeStruct((t, SGU_WIDTH), BF16),
        compiler_params=_cparams(("arbitrary",)),
        name="sgu",
    )(otail, otail, lng, lnb, ws, bs)


def _mix_kernel(ya_ref, yb_ref, ga_ref, gb_ref, x_ref, pa_ref, pb_ref, wo_ref, g2_ref, x1_ref, h2_ref):
    mixed = (jax.nn.sigmoid(ga_ref[...].astype(F32)) * _dot(ya_ref[...], pa_ref[...])
             + jax.nn.sigmoid(gb_ref[...].astype(F32)) * _dot(yb_ref[...], pb_ref[...]))
    x1 = x_ref[...] + _dot(mixed.astype(BF16), wo_ref[...])
    x1_ref[...] = x1
    h2_ref[...] = _rms(x1, g2_ref[...]).astype(BF16)


def _mix(ya, yb, o32, x, pa, pb, wo, g2, t):
    tm = min(512, t)
    once = pl.Buffered(1)
    return pl.pallas_call(
        _mix_kernel,
        grid=(t // tm,),
        in_specs=[
            pl.BlockSpec((tm, Q_W), lambda i: (i, 0)),
            pl.BlockSpec((tm, SGU_WIDTH), lambda i: (i, 0)),
            pl.BlockSpec((tm, D_MODEL), lambda i: (i, 1)),
            pl.BlockSpec((tm, D_MODEL), lambda i: (i, 2)),
            pl.BlockSpec((tm, D_MODEL), lambda i: (i, 0)),
            pl.BlockSpec((Q_W, D_MODEL), lambda i: (0, 0), pipeline_mode=once),
            pl.BlockSpec((SGU_WIDTH, D_MODEL), lambda i: (0, 0), pipeline_mode=once),
            pl.BlockSpec((D_MODEL, D_MODEL), lambda i: (0, 0), pipeline_mode=once),
            pl.BlockSpec((1, D_MODEL), lambda i: (0, 0)),
        ],
        out_specs=[
            pl.BlockSpec((tm, D_MODEL), lambda i: (i, 0)),
            pl.BlockSpec((tm, D_MODEL), lambda i: (i, 0)),
        ],
        out_shape=[jax.ShapeDtypeStruct((t, D_MODEL), F32), jax.ShapeDtypeStruct((t, D_MODEL), BF16)],
        compiler_params=_cparams(("arbitrary",)),
        name="mix_out",
    )(ya, yb, o32, o32, x, pa, pb, wo, g2)


FFN_TILES = pl.cdiv(D_FF, FFN_TN)
FFN_BACK = FFN_TILES * FFN_TN - D_FF
FFN_MAIN = (FFN_TILES - 1) * FFN_TN


def _ffn_col(j, base=0):
    return (base // LANES + j * (FFN_TN // LANES) - (j // (FFN_TILES - 1)) * (FFN_BACK // LANES)) * LANES


def _ffn_in_kernel(h_ref, wa_ref, wb_ref, cw_ref, cb_ref, o_ref, wa_scr, wb_scr, a_scr):
    tm = h_ref.shape[0]

    @pl.when(pl.program_id(1) == 0)
    def _():
        wa_scr[...] = wa_ref[...].astype(BF16)
        wb_scr[...] = wb_ref[...].astype(BF16)
        a_scr[0:SUBLANES, :] = jnp.zeros((SUBLANES, FFN_TN), F32)

    h = h_ref[...]
    a = _dot(h, wa_scr[...])
    b = _dot(h, wb_scr[...])
    a_scr[SUBLANES:SUBLANES + tm, :] = a
    cw = cw_ref[...]
    y = (cw[0:1] * a_scr[pl.ds(SUBLANES - 2, tm), :] + cw[1:2] * a_scr[pl.ds(SUBLANES - 1, tm), :]
         + cw[2:3] * a + cb_ref[...])
    o_ref[...] = (jax.nn.gelu(y) * b).astype(BF16)
    a_scr[0:SUBLANES, :] = a[tm - SUBLANES:tm, :]


def _ffn_in(h2, w, cw, cb, t):
    tm = min(1024, t)
    tile = lambda rows, base: pl.BlockSpec((pl.Element(rows), pl.Element(FFN_TN)),
                                           lambda j, i: (0, _ffn_col(j, base)))
    return pl.pallas_call(
        _ffn_in_kernel,
        grid=(FFN_TILES, t // tm),
        in_specs=[
            pl.BlockSpec((tm, D_MODEL), lambda j, i: (i, 0)),
            tile(D_MODEL, 0),
            tile(D_MODEL, D_FF),
            tile(SUBLANES, 0),
            tile(1, 0),
        ],
        out_specs=pl.BlockSpec((tm, FFN_TN), lambda j, i: (i, j)),
        out_shape=jax.ShapeDtypeStruct((t, FFN_TILES * FFN_TN), BF16),
        scratch_shapes=[pltpu.VMEM((D_MODEL, FFN_TN), BF16), pltpu.VMEM((D_MODEL, FFN_TN), BF16),
                        pltpu.VMEM((tm + SUBLANES, FFN_TN), F32)],
        compiler_params=_cparams(("arbitrary", "arbitrary")),
        name="ffn_in",
    )(h2, w, w, cw, cb)


def _ffn_out_kernel(a0_ref, a1_ref, w0_ref, w1_ref, x_ref, g_ref, o_ref, *, final_norm):
    x2 = x_ref[...] + _dot(a0_ref[...], w0_ref[...]) + _dot(a1_ref[...], w1_ref[...])
    o_ref[...] = _rms(x2, g_ref[...]) if final_norm else x2


def _ffn_out(act, wd, x1, g, t, final_norm):
    tm = min(512, t)
    once = pl.Buffered(1)
    rest = D_FF - FFN_MAIN
    return pl.pallas_call(
        functools.partial(_ffn_out_kernel, final_norm=final_norm),
        grid=(t // tm,),
        in_specs=[
            pl.BlockSpec((pl.Element(tm), pl.Element(FFN_MAIN)), lambda i: (i * tm, 0)),
            pl.BlockSpec((pl.Element(tm), pl.Element(rest)), lambda i: (i * tm, FFN_MAIN + FFN_BACK)),
            pl.BlockSpec((pl.Element(FFN_MAIN), pl.Element(D_MODEL)), lambda i: (0, 0), pipeline_mode=once),
            pl.BlockSpec((pl.Element(rest), pl.Element(D_MODEL)), lambda i: (FFN_MAIN, 0), pipeline_mode=once),
            pl.BlockSpec((tm, D_MODEL), lambda i: (i, 0)),
            pl.BlockSpec((1, D_MODEL), lambda i: (0, 0)),
        ],
        out_specs=pl.BlockSpec((tm, D_MODEL), lambda i: (i, 0)),
        out_shape=jax.ShapeDtypeStruct((t, D_MODEL), F32),
        compiler_params=_cparams(("arbitrary",)),
        name="ffn_out",
    )(act, act, wd, wd, x1, g)


def _layer(x, p, final_g, t):
    offs = np.cumsum((0,) + IN_SIZES)
    w_in = p["w_in"]
    wg = w_in[:, offs[7]:offs[8]].astype(BF16).reshape(D_MODEL, N_KV, HEADS_PER_KV * 3)
    wg = jnp.pad(wg, ((0, 0), (0, 0), (0, LANES - HEADS_PER_KV * 3))).reshape(D_MODEL, N_KV * LANES)
    o16, otail, okvc, ng = _in_proj(x, p["norm1_g"][None, :], w_in[:, :offs[7]].astype(BF16),
                                    w_in[:, offs[8]:].astype(BF16), wg)

    pe = jnp.stack([p["cmp_k_pe"], p["cmp_v_pe"]])
    w1 = jnp.stack([p["cmp_k_w1"], p["cmp_v_w1"]]).reshape(2, CMP_BLOCK, HEAD_DIM, HEAD_DIM).astype(BF16)
    w2 = jnp.stack([p["cmp_k_w2"], p["cmp_v_w2"]]).astype(BF16)
    kvc = _compress(okvc, 0, pe, w1, w2, t)

    y_a = _nsa(o16, kvc, ng, t)
    bs = jnp.broadcast_to(p["sgu_b"][:, :, None], (SGU_GROUPS, SGU_CHUNK, LANES))
    y_b = _sgu(otail, p["sgu_ln_g"][None, :], p["sgu_ln_b"][None, :], p["sgu_w"], bs, t)
    x1, h2 = _mix(y_a, y_b, otail, x, p["w_branch_a"].astype(BF16), p["w_branch_b"].astype(BF16),
                  p["w_out"].astype(BF16), p["norm2_g"][None, :], t)

    cw = jnp.pad(p["ffn_conv_w"], ((0, SUBLANES - CONV_WIDTH), (0, 0)))
    act = _ffn_in(h2, p["ffn_w_in"], cw, p["ffn_conv_b"][None, :], t)
    wd = p["ffn_w_down"].astype(BF16)
    g = final_g[None, :] if final_g is not None else jnp.ones((1, D_MODEL), F32)
    return _ffn_out(act, wd, x1, g, t, final_g is not None)


def kernel(x, norm1_g, w_in, cmp_k_pe, cmp_k_w1, cmp_k_w2, cmp_v_pe, cmp_v_w1, cmp_v_w2, sgu_ln_g, sgu_ln_b, sgu_w, sgu_b, w_branch_a, w_branch_b, w_out, norm2_g, ffn_w_in, ffn_conv_w, ffn_conv_b, ffn_w_down, final_g):
    b, t, _ = x.shape
    assert b == 1 and t % 1024 == 0, "one sequence whose length is a multiple of 1024"
    params = dict(norm1_g=norm1_g, w_in=w_in, cmp_k_pe=cmp_k_pe, cmp_k_w1=cmp_k_w1, cmp_k_w2=cmp_k_w2,
                  cmp_v_pe=cmp_v_pe, cmp_v_w1=cmp_v_w1, cmp_v_w2=cmp_v_w2, sgu_ln_g=sgu_ln_g, sgu_ln_b=sgu_ln_b,
                  sgu_w=sgu_w, sgu_b=sgu_b, w_branch_a=w_branch_a, w_branch_b=w_branch_b, w_out=w_out,
                  norm2_g=norm2_g, ffn_w_in=ffn_w_in, ffn_conv_w=ffn_conv_w, ffn_conv_b=ffn_conv_b,
                  ffn_w_down=ffn_w_down)
    depth = norm1_g.shape[0]
    h = x[0]
    for l in range(depth):
        layer = {k: v[l] for k, v in params.items()}
        h = _layer(h, layer, final_g if l == depth - 1 else None, t)
    return h[None]
```

```python
import functools

import numpy as np
import jax
import jax.numpy as jnp
from jax import lax
from jax.experimental import pallas as pl
from jax.experimental.pallas import tpu as pltpu

F32 = jnp.float32
BF16 = jnp.bfloat16

D_MODEL = 2048
N_HEADS = 8
HEAD_DIM = 128
N_KV = 2
HEADS_PER_KV = N_HEADS // N_KV
CMP_BLOCK = 32
CMP_STRIDE = 16
SEL_BLOCK = 64
N_SELECT = 16
N_FORCED = 3
WINDOW = 512
Q_BLOCK = 128
SGU_WIDTH = 1024
SGU_GROUPS = 8
SGU_GROUP_DIM = SGU_WIDTH // SGU_GROUPS
SGU_CHUNK = 128
D_FF = 5504
CONV_WIDTH = 3
NORM_EPS = 1e-6
LN_EPS = 1e-5
NEG_INF = -1e30

Q_W = N_HEADS * HEAD_DIM
KV_W = N_KV * HEAD_DIM
NSA_GATE_W = 3 * N_HEADS
IN_SIZES = (Q_W, KV_W, KV_W, KV_W, KV_W, KV_W, KV_W, NSA_GATE_W, SGU_WIDTH, SGU_WIDTH, D_MODEL, D_MODEL)

LANES = 128
SUBLANES = 8
QROWS = HEADS_PER_KV * Q_BLOCK
SEL_SHIFT = SEL_BLOCK.bit_length() - 1
BLOCKS_PER_Q = Q_BLOCK // SEL_BLOCK
SLC_BLOCKS = 10
SLC_CHUNK = SLC_BLOCKS * SEL_BLOCK
WIN_KEYS = WINDOW + Q_BLOCK
PROJ_TN = 512
N16_TILES = 4
FFN_TN = 512
VMEM_LIMIT = 56 * 1024 * 1024


def _cparams(sem):
    return pltpu.CompilerParams(dimension_semantics=sem, vmem_limit_bytes=VMEM_LIMIT)


def _dot(a, b):
    return jnp.dot(a, b, preferred_element_type=F32)


def _dot_nt(a, b):
    return lax.dot_general(a, b, (((1,), (1,)), ((), ())), preferred_element_type=F32)


def _rms(x, g):
    return x * lax.rsqrt(jnp.mean(x * x, -1, keepdims=True) + NORM_EPS) * g


TAIL_TN = 2 * PROJ_TN
N_HEAD_TILES = N16_TILES + 1


def _proj_tail_kernel(x_ref, g_ref, w_ref, o_ref, h_ref):
    @pl.when(pl.program_id(1) == 0)
    def _():
        h_ref[...] = _rms(x_ref[...], g_ref[...]).astype(BF16)

    o_ref[...] = _dot(h_ref[...], w_ref[...]).astype(BF16)


def _proj_head_kernel(h_ref, w_ref, wg_ref, ohead_ref, okvc_ref, ong_ref):
    j = pl.program_id(1)

    @pl.when(j == 0)
    def _():
        ong_ref[...] = _dot(h_ref[...], wg_ref[...])

    acc = _dot(h_ref[...], w_ref[...])
    okvc_ref[...] = acc
    ohead_ref[...] = (acc * jnp.where(j <= 1, HEAD_DIM ** -0.5, 1.0)).astype(BF16)


def _in_proj(x, g, w_head, w_tail, wg):
    t = x.shape[0]
    tm = min(1024, t)
    otail, h = pl.pallas_call(
        _proj_tail_kernel,
        grid=(t // tm, w_tail.shape[0]),
        in_specs=[
            pl.BlockSpec((tm, D_MODEL), lambda i, j: (i, 0)),
            pl.BlockSpec((1, D_MODEL), lambda i, j: (0, 0)),
            pl.BlockSpec((None, D_MODEL, TAIL_TN), lambda i, j: (j, 0, 0)),
        ],
        out_specs=[
            pl.BlockSpec((None, tm, TAIL_TN), lambda i, j: (j, i, 0)),
            pl.BlockSpec((tm, D_MODEL), lambda i, j: (i, 0)),
        ],
        out_shape=[jax.ShapeDtypeStruct((w_tail.shape[0], t, TAIL_TN), BF16), jax.ShapeDtypeStruct((t, D_MODEL), BF16)],
        compiler_params=_cparams(("arbitrary", "arbitrary")),
        name="proj_tail",
    )(x, g, w_tail)
    head_tile = lambda j: jnp.where(j <= 1, j, jnp.where(j <= 3, j + 1, 2))
    ohead, okvc, ng = pl.pallas_call(
        _proj_head_kernel,
        grid=(t // tm, N_HEAD_TILES),
        in_specs=[
            pl.BlockSpec((tm, D_MODEL), lambda i, j: (i, 0)),
            pl.BlockSpec((None, D_MODEL, PROJ_TN), lambda i, j: (head_tile(j), 0, 0)),
            pl.BlockSpec((D_MODEL, 2 * LANES), lambda i, j: (0, 0)),
        ],
        out_specs=[
            pl.BlockSpec((None, tm, PROJ_TN), lambda i, j: (j, i, 0)),
            pl.BlockSpec((tm, PROJ_TN), lambda i, j: (i, 0)),
            pl.BlockSpec((tm, 2 * LANES), lambda i, j: (i, 0)),
        ],
        out_shape=[
            jax.ShapeDtypeStruct((N_HEAD_TILES, t, PROJ_TN), BF16),
            jax.ShapeDtypeStruct((t, PROJ_TN), F32),
            jax.ShapeDtypeStruct((t, 2 * LANES), F32),
        ],
        compiler_params=_cparams(("arbitrary", "arbitrary")),
        name="proj_head",
    )(h, w_head, wg)
    return ohead, otail, okvc, ng


def _compress_kernel(kv_ref, pe_ref, w1_ref, w2_ref, o_ref):
    nu = o_ref.shape[0]
    a = jnp.zeros((nu, HEAD_DIM), F32)
    b = jnp.zeros((nu, HEAD_DIM), F32)
    for r in range(CMP_STRIDE):
        x = kv_ref[pl.ds(r, nu, stride=CMP_STRIDE), :]
        a += _dot((x + pe_ref[pl.ds(r, 1), :]).astype(BF16), w1_ref[r])
        b += _dot((x + pe_ref[pl.ds(CMP_STRIDE + r, 1), :]).astype(BF16), w1_ref[CMP_STRIDE + r])
    hid = a + pltpu.roll(b, nu - 1, 0)
    o_ref[...] = _dot(jax.nn.gelu(hid).astype(BF16), w2_ref[...]).astype(BF16)


def _compress(o32, col0, pe, w1, w2, t):
    nu = t // CMP_STRIDE
    return pl.pallas_call(
        _compress_kernel,
        grid=(2, N_KV),
        in_specs=[
            pl.BlockSpec((t, HEAD_DIM), lambda a, g: (0, col0 + N_KV * a + g)),
            pl.BlockSpec((None, CMP_BLOCK, HEAD_DIM), lambda a, g: (a, 0, 0)),
            pl.BlockSpec((None, CMP_BLOCK, HEAD_DIM, HEAD_DIM), lambda a, g: (a, 0, 0, 0)),
            pl.BlockSpec((None, HEAD_DIM, HEAD_DIM), lambda a, g: (a, 0, 0)),
        ],
        out_specs=pl.BlockSpec((None, None, nu, HEAD_DIM), lambda a, g: (a, g, 0, 0)),
        out_shape=jax.ShapeDtypeStruct((2, N_KV, nu, HEAD_DIM), BF16),
        compiler_params=_cparams(("arbitrary", "arbitrary")),
        name="compress",
    )(o32, pe, w1, w2)


def _lane_tile(x, n):
    return jnp.concatenate([x] * n, axis=1)


def _row_tile(x, n):
    return jnp.concatenate([x] * n, axis=0)


def _softmax_numerators(s):
    return jnp.exp(s - jnp.max(s, axis=1, keepdims=True)).astype(BF16)


def _stack_heads(q, g):
    return jnp.concatenate([q[:, (g * HEADS_PER_KV + h) * HEAD_DIM:(g * HEADS_PER_KV + h + 1) * HEAD_DIM]
                            for h in range(HEADS_PER_KV)], axis=0)


def _nsa_select_kernel(q0_ref, q1_ref, kvc_ref, bc_ref, sj_ref, slr_ref, ov_ref, lt_ref, ocmp_ref, aug_ref, lists_ref):
    i = pl.program_id(0)
    t0 = i * Q_BLOCK
    t0f = t0.astype(F32)
    q = jnp.concatenate([q0_ref[...], q1_ref[...]], axis=1)
    nc = kvc_ref.shape[2]
    row = lax.broadcasted_iota(jnp.int32, (QROWS, 1), 0) & (Q_BLOCK - 1)
    jr = lax.broadcasted_iota(jnp.int32, (LANES, Q_BLOCK), 0)
    kl = lax.broadcasted_iota(jnp.int32, (LANES, Q_BLOCK), 1)
    tq = t0 + kl
    cur = lax.shift_right_logical(tq, SEL_SHIFT)
    forced = (jr == 0) | (jr == cur) | (jr == cur - 1)
    visible = jr * SEL_BLOCK <= tq
    jf = jr.astype(F32)

    ocmp, imps = [], []
    for g in range(N_KV):
        bc = bc_ref[g]
        s = jnp.where(bc <= _lane_tile(slr_ref[g] * t0f, nc // LANES),
                      _dot_nt(_stack_heads(q, g), kvc_ref[0, g]) + bc, NEG_INF)
        r = _dot(_softmax_numerators(s), jnp.concatenate([kvc_ref[1, g], ov_ref[...]], axis=1))
        l = jnp.sum(r[:, HEAD_DIM:], axis=1, keepdims=True) * (1.0 / CMP_BLOCK)
        inv = jnp.where(t0 + row >= CMP_BLOCK - 1, 1.0 / l, 0.0)
        o = r[:, :HEAD_DIM] * inv
        ocmp += [o[h * Q_BLOCK:(h + 1) * Q_BLOCK] for h in range(HEADS_PER_KV)]
        imp = r[:, HEAD_DIM:] * inv
        imps.append(sum(imp[h * Q_BLOCK:(h + 1) * Q_BLOCK] for h in range(HEADS_PER_KV)).T)
    ocmp_ref[...] = jnp.concatenate(ocmp, axis=1)

    forced2, visible2, jf2 = (_lane_tile(a, N_KV) for a in (forced, visible, jf))
    score = jnp.where(forced2 | jnp.logical_not(visible2), NEG_INF, jnp.concatenate(imps, axis=1))
    for _ in range(N_SELECT - N_FORCED):
        mx = jnp.max(score, axis=0, keepdims=True)
        idx = jnp.min(jnp.where(score == mx, jf2, float(LANES)), axis=0, keepdims=True)
        score = jnp.where(jf2 == idx, -jnp.inf, score)
    sel_all = jnp.where(visible2 & (forced2 | (score == -jnp.inf)), 1.0, 0.0)

    rows = []
    for g in range(N_KV):
        sel_t = sel_all[:, g * Q_BLOCK:(g + 1) * Q_BLOCK]
        aug_ref[g] = jnp.where(_row_tile(sel_t.T, HEADS_PER_KV) > 0.5, sj_ref[g], NEG_INF).astype(BF16)

        used = jnp.max(jnp.where(jr < BLOCKS_PER_Q * i, sel_t, 0.0), axis=1, keepdims=True)
        used = jnp.broadcast_to(used, (LANES, Q_BLOCK))
        slot = _dot(lt_ref[...], used.astype(BF16))
        lst = jnp.sum(jnp.where((slot == kl.astype(F32)) & (used > 0.5), jf, 0.0), axis=0, keepdims=True)
        cnt = jnp.sum(used, axis=0, keepdims=True)
        lst = jnp.where(kl[:1].astype(F32) < cnt, lst, (BLOCKS_PER_Q * i).astype(F32))
        rows += [lst, cnt]
    pad =[jnp.zeros((1, LANES), F32)] * (SUBLANES - len(rows))
    lists_ref[...] = jnp.concatenate(rows[0::2] + rows[1::2] + pad, axis=0).astype(jnp.int32)


def _nsa_attend_kernel(lists_ref, counts_ref, q0_ref, q1_ref, ksv_ref, kwv_ref, aug_ref, ocmp_ref, ng_ref,
                       bw_ref, b1_ref, b1d_ref, et_ref, o_ref):
    i = pl.program_id(0)
    nqb = pl.num_programs(0)
    t0 = i * Q_BLOCK
    q = jnp.concatenate([q0_ref[...], q1_ref[...]], axis=1)
    q4 = [_stack_heads(q, g) for g in range(N_KV)]
    ones_col = jnp.ones((WIN_KEYS, LANES), BF16)
    gcols = lambda g: slice(g * HEAD_DIM, (g + 1) * HEAD_DIM)
    vcols = lambda g: slice(KV_W + g * HEAD_DIM, KV_W + (g + 1) * HEAD_DIM)
    grows = lambda a, g: a[g * QROWS:(g + 1) * QROWS]
    both = lambda f: jnp.concatenate([f(g) for g in range(N_KV)], axis=0)

    w0 = pl.multiple_of(jnp.maximum(t0 - WINDOW, 0), Q_BLOCK)
    shift = pl.multiple_of(w0 - (t0 - WINDOW), Q_BLOCK)
    s = both(lambda g: _dot_nt(q4[g], kwv_ref[pl.ds(w0, WIN_KEYS), gcols(g)])) + bw_ref[:, pl.ds(shift, WIN_KEYS)]
    e = _softmax_numerators(s)
    r = both(lambda g: _dot(grows(e, g), jnp.concatenate([kwv_ref[pl.ds(w0, WIN_KEYS), vcols(g)], ones_col], axis=1)))
    o_win = r[:, :HEAD_DIM] * (1.0 / r[:, HEAD_DIM:])

    gl = lax.broadcasted_iota(jnp.int32, (Q_BLOCK, LANES), 1)
    o_cmp = ocmp_ref[...]
    slc_gate, partial = [], []
    for g in range(N_KV):
        sg = jax.nn.sigmoid(ng_ref[:, gcols(g)])
        gate = lambda col: jnp.sum(jnp.where(gl == col, sg, 0.0), axis=1, keepdims=True)
        for h in range(HEADS_PER_KV):
            hh = g * HEADS_PER_KV + h
            slc_gate.append(gate(3 * h + 1))
            partial.append(gate(3 * h) * o_cmp[:, hh * HEAD_DIM:(hh + 1) * HEAD_DIM]
                           + gate(3 * h + 2) * o_win[hh * Q_BLOCK:(hh + 1) * Q_BLOCK])

    jl = lax.broadcasted_iota(jnp.int32, (QROWS, LANES), 1)
    qx = [jnp.concatenate([q4[g], aug_ref[g]], axis=1) for g in range(N_KV)]
    qx_past = [jnp.concatenate([q4[g], jnp.where(jl < BLOCKS_PER_Q * i, aug_ref[g], NEG_INF).astype(BF16)], axis=1)
               for g in range(N_KV)]
    b1 = _lane_tile(b1_ref[...], SLC_CHUNK // LANES)

    def gathered(ref, g, first, cols):
        parts = []
        for k in range(SLC_BLOCKS):
            slot = jnp.minimum(first + k, LANES - 1)
            k0 = pl.multiple_of(lists_ref[g * nqb + i, slot] * SEL_BLOCK, SEL_BLOCK)
            parts.append(ref[pl.ds(k0, SEL_BLOCK), cols])
        return jnp.concatenate(parts, axis=0)

    def slc_step(c, carry):
        m, acc = carry
        first = c * SLC_BLOCKS
        u = both(lambda g: _dot_nt(qx_past[g], jnp.concatenate(
            [gathered(ksv_ref, g, first, gcols(g)), gathered(et_ref, g, first, slice(None))], axis=1))) + b1
        m_new = jnp.maximum(m, jnp.max(u, axis=1, keepdims=True))
        e = jnp.exp(u - m_new).astype(BF16)
        pv = both(lambda g: _dot(grows(e, g), jnp.concatenate(
            [gathered(ksv_ref, g, first, vcols(g)), ones_col[:SLC_CHUNK]], axis=1)))
        return m_new, jnp.exp(m - m_new) * acc + pv

    n_blocks = jnp.maximum(counts_ref[i], counts_ref[nqb + i])
    n_trips = (n_blocks + (SLC_BLOCKS - 1)) // SLC_BLOCKS
    m, acc = lax.fori_loop(0, n_trips, slc_step,
                           (jnp.full((N_KV * QROWS, 1), NEG_INF, F32), jnp.zeros((N_KV * QROWS, 2 * HEAD_DIM), F32)))
    u = both(lambda g: _dot_nt(qx[g], jnp.concatenate(
        [ksv_ref[pl.ds(t0, Q_BLOCK), gcols(g)], et_ref[pl.ds(t0, Q_BLOCK), :]], axis=1))) + b1d_ref[...]
    m_new = jnp.maximum(m, jnp.max(u, axis=1, keepdims=True))
    e = jnp.exp(u - m_new).astype(BF16)
    pv = both(lambda g: _dot(grows(e, g), jnp.concatenate(
        [ksv_ref[pl.ds(t0, Q_BLOCK), vcols(g)], ones_col[:Q_BLOCK]], axis=1)))
    acc = jnp.exp(m - m_new) * acc + pv
    o_slc = acc[:, :HEAD_DIM] * (1.0 / acc[:, HEAD_DIM:])

    outs = [partial[hh] + slc_gate[hh] * o_slc[hh * Q_BLOCK:(hh + 1) * Q_BLOCK] for hh in range(N_HEADS)]
    o_ref[...] = jnp.concatenate(outs, axis=1).astype(BF16)


def _nsa_tables(t):
    nc = t // CMP_STRIDE
    f32 = np.float32
    r = (np.arange(QROWS) % Q_BLOCK)[:, None].astype(f32)
    slopes = np.exp2(-8.0 * np.arange(1, N_HEADS + 1, dtype=f32) / N_HEADS).astype(f32)
    slr = np.repeat(slopes.reshape(N_KV, HEADS_PER_KV), Q_BLOCK, axis=1)[:, :, None]
    bc = slr * ((np.arange(nc) * CMP_STRIDE + CMP_BLOCK - 1)[None, None, :] - r[None])
    dw = r - np.arange(WIN_KEYS)[None, :] + WINDOW
    bw = np.where((dw >= 0) & (dw < WINDOW), -slr * dw[None], f32(NEG_INF))
    bw = np.concatenate([bw, np.full(bw.shape[:2] + (WINDOW,), NEG_INF, f32)], axis=2)
    kk = np.arange(LANES)[None, :]
    b1 = slr * (kk % SEL_BLOCK)[None]
    b1d = np.where(kk <= r, b1, f32(NEG_INF))
    sj = slr * (SEL_BLOCK * kk)[None]
    et = (np.arange(t)[:, None] // SEL_BLOCK == kk)
    cmp_start = np.arange(nc) * CMP_STRIDE
    sel_start = np.arange(LANES) * SEL_BLOCK
    ov = np.clip(np.minimum(cmp_start[:, None] + CMP_BLOCK - 1, sel_start[None] + SEL_BLOCK - 1)
                 - np.maximum(cmp_start[:, None], sel_start[None]) + 1, 0, None)
    ov[nc - 1] = 0
    lt = np.tril(np.ones((LANES, LANES)), -1)
    slr = slr * np.ones((1, 1, LANES), f32)
    as_f32 = lambda a: jnp.asarray(a.astype(f32))
    as_bf16 = lambda a: jnp.asarray(a.astype(f32), BF16)
    return dict(bc=as_f32(bc), bw=as_f32(bw), b1=as_f32(b1), b1d=as_f32(b1d), sj=as_f32(sj), slr=as_f32(slr),
                et=as_bf16(et), ov=as_bf16(ov), lt=as_bf16(lt))


def _nsa(o16, kvc, ng, t):
    tb = _nsa_tables(t)
    nc = t // CMP_STRIDE
    nqb = t // Q_BLOCK
    whole = lambda a: pl.BlockSpec(a.shape, lambda i: (0,) * a.ndim)
    ocmp, aug, lists = pl.pallas_call(
        _nsa_select_kernel,
        grid=(nqb,),
        in_specs=[
            pl.BlockSpec((None, Q_BLOCK, PROJ_TN), lambda i: (0, i, 0)),
            pl.BlockSpec((None, Q_BLOCK, PROJ_TN), lambda i: (1, i, 0)),
            whole(kvc), whole(tb["bc"]), whole(tb["sj"]), whole(tb["slr"]), whole(tb["ov"]), whole(tb["lt"]),
        ],
        out_specs=[
            pl.BlockSpec((Q_BLOCK, Q_W), lambda i: (i, 0)),
            pl.BlockSpec((N_KV, QROWS, LANES), lambda i: (0, i, 0)),
            pl.BlockSpec((None, SUBLANES, LANES), lambda i: (i, 0, 0)),
        ],
        out_shape=[
            jax.ShapeDtypeStruct((t, Q_W), F32),
            jax.ShapeDtypeStruct((N_KV, nqb * QROWS, LANES), BF16),
            jax.ShapeDtypeStruct((nqb, SUBLANES, LANES), jnp.int32),
        ],
        compiler_params=_cparams(("arbitrary",)),
        name="nsa_select",
    )(o16, o16, kvc, tb["bc"], tb["sj"], tb["slr"], tb["ov"], tb["lt"])
    block_lists = lists[:, :N_KV, :].transpose(1, 0, 2).reshape(N_KV * nqb, LANES)
    counts = lists[:, N_KV:2 * N_KV, 0].T.reshape(N_KV * nqb)

    stacked = lambda a: a.reshape((N_KV * QROWS,) + a.shape[2:])
    bw, b1, b1d = stacked(tb["bw"]), stacked(tb["b1"]), stacked(tb["b1d"])
    once = pl.Buffered(1)
    const = lambda a: pl.BlockSpec(a.shape, lambda i, *_: (0,) * a.ndim, pipeline_mode=once)
    grid_spec = pltpu.PrefetchScalarGridSpec(
        num_scalar_prefetch=2,
        grid=(nqb,),
        in_specs=[
            pl.BlockSpec((None, Q_BLOCK, PROJ_TN), lambda i, *_: (0, i, 0)),
            pl.BlockSpec((None, Q_BLOCK, PROJ_TN), lambda i, *_: (1, i, 0)),
            pl.BlockSpec((None, t, PROJ_TN), lambda i, *_: (2, 0, 0), pipeline_mode=once),
            pl.BlockSpec((None, t, PROJ_TN), lambda i, *_: (3, 0, 0), pipeline_mode=once),
            pl.BlockSpec((N_KV, QROWS, LANES), lambda i, *_: (0, i, 0)),
            pl.BlockSpec((Q_BLOCK, Q_W), lambda i, *_: (i, 0)),
            pl.BlockSpec((Q_BLOCK, N_KV * LANES), lambda i, *_: (i, 0)),
            const(bw), const(b1), const(b1d), const(tb["et"]),
        ],
        out_specs=pl.BlockSpec((Q_BLOCK, Q_W), lambda i, *_: (i, 0)),
    )
    return pl.pallas_call(
        _nsa_attend_kernel,
        grid_spec=grid_spec,
        out_shape=jax.ShapeDtypeStruct((t, Q_W), BF16),
        compiler_params=_cparams(("arbitrary",)),
        name="nsa_attend",
    )(block_lists, counts, o16, o16, o16, o16, aug, ocmp, ng, bw, b1, b1d, tb["et"])


SGU_CHUNKS_PER_STEP = 4


def _sgu_kernel(u_ref, v_ref, lng_ref, lnb_ref, ws_ref, bs_ref, o_ref):
    gu = jax.nn.gelu(u_ref[...].astype(F32))
    gv = jax.nn.gelu(v_ref[...].astype(F32))
    xc = gv - jnp.mean(gv, -1, keepdims=True)
    vn = (xc * lax.rsqrt(jnp.mean(xc * xc, -1, keepdims=True) + LN_EPS) * lng_ref[...] + lnb_ref[...]).astype(BF16)
    causal = (lax.broadcasted_iota(jnp.int32, (SGU_CHUNK, SGU_CHUNK), 0)
              >= lax.broadcasted_iota(jnp.int32, (SGU_CHUNK, SGU_CHUNK), 1))
    w = [jnp.where(causal, ws_ref[g], 0.0).astype(BF16) for g in range(SGU_GROUPS)]
    rows = []
    for c in range(SGU_CHUNKS_PER_STEP):
        vc = vn[c * SGU_CHUNK:(c + 1) * SGU_CHUNK]
        rows.append(jnp.concatenate(
            [_dot(w[g], vc[:, g * SGU_GROUP_DIM:(g + 1) * SGU_GROUP_DIM]) + bs_ref[g] for g in range(SGU_GROUPS)],
            axis=1))
    o_ref[...] = (gu * jnp.concatenate(rows, axis=0)).astype(BF16)


def _sgu(otail, lng, lnb, ws, bs, t):
    tm = SGU_CHUNKS_PER_STEP * SGU_CHUNK
    return pl.pallas_call(
        _sgu_kernel,
        grid=(t // tm,),
        in_specs=[
            pl.BlockSpec((None, tm, SGU_WIDTH), lambda i: (0, i, 0)),
            pl.BlockSpec((None, tm, SGU_WIDTH), lambda i: (1, i, 0)),
            pl.BlockSpec((1, SGU_WIDTH), lambda i: (0, 0)),
            pl.BlockSpec((1, SGU_WIDTH), lambda i: (0, 0)),
            pl.BlockSpec((SGU_GROUPS, SGU_CHUNK, SGU_CHUNK), lambda i: (0, 0, 0)),
            pl.BlockSpec((SGU_GROUPS, SGU_CHUNK, LANES), lambda i: (0, 0, 0)),
        ],
        out_specs=pl.BlockSpec((tm, SGU_WIDTH), lambda i: (i, 0)),
        out_shape=jax.ShapeDtypeStruct((t, SGU_WIDTH), BF16),
        compiler_params=_cparams(("arbitrary",)),
        name="sgu",
    )(otail, otail, lng, lnb, ws, bs)


def _mix_kernel(ya_ref, yb_ref, ga0_ref, ga1_ref, gb0_ref, gb1_ref, x_ref, pa_ref, pb_ref, wo_ref, g2_ref,
                x1_ref, h2_ref):
    gate = lambda lo, hi: jax.nn.sigmoid(jnp.concatenate([lo[...], hi[...]], axis=1).astype(F32))
    mixed = (gate(ga0_ref, ga1_ref) * _dot(ya_ref[...], pa_ref[...])
             + gate(gb0_ref, gb1_ref) * _dot(yb_ref[...], pb_ref[...]))
    x1 = x_ref[...] + _dot(mixed.astype(BF16), wo_ref[...])
    x1_ref[...] = x1
    h2_ref[...] = _rms(x1, g2_ref[...]).astype(BF16)


def _mix(ya, yb, o32, x, pa, pb, wo, g2, t):
    tm = min(512, t)
    once = pl.Buffered(1)
    tail_tile = lambda k: pl.BlockSpec((None, tm, TAIL_TN), lambda i: (k, i, 0))
    return pl.pallas_call(
        _mix_kernel,
        grid=(t // tm,),
        in_specs=[
            pl.BlockSpec((tm, Q_W), lambda i: (i, 0)),
            pl.BlockSpec((tm, SGU_WIDTH), lambda i: (i, 0)),
            tail_tile(2), tail_tile(3), tail_tile(4), tail_tile(5),
            pl.BlockSpec((tm, D_MODEL), lambda i: (i, 0)),
            pl.BlockSpec((Q_W, D_MODEL), lambda i: (0, 0), pipeline_mode=once),
            pl.BlockSpec((SGU_WIDTH, D_MODEL), lambda i: (0, 0), pipeline_mode=once),
            pl.BlockSpec((D_MODEL, D_MODEL), lambda i: (0, 0), pipeline_mode=once),
            pl.BlockSpec((1, D_MODEL), lambda i: (0, 0)),
        ],
        out_specs=[
            pl.BlockSpec((tm, D_MODEL), lambda i: (i, 0)),
            pl.BlockSpec((tm, D_MODEL), lambda i: (i, 0)),
        ],
        out_shape=[jax.ShapeDtypeStruct((t, D_MODEL), F32), jax.ShapeDtypeStruct((t, D_MODEL), BF16)],
        compiler_params=_cparams(("arbitrary",)),
        name="mix_out",
    )(ya, yb, o32, o32, o32, o32, x, pa, pb, wo, g2)


FFN_TILES = pl.cdiv(D_FF, FFN_TN)
FFN_BACK = FFN_TILES * FFN_TN - D_FF
FFN_MAIN = (FFN_TILES - 1) * FFN_TN


def _ffn_col(j, base=0):
    return (base // LANES + j * (FFN_TN // LANES) - (j // (FFN_TILES - 1)) * (FFN_BACK // LANES)) * LANES


def _ffn_in_kernel(h_ref, wa_ref, wb_ref, cw_ref, cb_ref, o_ref, wa_scr, wb_scr, a_scr):
    tm = h_ref.shape[0]

    @pl.when(pl.program_id(1) == 0)
    def _():
        wa_scr[...] = wa_ref[...].astype(BF16)
        wb_scr[...] = wb_ref[...].astype(BF16)
        a_scr[0:SUBLANES, :] = jnp.zeros((SUBLANES, FFN_TN), F32)

    h = h_ref[...]
    a = _dot(h, wa_scr[...])
    b = _dot(h, wb_scr[...])
    a_scr[SUBLANES:SUBLANES + tm, :] = a
    cw = cw_ref[...]
    y = (cw[0:1] * a_scr[pl.ds(SUBLANES - 2, tm), :] + cw[1:2] * a_scr[pl.ds(SUBLANES - 1, tm), :]
         + cw[2:3] * a + cb_ref[...])
    o_ref[...] = (jax.nn.gelu(y) * b).astype(BF16)
    a_scr[0:SUBLANES, :] = a[tm - SUBLANES:tm, :]


def _ffn_in(h2, w, cw, cb, t):
    tm = min(1024, t)
    tile = lambda rows, base: pl.BlockSpec((pl.Element(rows), pl.Element(FFN_TN)),
                                           lambda j, i: (0, _ffn_col(j, base)))
    return pl.pallas_call(
        _ffn_in_kernel,
        grid=(FFN_TILES, t // tm),
        in_specs=[
            pl.BlockSpec((tm, D_MODEL), lambda j, i: (i, 0)),
            tile(D_MODEL, 0),
            tile(D_MODEL, D_FF),
            tile(SUBLANES, 0),
            tile(1, 0),
        ],
        out_specs=pl.BlockSpec((tm, FFN_TN), lambda j, i: (i, j)),
        out_shape=jax.ShapeDtypeStruct((t, FFN_TILES * FFN_TN), BF16),
        scratch_shapes=[pltpu.VMEM((D_MODEL, FFN_TN), BF16), pltpu.VMEM((D_MODEL, FFN_TN), BF16),
                        pltpu.VMEM((tm + SUBLANES, FFN_TN), F32)],
        compiler_params=_cparams(("arbitrary", "arbitrary")),
        name="ffn_in",
    )(h2, w, w, cw, cb)


def _ffn_out_kernel(a0_ref, a1_ref, w0_ref, w1_ref, x_ref, g_ref, o_ref, *, final_norm):
    x2 = x_ref[...] + _dot(a0_ref[...], w0_ref[...]) + _dot(a1_ref[...], w1_ref[...])
    o_ref[...] = _rms(x2, g_ref[...]) if final_norm else x2


def _ffn_out(act, wd, x1, g, t, final_norm):
    tm = min(512, t)
    once = pl.Buffered(1)
    rest = D_FF - FFN_MAIN
    return pl.pallas_call(
        functools.partial(_ffn_out_kernel, final_norm=final_norm),
        grid=(t // tm,),
        in_specs=[
            pl.BlockSpec((pl.Element(tm), pl.Element(FFN_MAIN)), lambda i: (i * tm, 0)),
            pl.BlockSpec((pl.Element(tm), pl.Element(rest)), lambda i: (i * tm, FFN_MAIN + FFN_BACK)),
            pl.BlockSpec((pl.Element(FFN_MAIN), pl.Element(D_MODEL)), lambda i: (0, 0), pipeline_mode=once),
            pl.BlockSpec((pl.Element(rest), pl.Element(D_MODEL)), lambda i: (FFN_MAIN, 0), pipeline_mode=once),
            pl.BlockSpec((tm, D_MODEL), lambda i: (i, 0)),
            pl.BlockSpec((1, D_MODEL), lambda i: (0, 0)),
        ],
        out_specs=pl.BlockSpec((tm, D_MODEL), lambda i: (i, 0)),
        out_shape=jax.ShapeDtypeStruct((t, D_MODEL), F32),
        compiler_params=_cparams(("arbitrary",)),
        name="ffn_out",
    )(act, act, wd, wd, x1, g)


def _layer(x, p, final_g, t):
    offs = np.cumsum((0,) + IN_SIZES)
    w_in = p["w_in"]
    wg = w_in[:, offs[7]:offs[8]].astype(BF16).reshape(D_MODEL, N_KV, HEADS_PER_KV * 3)
    wg = jnp.pad(wg, ((0, 0), (0, 0), (0, LANES - HEADS_PER_KV * 3))).reshape(D_MODEL, N_KV * LANES)
    tile_major = lambda w, tn: w.astype(BF16).reshape(D_MODEL, w.shape[1] // tn, tn).transpose(1, 0, 2)
    o16, otail, okvc, ng = _in_proj(x, p["norm1_g"][None, :], tile_major(w_in[:, :offs[7]], PROJ_TN),
                                    tile_major(w_in[:, offs[8]:], TAIL_TN), wg)

    pe = jnp.stack([p["cmp_k_pe"], p["cmp_v_pe"]])
    w1 = jnp.stack([p["cmp_k_w1"], p["cmp_v_w1"]]).reshape(2, CMP_BLOCK, HEAD_DIM, HEAD_DIM).astype(BF16)
    w2 = jnp.stack([p["cmp_k_w2"], p["cmp_v_w2"]]).astype(BF16)
    kvc = _compress(okvc, 0, pe, w1, w2, t)

    y_a = _nsa(o16, kvc, ng, t)
    bs = jnp.broadcast_to(p["sgu_b"][:, :, None], (SGU_GROUPS, SGU_CHUNK, LANES))
    y_b = _sgu(otail, p["sgu_ln_g"][None, :], p["sgu_ln_b"][None, :], p["sgu_w"], bs, t)
    x1, h2 = _mix(y_a, y_b, otail, x, p["w_branch_a"].astype(BF16), p["w_branch_b"].astype(BF16),
                  p["w_out"].astype(BF16), p["norm2_g"][None, :], t)

    cw = jnp.pad(p["ffn_conv_w"], ((0, SUBLANES - CONV_WIDTH), (0, 0)))
    act = _ffn_in(h2, p["ffn_w_in"], cw, p["ffn_conv_b"][None, :], t)
    wd = p["ffn_w_down"].astype(BF16)
    g = final_g[None, :] if final_g is not None else jnp.ones((1, D_MODEL), F32)
    return _ffn_out(act, wd, x1, g, t, final_g is not None)


def kernel(x, norm1_g, w_in, cmp_k_pe, cmp_k_w1, cmp_k_w2, cmp_v_pe, cmp_v_w1, cmp_v_w2, sgu_ln_g, sgu_ln_b, sgu_w, sgu_b, w_branch_a, w_branch_b, w_out, norm2_g, ffn_w_in, ffn_conv_w, ffn_conv_b, ffn_w_down, final_g):
    b, t, _ = x.shape
    assert b == 1 and t % 1024 == 0, "one sequence whose length is a multiple of 1024"
    params = dict(norm1_g=norm1_g, w_in=w_in, cmp_k_pe=cmp_k_pe, cmp_k_w1=cmp_k_w1, cmp_k_w2=cmp_k_w2,
                  cmp_v_pe=cmp_v_pe, cmp_v_w1=cmp_v_w1, cmp_v_w2=cmp_v_w2, sgu_ln_g=sgu_ln_g, sgu_ln_b=sgu_ln_b,
                  sgu_w=sgu_w, sgu_b=sgu_b, w_branch_a=w_branch_a, w_branch_b=w_branch_b, w_out=w_out,
                  norm2_g=norm2_g, ffn_w_in=ffn_w_in, ffn_conv_w=ffn_conv_w, ffn_conv_b=ffn_conv_b,
                  ffn_w_down=ffn_w_down)
    depth = norm1_g.shape[0]
    h = x[0]
    for l in range(depth):
        layer = {k: v[l] for k, v in params.items()}
        h = _layer(h, layer, final_g if l == depth - 1 else None, t)
    return h[None]
```

```python
import functools

import numpy as np
import jax
import jax.numpy as jnp
from jax import lax
from jax.experimental import pallas as pl
from jax.experimental.pallas import tpu as pltpu

F32 = jnp.float32
BF16 = jnp.bfloat16

D_MODEL = 2048
N_HEADS = 8
HEAD_DIM = 128
N_KV = 2
HEADS_PER_KV = N_HEADS // N_KV
CMP_BLOCK = 32
CMP_STRIDE = 16
SEL_BLOCK = 64
N_SELECT = 16
N_FORCED = 3
WINDOW = 512
Q_BLOCK = 128
SGU_WIDTH = 1024
SGU_GROUPS = 8
SGU_GROUP_DIM = SGU_WIDTH // SGU_GROUPS
SGU_CHUNK = 128
D_FF = 5504
CONV_WIDTH = 3
NORM_EPS = 1e-6
LN_EPS = 1e-5
NEG_INF = -1e30

Q_W = N_HEADS * HEAD_DIM
KV_W = N_KV * HEAD_DIM
NSA_GATE_W = 3 * N_HEADS
IN_SIZES = (Q_W, KV_W, KV_W, KV_W, KV_W, KV_W, KV_W, NSA_GATE_W, SGU_WIDTH, SGU_WIDTH, D_MODEL, D_MODEL)

LANES = 128
SUBLANES = 8
QROWS = HEADS_PER_KV * Q_BLOCK
SEL_SHIFT = SEL_BLOCK.bit_length() - 1
BLOCKS_PER_Q = Q_BLOCK // SEL_BLOCK
SLC_BLOCKS = 10
SLC_CHUNK = SLC_BLOCKS * SEL_BLOCK
WIN_KEYS = WINDOW + Q_BLOCK
PROJ_TN = 512
N16_TILES = 4
FFN_TN = 512
VMEM_LIMIT = 56 * 1024 * 1024


def _cparams(sem):
    return pltpu.CompilerParams(dimension_semantics=sem, vmem_limit_bytes=VMEM_LIMIT)


def _dot(a, b):
    return jnp.dot(a, b, preferred_element_type=F32)


def _dot_nt(a, b):
    return lax.dot_general(a, b, (((1,), (1,)), ((), ())), preferred_element_type=F32)


def _rms(x, g):
    return x * lax.rsqrt(jnp.mean(x * x, -1, keepdims=True) + NORM_EPS) * g


TAIL_TN = 4 * PROJ_TN
N_HEAD_TILES = N16_TILES + 1


def _proj_tail_kernel(x_ref, g_ref, w_ref, o_ref, h_ref):
    @pl.when(pl.program_id(1) == 0)
    def _():
        h_ref[...] = _rms(x_ref[...], g_ref[...]).astype(BF16)

    o_ref[...] = _dot(h_ref[...], w_ref[...]).astype(BF16)


def _proj_head_kernel(h_ref, w_ref, wg_ref, ohead_ref, okvc_ref, ong_ref):
    j = pl.program_id(1)

    @pl.when(j == 0)
    def _():
        ong_ref[...] = _dot(h_ref[...], wg_ref[...])

    acc = _dot(h_ref[...], w_ref[...])
    okvc_ref[...] = acc
    ohead_ref[...] = (acc * jnp.where(j <= 1, HEAD_DIM ** -0.5, 1.0)).astype(BF16)


def _in_proj(x, g, w_head, w_tail, wg):
    t = x.shape[0]
    tm = min(1024, t)
    otail, h = pl.pallas_call(
        _proj_tail_kernel,
        grid=(t // tm, w_tail.shape[1] // TAIL_TN),
        in_specs=[
            pl.BlockSpec((tm, D_MODEL), lambda i, j: (i, 0)),
            pl.BlockSpec((1, D_MODEL), lambda i, j: (0, 0)),
            pl.BlockSpec((D_MODEL, TAIL_TN), lambda i, j: (0, j)),
        ],
        out_specs=[
            pl.BlockSpec((tm, TAIL_TN), lambda i, j: (i, j)),
            pl.BlockSpec((tm, D_MODEL), lambda i, j: (i, 0)),
        ],
        out_shape=[jax.ShapeDtypeStruct((t, w_tail.shape[1]), BF16), jax.ShapeDtypeStruct((t, D_MODEL), BF16)],
        compiler_params=_cparams(("arbitrary", "arbitrary")),
        name="proj_tail",
    )(x, g, w_tail)
    head_tile = lambda j: jnp.where(j <= 1, j, jnp.where(j <= 3, j + 1, 2))
    ohead, okvc, ng = pl.pallas_call(
        _proj_head_kernel,
        grid=(t // tm, N_HEAD_TILES),
        in_specs=[
            pl.BlockSpec((tm, D_MODEL), lambda i, j: (i, 0)),
            pl.BlockSpec((D_MODEL, PROJ_TN), lambda i, j: (0, head_tile(j))),
            pl.BlockSpec((D_MODEL, 2 * LANES), lambda i, j: (0, 0)),
        ],
        out_specs=[
            pl.BlockSpec((tm, PROJ_TN), lambda i, j: (i, j)),
            pl.BlockSpec((tm, PROJ_TN), lambda i, j: (i, 0)),
            pl.BlockSpec((tm, 2 * LANES), lambda i, j: (i, 0)),
        ],
        out_shape=[
            jax.ShapeDtypeStruct((t, N_HEAD_TILES * PROJ_TN), BF16),
            jax.ShapeDtypeStruct((t, PROJ_TN), F32),
            jax.ShapeDtypeStruct((t, 2 * LANES), F32),
        ],
        compiler_params=_cparams(("arbitrary", "arbitrary")),
        name="proj_head",
    )(h, w_head, wg)
    return ohead, otail, okvc, ng


def _compress_kernel(kv_ref, pe_ref, w1_ref, w2_ref, o_ref):
    nu = o_ref.shape[0]
    a = jnp.zeros((nu, HEAD_DIM), F32)
    b = jnp.zeros((nu, HEAD_DIM), F32)
    for r in range(CMP_STRIDE):
        x = kv_ref[pl.ds(r, nu, stride=CMP_STRIDE), :]
        a += _dot((x + pe_ref[pl.ds(r, 1), :]).astype(BF16), w1_ref[r])
        b += _dot((x + pe_ref[pl.ds(CMP_STRIDE + r, 1), :]).astype(BF16), w1_ref[CMP_STRIDE + r])
    hid = a + pltpu.roll(b, nu - 1, 0)
    o_ref[...] = _dot(jax.nn.gelu(hid).astype(BF16), w2_ref[...]).astype(BF16)


def _compress(o32, col0, pe, w1, w2, t):
    nu = t // CMP_STRIDE
    return pl.pallas_call(
        _compress_kernel,
        grid=(2, N_KV),
        in_specs=[
            pl.BlockSpec((t, HEAD_DIM), lambda a, g: (0, col0 + N_KV * a + g)),
            pl.BlockSpec((None, CMP_BLOCK, HEAD_DIM), lambda a, g: (a, 0, 0)),
            pl.BlockSpec((None, CMP_BLOCK, HEAD_DIM, HEAD_DIM), lambda a, g: (a, 0, 0, 0)),
            pl.BlockSpec((None, HEAD_DIM, HEAD_DIM), lambda a, g: (a, 0, 0)),
        ],
        out_specs=pl.BlockSpec((None, None, nu, HEAD_DIM), lambda a, g: (a, g, 0, 0)),
        out_shape=jax.ShapeDtypeStruct((2, N_KV, nu, HEAD_DIM), BF16),
        compiler_params=_cparams(("arbitrary", "arbitrary")),
        name="compress",
    )(o32, pe, w1, w2)


def _lane_tile(x, n):
    return jnp.concatenate([x] * n, axis=1)


def _row_tile(x, n):
    return jnp.concatenate([x] * n, axis=0)


def _softmax_numerators(s):
    return jnp.exp(s - jnp.max(s, axis=1, keepdims=True)).astype(BF16)


def _stack_heads(q, g):
    return jnp.concatenate([q[:, (g * HEADS_PER_KV + h) * HEAD_DIM:(g * HEADS_PER_KV + h + 1) * HEAD_DIM]
                            for h in range(HEADS_PER_KV)], axis=0)


def _nsa_select_kernel(q_ref, kvc_ref, bc_ref, sj_ref, slr_ref, ov_ref, lt_ref, ocmp_ref, aug_ref, lists_ref):
    i = pl.program_id(0)
    t0 = i * Q_BLOCK
    t0f = t0.astype(F32)
    q = q_ref[...]
    nc = kvc_ref.shape[2]
    row = lax.broadcasted_iota(jnp.int32, (QROWS, 1), 0) & (Q_BLOCK - 1)
    jr = lax.broadcasted_iota(jnp.int32, (LANES, Q_BLOCK), 0)
    kl = lax.broadcasted_iota(jnp.int32, (LANES, Q_BLOCK), 1)
    tq = t0 + kl
    cur = lax.shift_right_logical(tq, SEL_SHIFT)
    forced = (jr == 0) | (jr == cur) | (jr == cur - 1)
    visible = jr * SEL_BLOCK <= tq
    jf = jr.astype(F32)

    ocmp, imps = [], []
    for g in range(N_KV):
        bc = bc_ref[g]
        s = jnp.where(bc <= _lane_tile(slr_ref[g] * t0f, nc // LANES),
                      _dot_nt(_stack_heads(q, g), kvc_ref[0, g]) + bc, NEG_INF)
        r = _dot(_softmax_numerators(s), jnp.concatenate([kvc_ref[1, g], ov_ref[...]], axis=1))
        l = jnp.sum(r[:, HEAD_DIM:], axis=1, keepdims=True) * (1.0 / CMP_BLOCK)
        inv = jnp.where(t0 + row >= CMP_BLOCK - 1, 1.0 / l, 0.0)
        o = r[:, :HEAD_DIM] * inv
        ocmp += [o[h * Q_BLOCK:(h + 1) * Q_BLOCK] for h in range(HEADS_PER_KV)]
        imp = r[:, HEAD_DIM:] * inv
        imps.append(sum(imp[h * Q_BLOCK:(h + 1) * Q_BLOCK] for h in range(HEADS_PER_KV)).T)
    ocmp_ref[...] = jnp.concatenate(ocmp, axis=1)

    forced2, visible2, jf2 = (_lane_tile(a, N_KV) for a in (forced, visible, jf))
    score = jnp.where(forced2 | jnp.logical_not(visible2), NEG_INF, jnp.concatenate(imps, axis=1))
    for _ in range(N_SELECT - N_FORCED):
        mx = jnp.max(score, axis=0, keepdims=True)
        idx = jnp.min(jnp.where(score == mx, jf2, float(LANES)), axis=0, keepdims=True)
        score = jnp.where(jf2 == idx, -jnp.inf, score)
    sel_all = jnp.where(visible2 & (forced2 | (score == -jnp.inf)), 1.0, 0.0)

    rows = []
    for g in range(N_KV):
        sel_t = sel_all[:, g * Q_BLOCK:(g + 1) * Q_BLOCK]
        aug_ref[g] = jnp.where(_row_tile(sel_t.T, HEADS_PER_KV) > 0.5, sj_ref[g], NEG_INF).astype(BF16)

        used = jnp.max(jnp.where(jr < BLOCKS_PER_Q * i, sel_t, 0.0), axis=1, keepdims=True)
        used = jnp.broadcast_to(used, (LANES, Q_BLOCK))
        slot = _dot(lt_ref[...], used.astype(BF16))
        lst = jnp.sum(jnp.where((slot == kl.astype(F32)) & (used > 0.5), jf, 0.0), axis=0, keepdims=True)
        cnt = jnp.sum(used, axis=0, keepdims=True)
        lst = jnp.where(kl[:1].astype(F32) < cnt, lst, (BLOCKS_PER_Q * i).astype(F32))
        rows += [lst, cnt]
    pad =[jnp.zeros((1, LANES), F32)] * (SUBLANES - len(rows))
    lists_ref[...] = jnp.concatenate(rows[0::2] + rows[1::2] + pad, axis=0).astype(jnp.int32)


def _nsa_attend_kernel(lists_ref, counts_ref, q_ref, ks_ref, vs_ref, kw_ref, vw_ref, aug_ref, ocmp_ref, ng_ref,
                       bw_ref, b1_ref, b1d_ref, et_ref, o_ref):
    i = pl.program_id(0)
    nqb = pl.num_programs(0)
    t0 = i * Q_BLOCK
    q = q_ref[...]
    q4 = [_stack_heads(q, g) for g in range(N_KV)]
    ones_col = jnp.ones((WIN_KEYS, LANES), BF16)
    gcols = lambda g: slice(g * HEAD_DIM, (g + 1) * HEAD_DIM)
    grows = lambda a, g: a[g * QROWS:(g + 1) * QROWS]
    both = lambda f: jnp.concatenate([f(g) for g in range(N_KV)], axis=0)

    w0 = pl.multiple_of(jnp.maximum(t0 - WINDOW, 0), Q_BLOCK)
    shift = pl.multiple_of(w0 - (t0 - WINDOW), Q_BLOCK)
    s = both(lambda g: _dot_nt(q4[g], kw_ref[pl.ds(w0, WIN_KEYS), gcols(g)])) + bw_ref[:, pl.ds(shift, WIN_KEYS)]
    e = _softmax_numerators(s)
    r = both(lambda g: _dot(grows(e, g), jnp.concatenate([vw_ref[pl.ds(w0, WIN_KEYS), gcols(g)], ones_col], axis=1)))
    o_win = r[:, :HEAD_DIM] * (1.0 / r[:, HEAD_DIM:])

    gl = lax.broadcasted_iota(jnp.int32, (Q_BLOCK, LANES), 1)
    o_cmp = ocmp_ref[...]
    slc_gate, partial = [], []
    for g in range(N_KV):
        sg = jax.nn.sigmoid(ng_ref[:, gcols(g)])
        gate = lambda col: jnp.sum(jnp.where(gl == col, sg, 0.0), axis=1, keepdims=True)
        for h in range(HEADS_PER_KV):
            hh = g * HEADS_PER_KV + h
            slc_gate.append(gate(3 * h + 1))
            partial.append(gate(3 * h) * o_cmp[:, hh * HEAD_DIM:(hh + 1) * HEAD_DIM]
                           + gate(3 * h + 2) * o_win[hh * Q_BLOCK:(hh + 1) * Q_BLOCK])

    jl = lax.broadcasted_iota(jnp.int32, (QROWS, LANES), 1)
    qx = [jnp.concatenate([q4[g], aug_ref[g]], axis=1) for g in range(N_KV)]
    qx_past = [jnp.concatenate([q4[g], jnp.where(jl < BLOCKS_PER_Q * i, aug_ref[g], NEG_INF).astype(BF16)], axis=1)
               for g in range(N_KV)]
    b1 = _lane_tile(b1_ref[...], SLC_CHUNK // LANES)

    def gathered(ref, g, first, cols):
        parts = []
        for k in range(SLC_BLOCKS):
            slot = jnp.minimum(first + k, LANES - 1)
            k0 = pl.multiple_of(lists_ref[g * nqb + i, slot] * SEL_BLOCK, SEL_BLOCK)
            parts.append(ref[pl.ds(k0, SEL_BLOCK), cols])
        return jnp.concatenate(parts, axis=0)

    def slc_step(c, carry):
        m, acc = carry
        first = c * SLC_BLOCKS
        u = both(lambda g: _dot_nt(qx_past[g], jnp.concatenate(
            [gathered(ks_ref, g, first, gcols(g)), gathered(et_ref, g, first, slice(None))], axis=1))) + b1
        m_new = jnp.maximum(m, jnp.max(u, axis=1, keepdims=True))
        e = jnp.exp(u - m_new).astype(BF16)
        pv = both(lambda g: _dot(grows(e, g), jnp.concatenate(
            [gathered(vs_ref, g, first, gcols(g)), ones_col[:SLC_CHUNK]], axis=1)))
        return m_new, jnp.exp(m - m_new) * acc + pv

    n_blocks = jnp.maximum(counts_ref[i], counts_ref[nqb + i])
    n_trips = (n_blocks + (SLC_BLOCKS - 1)) // SLC_BLOCKS
    m, acc = lax.fori_loop(0, n_trips, slc_step,
                           (jnp.full((N_KV * QROWS, 1), NEG_INF, F32), jnp.zeros((N_KV * QROWS, 2 * HEAD_DIM), F32)))
    u = both(lambda g: _dot_nt(qx[g], jnp.concatenate(
        [ks_ref[pl.ds(t0, Q_BLOCK), gcols(g)], et_ref[pl.ds(t0, Q_BLOCK), :]], axis=1))) + b1d_ref[...]
    m_new = jnp.maximum(m, jnp.max(u, axis=1, keepdims=True))
    e = jnp.exp(u - m_new).astype(BF16)
    pv = both(lambda g: _dot(grows(e, g), jnp.concatenate(
        [vs_ref[pl.ds(t0, Q_BLOCK), gcols(g)], ones_col[:Q_BLOCK]], axis=1)))
    acc = jnp.exp(m - m_new) * acc + pv
    o_slc = acc[:, :HEAD_DIM] * (1.0 / acc[:, HEAD_DIM:])

    outs = [partial[hh] + slc_gate[hh] * o_slc[hh * Q_BLOCK:(hh + 1) * Q_BLOCK] for hh in range(N_HEADS)]
    o_ref[...] = jnp.concatenate(outs, axis=1).astype(BF16)


def _nsa_tables(t):
    nc = t // CMP_STRIDE
    f32 = np.float32
    r = (np.arange(QROWS) % Q_BLOCK)[:, None].astype(f32)
    slopes = np.exp2(-8.0 * np.arange(1, N_HEADS + 1, dtype=f32) / N_HEADS).astype(f32)
    slr = np.repeat(slopes.reshape(N_KV, HEADS_PER_KV), Q_BLOCK, axis=1)[:, :, None]
    bc = slr * ((np.arange(nc) * CMP_STRIDE + CMP_BLOCK - 1)[None, None, :] - r[None])
    dw = r - np.arange(WIN_KEYS)[None, :] + WINDOW
    bw = np.where((dw >= 0) & (dw < WINDOW), -slr * dw[None], f32(NEG_INF))
    bw = np.concatenate([bw, np.full(bw.shape[:2] + (WINDOW,), NEG_INF, f32)], axis=2)
    kk = np.arange(LANES)[None, :]
    b1 = slr * (kk % SEL_BLOCK)[None]
    b1d = np.where(kk <= r, b1, f32(NEG_INF))
    sj = slr * (SEL_BLOCK * kk)[None]
    et = (np.arange(t)[:, None] // SEL_BLOCK == kk)
    cmp_start = np.arange(nc) * CMP_STRIDE
    sel_start = np.arange(LANES) * SEL_BLOCK
    ov = np.clip(np.minimum(cmp_start[:, None] + CMP_BLOCK - 1, sel_start[None] + SEL_BLOCK - 1)
                 - np.maximum(cmp_start[:, None], sel_start[None]) + 1, 0, None)
    ov[nc - 1] = 0
    lt = np.tril(np.ones((LANES, LANES)), -1)
    slr = slr * np.ones((1, 1, LANES), f32)
    as_f32 = lambda a: jnp.asarray(a.astype(f32))
    as_bf16 = lambda a: jnp.asarray(a.astype(f32), BF16)
    return dict(bc=as_f32(bc), bw=as_f32(bw), b1=as_f32(b1), b1d=as_f32(b1d), sj=as_f32(sj), slr=as_f32(slr),
                et=as_bf16(et), ov=as_bf16(ov), lt=as_bf16(lt))


def _nsa(o16, kvc, ng, t):
    tb = _nsa_tables(t)
    nc = t // CMP_STRIDE
    nqb = t // Q_BLOCK
    whole = lambda a: pl.BlockSpec(a.shape, lambda i: (0,) * a.ndim)
    ocmp, aug, lists = pl.pallas_call(
        _nsa_select_kernel,
        grid=(nqb,),
        in_specs=[
            pl.BlockSpec((Q_BLOCK, Q_W), lambda i: (i, 0)),
            whole(kvc), whole(tb["bc"]), whole(tb["sj"]), whole(tb["slr"]), whole(tb["ov"]), whole(tb["lt"]),
        ],
        out_specs=[
            pl.BlockSpec((Q_BLOCK, Q_W), lambda i: (i, 0)),
            pl.BlockSpec((N_KV, QROWS, LANES), lambda i: (0, i, 0)),
            pl.BlockSpec((None, SUBLANES, LANES), lambda i: (i, 0, 0)),
        ],
        out_shape=[
            jax.ShapeDtypeStruct((t, Q_W), F32),
            jax.ShapeDtypeStruct((N_KV, nqb * QROWS, LANES), BF16),
            jax.ShapeDtypeStruct((nqb, SUBLANES, LANES), jnp.int32),
        ],
        compiler_params=_cparams(("arbitrary",)),
        name="nsa_select",
    )(o16, kvc, tb["bc"], tb["sj"], tb["slr"], tb["ov"], tb["lt"])
    block_lists = lists[:, :N_KV, :].transpose(1, 0, 2).reshape(N_KV * nqb, LANES)
    counts = lists[:, N_KV:2 * N_KV, 0].T.reshape(N_KV * nqb)

    stacked = lambda a: a.reshape((N_KV * QROWS,) + a.shape[2:])
    bw, b1, b1d = stacked(tb["bw"]), stacked(tb["b1"]), stacked(tb["b1d"])
    once = pl.Buffered(1)
    const = lambda a: pl.BlockSpec(a.shape, lambda i, *_: (0,) * a.ndim, pipeline_mode=once)
    grid_spec = pltpu.PrefetchScalarGridSpec(
        num_scalar_prefetch=2,
        grid=(nqb,),
        in_specs=[
            pl.BlockSpec((Q_BLOCK, Q_W), lambda i, *_: (i, 0)),
            pl.BlockSpec((t, KV_W), lambda i, *_: (0, Q_W // KV_W), pipeline_mode=once),
            pl.BlockSpec((t, KV_W), lambda i, *_: (0, Q_W // KV_W + 1), pipeline_mode=once),
            pl.BlockSpec((t, KV_W), lambda i, *_: (0, Q_W // KV_W + 2), pipeline_mode=once),
            pl.BlockSpec((t, KV_W), lambda i, *_: (0, Q_W // KV_W + 3), pipeline_mode=once),
            pl.BlockSpec((N_KV, QROWS, LANES), lambda i, *_: (0, i, 0)),
            pl.BlockSpec((Q_BLOCK, Q_W), lambda i, *_: (i, 0)),
            pl.BlockSpec((Q_BLOCK, N_KV * LANES), lambda i, *_: (i, 0)),
            const(bw), const(b1), const(b1d), const(tb["et"]),
        ],
        out_specs=pl.BlockSpec((Q_BLOCK, Q_W), lambda i, *_: (i, 0)),
    )
    return pl.pallas_call(
        _nsa_attend_kernel,
        grid_spec=grid_spec,
        out_shape=jax.ShapeDtypeStruct((t, Q_W), BF16),
        compiler_params=_cparams(("arbitrary",)),
        name="nsa_attend",
    )(block_lists, counts, o16, o16, o16, o16, o16, aug, ocmp, ng, bw, b1, b1d, tb["et"])


SGU_CHUNKS_PER_STEP = 4


def _sgu_kernel(u_ref, v_ref, lng_ref, lnb_ref, ws_ref, bs_ref, o_ref):
    gu = jax.nn.gelu(u_ref[...].astype(F32))
    gv = jax.nn.gelu(v_ref[...].astype(F32))
    xc = gv - jnp.mean(gv, -1, keepdims=True)
    vn = (xc * lax.rsqrt(jnp.mean(xc * xc, -1, keepdims=True) + LN_EPS) * lng_ref[...] + lnb_ref[...]).astype(BF16)
    causal = (lax.broadcasted_iota(jnp.int32, (SGU_CHUNK, SGU_CHUNK), 0)
              >= lax.broadcasted_iota(jnp.int32, (SGU_CHUNK, SGU_CHUNK), 1))
    w = [jnp.where(causal, ws_ref[g], 0.0).astype(BF16) for g in range(SGU_GROUPS)]
    rows = []
    for c in range(SGU_CHUNKS_PER_STEP):
        vc = vn[c * SGU_CHUNK:(c + 1) * SGU_CHUNK]
        rows.append(jnp.concatenate(
            [_dot(w[g], vc[:, g * SGU_GROUP_DIM:(g + 1) * SGU_GROUP_DIM]) + bs_ref[g] for g in range(SGU_GROUPS)],
            axis=1))
    o_ref[...] = (gu * jnp.concatenate(rows, axis=0)).astype(BF16)


def _sgu(otail, lng, lnb, ws, bs, t):
    tm = SGU_CHUNKS_PER_STEP * SGU_CHUNK
    return pl.pallas_call(
        _sgu_kernel,
        grid=(t // tm,),
        in_specs=[
            pl.BlockSpec((tm, SGU_WIDTH), lambda i: (i, 0)),
            pl.BlockSpec((tm, SGU_WIDTH), lambda i: (i, 1)),
            pl.BlockSpec((1, SGU_WIDTH), lambda i: (0, 0)),
            pl.BlockSpec((1, SGU_WIDTH), lambda i: (0, 0)),
            pl.BlockSpec((SGU_GROUPS, SGU_CHUNK, SGU_CHUNK), lambda i: (0, 0, 0)),
            pl.BlockSpec((SGU_GROUPS, SGU_CHUNK, LANES), lambda i: (0, 0, 0)),
        ],
        out_specs=pl.BlockSpec((tm, SGU_WIDTH), lambda i: (i, 0)),
        out_shape=jax.ShapeDtypeStruct((t, SGU_WIDTH), BF16),
        compiler_params=_cparams(("arbitrary",)),
        name="sgu",
    )(otail, otail, lng, lnb, ws, bs)


def _mix_kernel(ya_ref, yb_ref, ga_ref, gb_ref, x_ref, pa_ref, pb_ref, wo_ref, g2_ref, x1_ref, h2_ref):
    mixed = (jax.nn.sigmoid(ga_ref[...].astype(F32)) * _dot(ya_ref[...], pa_ref[...])
             + jax.nn.sigmoid(gb_ref[...].astype(F32)) * _dot(yb_ref[...], pb_ref[...]))
    x1 = x_ref[...] + _dot(mixed.astype(BF16), wo_ref[...])
    x1_ref[...] = x1
    h2_ref[...] = _rms(x1, g2_ref[...]).astype(BF16)


def _mix(ya, yb, o32, x, pa, pb, wo, g2, t):
    tm = min(512, t)
    once = pl.Buffered(1)
    return pl.pallas_call(
        _mix_kernel,
        grid=(t // tm,),
        in_specs=[
            pl.BlockSpec((tm, Q_W), lambda i: (i, 0)),
            pl.BlockSpec((tm, SGU_WIDTH), lambda i: (i, 0)),
            pl.BlockSpec((tm, D_MODEL), lambda i: (i, 1)),
            pl.BlockSpec((tm, D_MODEL), lambda i: (i, 2)),
            pl.BlockSpec((tm, D_MODEL), lambda i: (i, 0)),
            pl.BlockSpec((Q_W, D_MODEL), lambda i: (0, 0), pipeline_mode=once),
            pl.BlockSpec((SGU_WIDTH, D_MODEL), lambda i: (0, 0), pipeline_mode=once),
            pl.BlockSpec((D_MODEL, D_MODEL), lambda i: (0, 0), pipeline_mode=once),
            pl.BlockSpec((1, D_MODEL), lambda i: (0, 0)),
        ],
        out_specs=[
            pl.BlockSpec((tm, D_MODEL), lambda i: (i, 0)),
            pl.BlockSpec((tm, D_MODEL), lambda i: (i, 0)),
        ],
        out_shape=[jax.ShapeDtypeStruct((t, D_MODEL), F32), jax.ShapeDtypeStruct((t, D_MODEL), BF16)],
        compiler_params=_cparams(("arbitrary",)),
        name="mix_out",
    )(ya, yb, o32, o32, x, pa, pb, wo, g2)


FFN_TILES = pl.cdiv(D_FF, FFN_TN)
FFN_BACK = FFN_TILES * FFN_TN - D_FF
FFN_MAIN = (FFN_TILES - 1) * FFN_TN


def _ffn_col(j, base=0):
    return (base // LANES + j * (FFN_TN // LANES) - (j // (FFN_TILES - 1)) * (FFN_BACK // LANES)) * LANES


def _ffn_in_kernel(h_ref, wa_ref, wb_ref, cw_ref, cb_ref, o_ref, wa_scr, wb_scr, a_scr):
    tm = h_ref.shape[0]

    @pl.when(pl.program_id(1) == 0)
    def _():
        wa_scr[...] = wa_ref[...].astype(BF16)
        wb_scr[...] = wb_ref[...].astype(BF16)
        a_scr[0:SUBLANES, :] = jnp.zeros((SUBLANES, FFN_TN), F32)

    h = h_ref[...]
    a = _dot(h, wa_scr[...])
    b = _dot(h, wb_scr[...])
    a_scr[SUBLANES:SUBLANES + tm, :] = a
    cw = cw_ref[...]
    y = (cw[0:1] * a_scr[pl.ds(SUBLANES - 2, tm), :] + cw[1:2] * a_scr[pl.ds(SUBLANES - 1, tm), :]
         + cw[2:3] * a + cb_ref[...])
    o_ref[...] = (jax.nn.gelu(y) * b).astype(BF16)
    a_scr[0:SUBLANES, :] = a[tm - SUBLANES:tm, :]


def _ffn_in(h2, w, cw, cb, t):
    tm = min(1024, t)
    tile = lambda rows, base: pl.BlockSpec((pl.Element(rows), pl.Element(FFN_TN)),
                                           lambda j, i: (0, _ffn_col(j, base)))
    return pl.pallas_call(
        _ffn_in_kernel,
        grid=(FFN_TILES, t // tm),
        in_specs=[
            pl.BlockSpec((tm, D_MODEL), lambda j, i: (i, 0)),
            tile(D_MODEL, 0),
            tile(D_MODEL, D_FF),
            tile(SUBLANES, 0),
            tile(1, 0),
        ],
        out_specs=pl.BlockSpec((tm, FFN_TN), lambda j, i: (i, j)),
        out_shape=jax.ShapeDtypeStruct((t, FFN_TILES * FFN_TN), BF16),
        scratch_shapes=[pltpu.VMEM((D_MODEL, FFN_TN), BF16), pltpu.VMEM((D_MODEL, FFN_TN), BF16),
                        pltpu.VMEM((tm + SUBLANES, FFN_TN), F32)],
        compiler_params=_cparams(("arbitrary", "arbitrary")),
        name="ffn_in",
    )(h2, w, w, cw, cb)


def _ffn_out_kernel(a0_ref, a1_ref, w0_ref, w1_ref, x_ref, g_ref, o_ref, *, final_norm):
    x2 = x_ref[...] + _dot(a0_ref[...], w0_ref[...]) + _dot(a1_ref[...], w1_ref[...])
    o_ref[...] = _rms(x2, g_ref[...]) if final_norm else x2


def _ffn_out(act, wd, x1, g, t, final_norm):
    tm = min(512, t)
    once = pl.Buffered(1)
    rest = D_FF - FFN_MAIN
    return pl.pallas_call(
        functools.partial(_ffn_out_kernel, final_norm=final_norm),
        grid=(t // tm,),
        in_specs=[
            pl.BlockSpec((pl.Element(tm), pl.Element(FFN_MAIN)), lambda i: (i * tm, 0)),
            pl.BlockSpec((pl.Element(tm), pl.Element(rest)), lambda i: (i * tm, FFN_MAIN + FFN_BACK)),
            pl.BlockSpec((pl.Element(FFN_MAIN), pl.Element(D_MODEL)), lambda i: (0, 0), pipeline_mode=once),
            pl.BlockSpec((pl.Element(rest), pl.Element(D_MODEL)), lambda i: (FFN_MAIN, 0), pipeline_mode=once),
            pl.BlockSpec((tm, D_MODEL), lambda i: (i, 0)),
            pl.BlockSpec((1, D_MODEL), lambda i: (0, 0)),
        ],
        out_specs=pl.BlockSpec((tm, D_MODEL), lambda i: (i, 0)),
        out_shape=jax.ShapeDtypeStruct((t, D_MODEL), F32),
        compiler_params=_cparams(("arbitrary",)),
        name="ffn_out",
    )(act, act, wd, wd, x1, g)


def _layer(x, p, final_g, t):
    offs = np.cumsum((0,) + IN_SIZES)
    w_in = p["w_in"]
    wg = w_in[:, offs[7]:offs[8]].astype(BF16).reshape(D_MODEL, N_KV, HEADS_PER_KV * 3)
    wg = jnp.pad(wg, ((0, 0), (0, 0), (0, LANES - HEADS_PER_KV * 3))).reshape(D_MODEL, N_KV * LANES)
    o16, otail, okvc, ng = _in_proj(x, p["norm1_g"][None, :], w_in[:, :offs[7]].astype(BF16),
                                    w_in[:, offs[8]:].astype(BF16), wg)

    pe = jnp.stack([p["cmp_k_pe"], p["cmp_v_pe"]])
    w1 = jnp.stack([p["cmp_k_w1"], p["cmp_v_w1"]]).reshape(2, CMP_BLOCK, HEAD_DIM, HEAD_DIM).astype(BF16)
    w2 = jnp.stack([p["cmp_k_w2"], p["cmp_v_w2"]]).astype(BF16)
    kvc = _compress(okvc, 0, pe, w1, w2, t)

    y_a = _nsa(o16, kvc, ng, t)
    bs = jnp.broadcast_to(p["sgu_b"][:, :, None], (SGU_GROUPS, SGU_CHUNK, LANES))
    y_b = _sgu(otail, p["sgu_ln_g"][None, :], p["sgu_ln_b"][None, :], p["sgu_w"], bs, t)
    x1, h2 = _mix(y_a, y_b, otail, x, p["w_branch_a"].astype(BF16), p["w_branch_b"].astype(BF16),
                  p["w_out"].astype(BF16), p["norm2_g"][None, :], t)

    cw = jnp.pad(p["ffn_conv_w"], ((0, SUBLANES - CONV_WIDTH), (0, 0)))
    act = _ffn_in(h2, p["ffn_w_in"], cw, p["ffn_conv_b"][None, :], t)
    wd = p["ffn_w_down"].astype(BF16)
    g = final_g[None, :] if final_g is not None else jnp.ones((1, D_MODEL), F32)
    return _ffn_out(act, wd, x1, g, t, final_g is not None)


def kernel(x, norm1_g, w_in, cmp_k_pe, cmp_k_w1, cmp_k_w2, cmp_v_pe, cmp_v_w1, cmp_v_w2, sgu_ln_g, sgu_ln_b, sgu_w, sgu_b, w_branch_a, w_branch_b, w_out, norm2_g, ffn_w_in, ffn_conv_w, ffn_conv_b, ffn_w_down, final_g):
    b, t, _ = x.shape
    assert b == 1 and t % 1024 == 0, "one sequence whose length is a multiple of 1024"
    params = dict(norm1_g=norm1_g, w_in=w_in, cmp_k_pe=cmp_k_pe, cmp_k_w1=cmp_k_w1, cmp_k_w2=cmp_k_w2,
                  cmp_v_pe=cmp_v_pe, cmp_v_w1=cmp_v_w1, cmp_v_w2=cmp_v_w2, sgu_ln_g=sgu_ln_g, sgu_ln_b=sgu_ln_b,
                  sgu_w=sgu_w, sgu_b=sgu_b, w_branch_a=w_branch_a, w_branch_b=w_branch_b, w_out=w_out,
                  norm2_g=norm2_g, ffn_w_in=ffn_w_in, ffn_conv_w=ffn_conv_w, ffn_conv_b=ffn_conv_b,
                  ffn_w_down=ffn_w_down)
    depth = norm1_g.shape[0]
    h = x[0]
    for l in range(depth):
        layer = {k: v[l] for k, v in params.items()}
        h = _layer(h, layer, final_g if l == depth - 1 else None, t)
    return h[None]
```

```python
import functools

import numpy as np
import jax
import jax.numpy as jnp
from jax import lax
from jax.experimental import pallas as pl
from jax.experimental.pallas import tpu as pltpu

F32 = jnp.float32
BF16 = jnp.bfloat16

D_MODEL = 2048
N_HEADS = 8
HEAD_DIM = 128
N_KV = 2
HEADS_PER_KV = N_HEADS // N_KV
CMP_BLOCK = 32
CMP_STRIDE = 16
SEL_BLOCK = 64
N_SELECT = 16
N_FORCED = 3
WINDOW = 512
Q_BLOCK = 128
SGU_WIDTH = 1024
SGU_GROUPS = 8
SGU_GROUP_DIM = SGU_WIDTH // SGU_GROUPS
SGU_CHUNK = 128
D_FF = 5504
CONV_WIDTH = 3
NORM_EPS = 1e-6
LN_EPS = 1e-5
NEG_INF = -1e30

Q_W = N_HEADS * HEAD_DIM
KV_W = N_KV * HEAD_DIM
NSA_GATE_W = 3 * N_HEADS
IN_SIZES = (Q_W, KV_W, KV_W, KV_W, KV_W, KV_W, KV_W, NSA_GATE_W, SGU_WIDTH, SGU_WIDTH, D_MODEL, D_MODEL)

LANES = 128
SUBLANES = 8
QROWS = HEADS_PER_KV * Q_BLOCK
SEL_SHIFT = SEL_BLOCK.bit_length() - 1
BLOCKS_PER_Q = Q_BLOCK // SEL_BLOCK
SLC_BLOCKS = 10
SLC_CHUNK = SLC_BLOCKS * SEL_BLOCK
WIN_KEYS = WINDOW + Q_BLOCK
FFN_TN = 512
VMEM_LIMIT = 56 * 1024 * 1024


def _cparams(sem):
    return pltpu.CompilerParams(dimension_semantics=sem, vmem_limit_bytes=VMEM_LIMIT)


def _dot(a, b):
    return jnp.dot(a, b, preferred_element_type=F32)


def _dot_nt(a, b):
    return lax.dot_general(a, b, (((1,), (1,)), ((), ())), preferred_element_type=F32)


def _rms(x, g):
    return x * lax.rsqrt(jnp.mean(x * x, -1, keepdims=True) + NORM_EPS) * g


TAIL_TN = 2048
HEAD_W = Q_W + 6 * KV_W
HEAD_TN = HEAD_W // 2


def _proj_tail_kernel(x_ref, g_ref, w_ref, o_ref, h_ref):
    @pl.when(pl.program_id(1) == 0)
    def _():
        h_ref[...] = _rms(x_ref[...], g_ref[...]).astype(BF16)

    o_ref[...] = _dot(h_ref[...], w_ref[...]).astype(BF16)


def _proj_head_kernel(h_ref, w_ref, wg_ref, cs_ref, o16_ref, o32_ref, ong_ref):
    @pl.when(pl.program_id(1) == 0)
    def _():
        ong_ref[...] = _dot(h_ref[...], wg_ref[...])

    acc = _dot(h_ref[...], w_ref[...])
    o32_ref[...] = acc
    o16_ref[...] = (acc * cs_ref[...]).astype(BF16)


def _in_proj(x, g, w_head, w_tail, wg):
    t = x.shape[0]
    tm = min(1024, t)
    otail, h = pl.pallas_call(
        _proj_tail_kernel,
        grid=(t // tm, w_tail.shape[1] // TAIL_TN),
        in_specs=[
            pl.BlockSpec((tm, D_MODEL), lambda i, j: (i, 0)),
            pl.BlockSpec((1, D_MODEL), lambda i, j: (0, 0)),
            pl.BlockSpec((D_MODEL, TAIL_TN), lambda i, j: (0, j)),
        ],
        out_specs=[
            pl.BlockSpec((tm, TAIL_TN), lambda i, j: (i, j)),
            pl.BlockSpec((tm, D_MODEL), lambda i, j: (i, 0)),
        ],
        out_shape=[jax.ShapeDtypeStruct((t, w_tail.shape[1]), BF16), jax.ShapeDtypeStruct((t, D_MODEL), BF16)],
        compiler_params=_cparams(("arbitrary", "arbitrary")),
        name="proj_tail",
    )(x, g, w_tail)
    cs = jnp.concatenate([jnp.full((1, Q_W), HEAD_DIM ** -0.5, F32), jnp.ones((1, HEAD_W - Q_W), F32)], axis=1)
    o16, o32, ng = pl.pallas_call(
        _proj_head_kernel,
        grid=(t // tm, HEAD_W // HEAD_TN),
        in_specs=[
            pl.BlockSpec((tm, D_MODEL), lambda i, j: (i, 0)),
            pl.BlockSpec((D_MODEL, HEAD_TN), lambda i, j: (0, j)),
            pl.BlockSpec((D_MODEL, 2 * LANES), lambda i, j: (0, 0)),
            pl.BlockSpec((1, HEAD_TN), lambda i, j: (0, j)),
        ],
        out_specs=[
            pl.BlockSpec((tm, HEAD_TN), lambda i, j: (i, j)),
            pl.BlockSpec((tm, HEAD_TN), lambda i, j: (i, j)),
            pl.BlockSpec((tm, 2 * LANES), lambda i, j: (i, 0)),
        ],
        out_shape=[
            jax.ShapeDtypeStruct((t, HEAD_W), BF16),
            jax.ShapeDtypeStruct((t, HEAD_W), F32),
            jax.ShapeDtypeStruct((t, 2 * LANES), F32),
        ],
        compiler_params=_cparams(("arbitrary", "arbitrary")),
        name="proj_head",
    )(h, w_head, wg, cs)
    return o16, otail, o32, ng


def _compress_kernel(kv_ref, pe_ref, w1_ref, w2_ref, o_ref):
    nu = o_ref.shape[0]
    a = jnp.zeros((nu, HEAD_DIM), F32)
    b = jnp.zeros((nu, HEAD_DIM), F32)
    for r in range(CMP_STRIDE):
        x = kv_ref[pl.ds(r, nu, stride=CMP_STRIDE), :]
        a += _dot((x + pe_ref[pl.ds(r, 1), :]).astype(BF16), w1_ref[r])
        b += _dot((x + pe_ref[pl.ds(CMP_STRIDE + r, 1), :]).astype(BF16), w1_ref[CMP_STRIDE + r])
    hid = a + pltpu.roll(b, nu - 1, 0)
    o_ref[...] = _dot(jax.nn.gelu(hid).astype(BF16), w2_ref[...]).astype(BF16)


def _compress(o32, col0, pe, w1, w2, t):
    nu = t // CMP_STRIDE
    return pl.pallas_call(
        _compress_kernel,
        grid=(2, N_KV),
        in_specs=[
            pl.BlockSpec((t, HEAD_DIM), lambda a, g: (0, col0 + N_KV * a + g)),
            pl.BlockSpec((None, CMP_BLOCK, HEAD_DIM), lambda a, g: (a, 0, 0)),
            pl.BlockSpec((None, CMP_BLOCK, HEAD_DIM, HEAD_DIM), lambda a, g: (a, 0, 0, 0)),
            pl.BlockSpec((None, HEAD_DIM, HEAD_DIM), lambda a, g: (a, 0, 0)),
        ],
        out_specs=pl.BlockSpec((None, None, nu, HEAD_DIM), lambda a, g: (a, g, 0, 0)),
        out_shape=jax.ShapeDtypeStruct((2, N_KV, nu, HEAD_DIM), BF16),
        compiler_params=_cparams(("arbitrary", "arbitrary")),
        name="compress",
    )(o32, pe, w1, w2)


def _lane_tile(x, n):
    return jnp.concatenate([x] * n, axis=1)


def _row_tile(x, n):
    return jnp.concatenate([x] * n, axis=0)


def _softmax_numerators(s):
    return jnp.exp(s - jnp.max(s, axis=1, keepdims=True)).astype(BF16)


def _stack_heads(q, g):
    return jnp.concatenate([q[:, (g * HEADS_PER_KV + h) * HEAD_DIM:(g * HEADS_PER_KV + h + 1) * HEAD_DIM]
                            for h in range(HEADS_PER_KV)], axis=0)


def _nsa_select_kernel(q_ref, kvc_ref, bc_ref, sj_ref, slr_ref, ov_ref, lt_ref, ocmp_ref, aug_ref, lists_ref):
    i = pl.program_id(0)
    t0 = i * Q_BLOCK
    t0f = t0.astype(F32)
    q = q_ref[...]
    nc = kvc_ref.shape[2]
    row = lax.broadcasted_iota(jnp.int32, (QROWS, 1), 0) & (Q_BLOCK - 1)
    jr = lax.broadcasted_iota(jnp.int32, (LANES, Q_BLOCK), 0)
    kl = lax.broadcasted_iota(jnp.int32, (LANES, Q_BLOCK), 1)
    tq = t0 + kl
    cur = lax.shift_right_logical(tq, SEL_SHIFT)
    forced = (jr == 0) | (jr == cur) | (jr == cur - 1)
    visible = jr * SEL_BLOCK <= tq
    jf = jr.astype(F32)

    ocmp, imps = [], []
    for g in range(N_KV):
        bc = bc_ref[g]
        s = jnp.where(bc <= _lane_tile(slr_ref[g] * t0f, nc // LANES),
                      _dot_nt(_stack_heads(q, g), kvc_ref[0, g]) + bc, NEG_INF)
        r = _dot(_softmax_numerators(s), jnp.concatenate([kvc_ref[1, g], ov_ref[...]], axis=1))
        l = jnp.sum(r[:, HEAD_DIM:], axis=1, keepdims=True) * (1.0 / CMP_BLOCK)
        inv = jnp.where(t0 + row >= CMP_BLOCK - 1, 1.0 / l, 0.0)
        o = r[:, :HEAD_DIM] * inv
        ocmp += [o[h * Q_BLOCK:(h + 1) * Q_BLOCK] for h in range(HEADS_PER_KV)]
        imp = r[:, HEAD_DIM:] * inv
        imps.append(sum(imp[h * Q_BLOCK:(h + 1) * Q_BLOCK] for h in range(HEADS_PER_KV)).T)
    ocmp_ref[...] = jnp.concatenate(ocmp, axis=1)

    forced2, visible2, jf2 = (_lane_tile(a, N_KV) for a in (forced, visible, jf))
    score = jnp.where(forced2 | jnp.logical_not(visible2), NEG_INF, jnp.concatenate(imps, axis=1))
    for _ in range(N_SELECT - N_FORCED):
        mx = jnp.max(score, axis=0, keepdims=True)
        idx = jnp.min(jnp.where(score == mx, jf2, float(LANES)), axis=0, keepdims=True)
        score = jnp.where(jf2 == idx, -jnp.inf, score)
    sel_all = jnp.where(visible2 & (forced2 | (score == -jnp.inf)), 1.0, 0.0)

    rows = []
    for g in range(N_KV):
        sel_t = sel_all[:, g * Q_BLOCK:(g + 1) * Q_BLOCK]
        aug_ref[g] = jnp.where(_row_tile(sel_t.T, HEADS_PER_KV) > 0.5, sj_ref[g], NEG_INF).astype(BF16)

        used = jnp.max(jnp.where(jr < BLOCKS_PER_Q * i, sel_t, 0.0), axis=1, keepdims=True)
        used = jnp.broadcast_to(used, (LANES, Q_BLOCK))
        slot = _dot(lt_ref[...], used.astype(BF16))
        lst = jnp.sum(jnp.where((slot == kl.astype(F32)) & (used > 0.5), jf, 0.0), axis=0, keepdims=True)
        cnt = jnp.sum(used, axis=0, keepdims=True)
        lst = jnp.where(kl[:1].astype(F32) < cnt, lst, (BLOCKS_PER_Q * i).astype(F32))
        rows += [lst, cnt]
    pad =[jnp.zeros((1, LANES), F32)] * (SUBLANES - len(rows))
    lists_ref[...] = jnp.concatenate(rows[0::2] + rows[1::2] + pad, axis=0).astype(jnp.int32)


def _nsa_attend_kernel(lists_ref, counts_ref, q_ref, ks_ref, vs_ref, kw_ref, vw_ref, aug_ref, ocmp_ref, ng_ref,
                       bw_ref, b1_ref, b1d_ref, et_ref, o_ref):
    i = pl.program_id(0)
    nqb = pl.num_programs(0)
    t0 = i * Q_BLOCK
    q = q_ref[...]
    q4 = [_stack_heads(q, g) for g in range(N_KV)]
    ones_col = jnp.ones((WIN_KEYS, LANES), BF16)
    gcols = lambda g: slice(g * HEAD_DIM, (g + 1) * HEAD_DIM)
    grows = lambda a, g: a[g * QROWS:(g + 1) * QROWS]
    both = lambda f: jnp.concatenate([f(g) for g in range(N_KV)], axis=0)

    w0 = pl.multiple_of(jnp.maximum(t0 - WINDOW, 0), Q_BLOCK)
    shift = pl.multiple_of(w0 - (t0 - WINDOW), Q_BLOCK)
    s = both(lambda g: _dot_nt(q4[g], kw_ref[pl.ds(w0, WIN_KEYS), gcols(g)])) + bw_ref[:, pl.ds(shift, WIN_KEYS)]
    e = _softmax_numerators(s)
    r = both(lambda g: _dot(grows(e, g), jnp.concatenate([vw_ref[pl.ds(w0, WIN_KEYS), gcols(g)], ones_col], axis=1)))
    o_win = r[:, :HEAD_DIM] * (1.0 / r[:, HEAD_DIM:])

    gl = lax.broadcasted_iota(jnp.int32, (Q_BLOCK, LANES), 1)
    o_cmp = ocmp_ref[...]
    slc_gate, partial = [], []
    for g in range(N_KV):
        sg = jax.nn.sigmoid(ng_ref[:, gcols(g)])
        gate = lambda col: jnp.sum(jnp.where(gl == col, sg, 0.0), axis=1, keepdims=True)
        for h in range(HEADS_PER_KV):
            hh = g * HEADS_PER_KV + h
            slc_gate.append(gate(3 * h + 1))
            partial.append(gate(3 * h) * o_cmp[:, hh * HEAD_DIM:(hh + 1) * HEAD_DIM]
                           + gate(3 * h + 2) * o_win[hh * Q_BLOCK:(hh + 1) * Q_BLOCK])

    jl = lax.broadcasted_iota(jnp.int32, (QROWS, LANES), 1)
    qx = [jnp.concatenate([q4[g], aug_ref[g]], axis=1) for g in range(N_KV)]
    qx_past = [jnp.concatenate([q4[g], jnp.where(jl < BLOCKS_PER_Q * i, aug_ref[g], NEG_INF).astype(BF16)], axis=1)
               for g in range(N_KV)]
    b1 = _lane_tile(b1_ref[...], SLC_CHUNK // LANES)

    def gathered(ref, g, first, cols):
        parts = []
        for k in range(SLC_BLOCKS):
            slot = jnp.minimum(first + k, LANES - 1)
            k0 = pl.multiple_of(lists_ref[g * nqb + i, slot] * SEL_BLOCK, SEL_BLOCK)
            parts.append(ref[pl.ds(k0, SEL_BLOCK), cols])
        return jnp.concatenate(parts, axis=0)

    def slc_step(c, carry):
        m, acc = carry
        first = c * SLC_BLOCKS
        u = both(lambda g: _dot_nt(qx_past[g], jnp.concatenate(
            [gathered(ks_ref, g, first, gcols(g)), gathered(et_ref, g, first, slice(None))], axis=1))) + b1
        m_new = jnp.maximum(m, jnp.max(u, axis=1, keepdims=True))
        e = jnp.exp(u - m_new).astype(BF16)
        pv = both(lambda g: _dot(grows(e, g), jnp.concatenate(
            [gathered(vs_ref, g, first, gcols(g)), ones_col[:SLC_CHUNK]], axis=1)))
        return m_new, jnp.exp(m - m_new) * acc + pv

    n_blocks = jnp.maximum(counts_ref[i], counts_ref[nqb + i])
    n_trips = (n_blocks + (SLC_BLOCKS - 1)) // SLC_BLOCKS
    m, acc = lax.fori_loop(0, n_trips, slc_step,
                           (jnp.full((N_KV * QROWS, 1), NEG_INF, F32), jnp.zeros((N_KV * QROWS, 2 * HEAD_DIM), F32)))
    u = both(lambda g: _dot_nt(qx[g], jnp.concatenate(
        [ks_ref[pl.ds(t0, Q_BLOCK), gcols(g)], et_ref[pl.ds(t0, Q_BLOCK), :]], axis=1))) + b1d_ref[...]
    m_new = jnp.maximum(m, jnp.max(u, axis=1, keepdims=True))
    e = jnp.exp(u - m_new).astype(BF16)
    pv = both(lambda g: _dot(grows(e, g), jnp.concatenate(
        [vs_ref[pl.ds(t0, Q_BLOCK), gcols(g)], ones_col[:Q_BLOCK]], axis=1)))
    acc = jnp.exp(m - m_new) * acc + pv
    o_slc = acc[:, :HEAD_DIM] * (1.0 / acc[:, HEAD_DIM:])

    outs = [partial[hh] + slc_gate[hh] * o_slc[hh * Q_BLOCK:(hh + 1) * Q_BLOCK] for hh in range(N_HEADS)]
    o_ref[...] = jnp.concatenate(outs, axis=1).astype(BF16)


def _nsa_tables(t):
    nc = t // CMP_STRIDE
    f32 = np.float32
    r = (np.arange(QROWS) % Q_BLOCK)[:, None].astype(f32)
    slopes = np.exp2(-8.0 * np.arange(1, N_HEADS + 1, dtype=f32) / N_HEADS).astype(f32)
    slr = np.repeat(slopes.reshape(N_KV, HEADS_PER_KV), Q_BLOCK, axis=1)[:, :, None]
    bc = slr * ((np.arange(nc) * CMP_STRIDE + CMP_BLOCK - 1)[None, None, :] - r[None])
    dw = r - np.arange(WIN_KEYS)[None, :] + WINDOW
    bw = np.where((dw >= 0) & (dw < WINDOW), -slr * dw[None], f32(NEG_INF))
    bw = np.concatenate([bw, np.full(bw.shape[:2] + (WINDOW,), NEG_INF, f32)], axis=2)
    kk = np.arange(LANES)[None, :]
    b1 = slr * (kk % SEL_BLOCK)[None]
    b1d = np.where(kk <= r, b1, f32(NEG_INF))
    sj = slr * (SEL_BLOCK * kk)[None]
    et = (np.arange(t)[:, None] // SEL_BLOCK == kk)
    cmp_start = np.arange(nc) * CMP_STRIDE
    sel_start = np.arange(LANES) * SEL_BLOCK
    ov = np.clip(np.minimum(cmp_start[:, None] + CMP_BLOCK - 1, sel_start[None] + SEL_BLOCK - 1)
                 - np.maximum(cmp_start[:, None], sel_start[None]) + 1, 0, None)
    ov[nc - 1] = 0
    lt = np.tril(np.ones((LANES, LANES)), -1)
    slr = slr * np.ones((1, 1, LANES), f32)
    as_f32 = lambda a: jnp.asarray(a.astype(f32))
    as_bf16 = lambda a: jnp.asarray(a.astype(f32), BF16)
    return dict(bc=as_f32(bc), bw=as_f32(bw), b1=as_f32(b1), b1d=as_f32(b1d), sj=as_f32(sj), slr=as_f32(slr),
                et=as_bf16(et), ov=as_bf16(ov), lt=as_bf16(lt))


def _nsa(o16, kvc, ng, t):
    tb = _nsa_tables(t)
    nc = t // CMP_STRIDE
    nqb = t // Q_BLOCK
    whole = lambda a: pl.BlockSpec(a.shape, lambda i: (0,) * a.ndim)
    ocmp, aug, lists = pl.pallas_call(
        _nsa_select_kernel,
        grid=(nqb,),
        in_specs=[
            pl.BlockSpec((Q_BLOCK, Q_W), lambda i: (i, 0)),
            whole(kvc), whole(tb["bc"]), whole(tb["sj"]), whole(tb["slr"]), whole(tb["ov"]), whole(tb["lt"]),
        ],
        out_specs=[
            pl.BlockSpec((Q_BLOCK, Q_W), lambda i: (i, 0)),
            pl.BlockSpec((N_KV, QROWS, LANES), lambda i: (0, i, 0)),
            pl.BlockSpec((None, SUBLANES, LANES), lambda i: (i, 0, 0)),
        ],
        out_shape=[
            jax.ShapeDtypeStruct((t, Q_W), F32),
            jax.ShapeDtypeStruct((N_KV, nqb * QROWS, LANES), BF16),
            jax.ShapeDtypeStruct((nqb, SUBLANES, LANES), jnp.int32),
        ],
        compiler_params=_cparams(("arbitrary",)),
        name="nsa_select",
    )(o16, kvc, tb["bc"], tb["sj"], tb["slr"], tb["ov"], tb["lt"])
    block_lists = lists[:, :N_KV, :].transpose(1, 0, 2).reshape(N_KV * nqb, LANES)
    counts = lists[:, N_KV:2 * N_KV, 0].T.reshape(N_KV * nqb)

    stacked = lambda a: a.reshape((N_KV * QROWS,) + a.shape[2:])
    bw, b1, b1d = stacked(tb["bw"]), stacked(tb["b1"]), stacked(tb["b1d"])
    once = pl.Buffered(1)
    const = lambda a: pl.BlockSpec(a.shape, lambda i, *_: (0,) * a.ndim, pipeline_mode=once)
    grid_spec = pltpu.PrefetchScalarGridSpec(
        num_scalar_prefetch=2,
        grid=(nqb,),
        in_specs=[
            pl.BlockSpec((Q_BLOCK, Q_W), lambda i, *_: (i, 0)),
            pl.BlockSpec((t, KV_W), lambda i, *_: (0, Q_W // KV_W + 2), pipeline_mode=once),
            pl.BlockSpec((t, KV_W), lambda i, *_: (0, Q_W // KV_W + 3), pipeline_mode=once),
            pl.BlockSpec((t, KV_W), lambda i, *_: (0, Q_W // KV_W + 4), pipeline_mode=once),
            pl.BlockSpec((t, KV_W), lambda i, *_: (0, Q_W // KV_W + 5), pipeline_mode=once),
            pl.BlockSpec((N_KV, QROWS, LANES), lambda i, *_: (0, i, 0)),
            pl.BlockSpec((Q_BLOCK, Q_W), lambda i, *_: (i, 0)),
            pl.BlockSpec((Q_BLOCK, N_KV * LANES), lambda i, *_: (i, 0)),
            const(bw), const(b1), const(b1d), const(tb["et"]),
        ],
        out_specs=pl.BlockSpec((Q_BLOCK, Q_W), lambda i, *_: (i, 0)),
    )
    return pl.pallas_call(
        _nsa_attend_kernel,
        grid_spec=grid_spec,
        out_shape=jax.ShapeDtypeStruct((t, Q_W), BF16),
        compiler_params=_cparams(("arbitrary",)),
        name="nsa_attend",
    )(block_lists, counts, o16, o16, o16, o16, o16, aug, ocmp, ng, bw, b1, b1d, tb["et"])


SGU_CHUNKS_PER_STEP = 4


def _sgu_kernel(u_ref, v_ref, lng_ref, lnb_ref, ws_ref, bs_ref, o_ref):
    gu = jax.nn.gelu(u_ref[...].astype(F32))
    gv = jax.nn.gelu(v_ref[...].astype(F32))
    xc = gv - jnp.mean(gv, -1, keepdims=True)
    vn = (xc * lax.rsqrt(jnp.mean(xc * xc, -1, keepdims=True) + LN_EPS) * lng_ref[...] + lnb_ref[...]).astype(BF16)
    causal = (lax.broadcasted_iota(jnp.int32, (SGU_CHUNK, SGU_CHUNK), 0)
              >= lax.broadcasted_iota(jnp.int32, (SGU_CHUNK, SGU_CHUNK), 1))
    w = [jnp.where(causal, ws_ref[g], 0.0).astype(BF16) for g in range(SGU_GROUPS)]
    rows = []
    for c in range(SGU_CHUNKS_PER_STEP):
        vc = vn[c * SGU_CHUNK:(c + 1) * SGU_CHUNK]
        rows.append(jnp.concatenate(
            [_dot(w[g], vc[:, g * SGU_GROUP_DIM:(g + 1) * SGU_GROUP_DIM]) + bs_ref[g] for g in range(SGU_GROUPS)],
            axis=1))
    o_ref[...] = (gu * jnp.concatenate(rows, axis=0)).astype(BF16)


def _sgu(otail, lng, lnb, ws, bs, t):
    tm = SGU_CHUNKS_PER_STEP * SGU_CHUNK
    return pl.pallas_call(
        _sgu_kernel,
        grid=(t // tm,),
        in_specs=[
            pl.BlockSpec((tm, SGU_WIDTH), lambda i: (i, 0)),
            pl.BlockSpec((tm, SGU_WIDTH), lambda i: (i, 1)),
            pl.BlockSpec((1, SGU_WIDTH), lambda i: (0, 0)),
            pl.BlockSpec((1, SGU_WIDTH), lambda i: (0, 0)),
            pl.BlockSpec((SGU_GROUPS, SGU_CHUNK, SGU_CHUNK), lambda i: (0, 0, 0)),
            pl.BlockSpec((SGU_GROUPS, SGU_CHUNK, LANES), lambda i: (0, 0, 0)),
        ],
        out_specs=pl.BlockSpec((tm, SGU_WIDTH), lambda i: (i, 0)),
        out_shape=jax.ShapeDtypeStruct((t, SGU_WIDTH), BF16),
        compiler_params=_cparams(("arbitrary",)),
        name="sgu",
    )(otail, otail, lng, lnb, ws, bs)


def _mix_kernel(ya_ref, yb_ref, ga_ref, gb_ref, x_ref, pa_ref, pb_ref, wo_ref, g2_ref, x1_ref, h2_ref):
    mixed = (jax.nn.sigmoid(ga_ref[...].astype(F32)) * _dot(ya_ref[...], pa_ref[...])
             + jax.nn.sigmoid(gb_ref[...].astype(F32)) * _dot(yb_ref[...], pb_ref[...]))
    x1 = x_ref[...] + _dot(mixed.astype(BF16), wo_ref[...])
    x1_ref[...] = x1
    h2_ref[...] = _rms(x1, g2_ref[...]).astype(BF16)


def _mix(ya, yb, o32, x, pa, pb, wo, g2, t):
    tm = min(512, t)
    once = pl.Buffered(1)
    return pl.pallas_call(
        _mix_kernel,
        grid=(t // tm,),
        in_specs=[
            pl.BlockSpec((tm, Q_W), lambda i: (i, 0)),
            pl.BlockSpec((tm, SGU_WIDTH), lambda i: (i, 0)),
            pl.BlockSpec((tm, D_MODEL), lambda i: (i, 1)),
            pl.BlockSpec((tm, D_MODEL), lambda i: (i, 2)),
            pl.BlockSpec((tm, D_MODEL), lambda i: (i, 0)),
            pl.BlockSpec((Q_W, D_MODEL), lambda i: (0, 0), pipeline_mode=once),
            pl.BlockSpec((SGU_WIDTH, D_MODEL), lambda i: (0, 0), pipeline_mode=once),
            pl.BlockSpec((D_MODEL, D_MODEL), lambda i: (0, 0), pipeline_mode=once),
            pl.BlockSpec((1, D_MODEL), lambda i: (0, 0)),
        ],
        out_specs=[
            pl.BlockSpec((tm, D_MODEL), lambda i: (i, 0)),
            pl.BlockSpec((tm, D_MODEL), lambda i: (i, 0)),
        ],
        out_shape=[jax.ShapeDtypeStruct((t, D_MODEL), F32), jax.ShapeDtypeStruct((t, D_MODEL), BF16)],
        compiler_params=_cparams(("arbitrary",)),
        name="mix_out",
    )(ya, yb, o32, o32, x, pa, pb, wo, g2)


FFN_TILES = pl.cdiv(D_FF, FFN_TN)
FFN_BACK = FFN_TILES * FFN_TN - D_FF
FFN_MAIN = (FFN_TILES - 1) * FFN_TN


def _ffn_col(j, base=0):
    return (base // LANES + j * (FFN_TN // LANES) - (j // (FFN_TILES - 1)) * (FFN_BACK // LANES)) * LANES


def _ffn_in_kernel(h_ref, wa_ref, wb_ref, cw_ref, cb_ref, o_ref, wa_scr, wb_scr, a_scr):
    tm = h_ref.shape[0]

    @pl.when(pl.program_id(1) == 0)
    def _():
        wa_scr[...] = wa_ref[...].astype(BF16)
        wb_scr[...] = wb_ref[...].astype(BF16)
        a_scr[0:SUBLANES, :] = jnp.zeros((SUBLANES, FFN_TN), F32)

    h = h_ref[...]
    a = _dot(h, wa_scr[...])
    b = _dot(h, wb_scr[...])
    a_scr[SUBLANES:SUBLANES + tm, :] = a
    cw = cw_ref[...]
    y = (cw[0:1] * a_scr[pl.ds(SUBLANES - 2, tm), :] + cw[1:2] * a_scr[pl.ds(SUBLANES - 1, tm), :]
         + cw[2:3] * a + cb_ref[...])
    o_ref[...] = (jax.nn.gelu(y) * b).astype(BF16)
    a_scr[0:SUBLANES, :] = a[tm - SUBLANES:tm, :]


def _ffn_in(h2, w, cw, cb, t):
    tm = min(1024, t)
    tile = lambda rows, base: pl.BlockSpec((pl.Element(rows), pl.Element(FFN_TN)),
                                           lambda j, i: (0, _ffn_col(j, base)))
    return pl.pallas_call(
        _ffn_in_kernel,
        grid=(FFN_TILES, t // tm),
        in_specs=[
            pl.BlockSpec((tm, D_MODEL), lambda j, i: (i, 0)),
            tile(D_MODEL, 0),
            tile(D_MODEL, D_FF),
            tile(SUBLANES, 0),
            tile(1, 0),
        ],
        out_specs=pl.BlockSpec((tm, FFN_TN), lambda j, i: (i, j)),
        out_shape=jax.ShapeDtypeStruct((t, FFN_TILES * FFN_TN), BF16),
        scratch_shapes=[pltpu.VMEM((D_MODEL, FFN_TN), BF16), pltpu.VMEM((D_MODEL, FFN_TN), BF16),
                        pltpu.VMEM((tm + SUBLANES, FFN_TN), F32)],
        compiler_params=_cparams(("arbitrary", "arbitrary")),
        name="ffn_in",
    )(h2, w, w, cw, cb)


def _ffn_out_kernel(a0_ref, a1_ref, w0_ref, w1_ref, x_ref, g_ref, o_ref, *, final_norm):
    x2 = x_ref[...] + _dot(a0_ref[...], w0_ref[...]) + _dot(a1_ref[...], w1_ref[...])
    o_ref[...] = _rms(x2, g_ref[...]) if final_norm else x2


def _ffn_out(act, wd, x1, g, t, final_norm):
    tm = min(512, t)
    once = pl.Buffered(1)
    rest = D_FF - FFN_MAIN
    return pl.pallas_call(
        functools.partial(_ffn_out_kernel, final_norm=final_norm),
        grid=(t // tm,),
        in_specs=[
            pl.BlockSpec((pl.Element(tm), pl.Element(FFN_MAIN)), lambda i: (i * tm, 0)),
            pl.BlockSpec((pl.Element(tm), pl.Element(rest)), lambda i: (i * tm, FFN_MAIN + FFN_BACK)),
            pl.BlockSpec((pl.Element(FFN_MAIN), pl.Element(D_MODEL)), lambda i: (0, 0), pipeline_mode=once),
            pl.BlockSpec((pl.Element(rest), pl.Element(D_MODEL)), lambda i: (FFN_MAIN, 0), pipeline_mode=once),
            pl.BlockSpec((tm, D_MODEL), lambda i: (i, 0)),
            pl.BlockSpec((1, D_MODEL), lambda i: (0, 0)),
        ],
        out_specs=pl.BlockSpec((tm, D_MODEL), lambda i: (i, 0)),
        out_shape=jax.ShapeDtypeStruct((t, D_MODEL), F32),
        compiler_params=_cparams(("arbitrary",)),
        name="ffn_out",
    )(act, act, wd, wd, x1, g)


def _layer(x, p, final_g, t):
    offs = np.cumsum((0,) + IN_SIZES)
    w_in = p["w_in"]
    wg = w_in[:, offs[7]:offs[8]].astype(BF16).reshape(D_MODEL, N_KV, HEADS_PER_KV * 3)
    wg = jnp.pad(wg, ((0, 0), (0, 0), (0, LANES - HEADS_PER_KV * 3))).reshape(D_MODEL, N_KV * LANES)
    o16, otail, okvc, ng = _in_proj(x, p["norm1_g"][None, :], w_in[:, :offs[7]].astype(BF16),
                                    w_in[:, offs[8]:].astype(BF16), wg)

    pe = jnp.stack([p["cmp_k_pe"], p["cmp_v_pe"]])
    w1 = jnp.stack([p["cmp_k_w1"], p["cmp_v_w1"]]).reshape(2, CMP_BLOCK, HEAD_DIM, HEAD_DIM).astype(BF16)
    w2 = jnp.stack([p["cmp_k_w2"], p["cmp_v_w2"]]).astype(BF16)
    kvc = _compress(okvc, Q_W // HEAD_DIM, pe, w1, w2, t)

    y_a = _nsa(o16, kvc, ng, t)
    bs = jnp.broadcast_to(p["sgu_b"][:, :, None], (SGU_GROUPS, SGU_CHUNK, LANES))
    y_b = _sgu(otail, p["sgu_ln_g"][None, :], p["sgu_ln_b"][None, :], p["sgu_w"], bs, t)
    x1, h2 = _mix(y_a, y_b, otail, x, p["w_branch_a"].astype(BF16), p["w_branch_b"].astype(BF16),
                  p["w_out"].astype(BF16), p["norm2_g"][None, :], t)

    cw = jnp.pad(p["ffn_conv_w"], ((0, SUBLANES - CONV_WIDTH), (0, 0)))
    act = _ffn_in(h2, p["ffn_w_in"], cw, p["ffn_conv_b"][None, :], t)
    wd = p["ffn_w_down"].astype(BF16)
    g = final_g[None, :] if final_g is not None else jnp.ones((1, D_MODEL), F32)
    return _ffn_out(act, wd, x1, g, t, final_g is not None)


def kernel(x, norm1_g, w_in, cmp_k_pe, cmp_k_w1, cmp_k_w2, cmp_v_pe, cmp_v_w1, cmp_v_w2, sgu_ln_g, sgu_ln_b, sgu_w, sgu_b, w_branch_a, w_branch_b, w_out, norm2_g, ffn_w_in, ffn_conv_w, ffn_conv_b, ffn_w_down, final_g):
    b, t, _ = x.shape
    assert b == 1 and t % 1024 == 0, "one sequence whose length is a multiple of 1024"
    params = dict(norm1_g=norm1_g, w_in=w_in, cmp_k_pe=cmp_k_pe, cmp_k_w1=cmp_k_w1, cmp_k_w2=cmp_k_w2,
                  cmp_v_pe=cmp_v_pe, cmp_v_w1=cmp_v_w1, cmp_v_w2=cmp_v_w2, sgu_ln_g=sgu_ln_g, sgu_ln_b=sgu_ln_b,
                  sgu_w=sgu_w, sgu_b=sgu_b, w_branch_a=w_branch_a, w_branch_b=w_branch_b, w_out=w_out,
                  norm2_g=norm2_g, ffn_w_in=ffn_w_in, ffn_conv_w=ffn_conv_w, ffn_conv_b=ffn_conv_b,
                  ffn_w_down=ffn_w_down)
    depth = norm1_g.shape[0]
    h = x[0]
    for l in range(depth):
        layer = {k: v[l] for k, v in params.items()}
        h = _layer(h, layer, final_g if l == depth - 1 else None, t)
    return h[None]
```
